```python
import jax, jax.numpy as jnp
from jax import lax
import numpy as np

D_MODEL = 1024
BATCH = 4
SEQ = 4096
DEPTH = 4
DEC_BATCH = 128
DEC_SEQ = 4
PAST_LEN = 8192
PAGE_SIZE = 128

D_MIX = D_MODEL
D_CONV = D_MIX // 4
D_POOL = D_MIX // 4
D_ATTN = D_MIX - D_CONV - D_POOL
HEAD_DIM = 64
N_HEADS = D_ATTN // HEAD_DIM
N_KV_HEADS = 2
GROUP = N_HEADS // N_KV_HEADS
D_KV = N_KV_HEADS * HEAD_DIM
WINDOW = 128
BLOCK = 128
CONV_WIDTH = 31
CONV_BUF = CONV_WIDTH - 1
POOL_WINDOWS = (2, 4, 8, 16)
N_POOL_GROUPS = len(POOL_WINDOWS)
POOL_GROUP_DIM = D_POOL // N_POOL_GROUPS
POOL_BUF = max(POOL_WINDOWS) - 1
ROPE_THETA = 10000.0
LN_EPS = 1e-5
ALPHA = (2.0 * DEPTH) ** 0.25
BETA = (8.0 * DEPTH) ** -0.25
MASK_VALUE = -1e30
IN_SIZES = (D_CONV, D_CONV, D_CONV, D_POOL, D_POOL, D_ATTN, D_KV, D_KV, D_ATTN)
D_IN = sum(IN_SIZES)
IN_SPLITS = tuple(int(s) for s in np.cumsum(IN_SIZES)[:-1])

kernel_name = "hymba_conv_pool_swa_deepnorm_step"


def _layernorm(x, g=None, b=None):
    xf = x.astype(jnp.float32)
    mu = jnp.mean(xf, axis=-1, keepdims=True)
    var = jnp.mean(jnp.square(xf - mu), axis=-1, keepdims=True)
    y = (xf - mu) * lax.rsqrt(var + LN_EPS)
    if g is not None:
        y = y * g.astype(jnp.float32) + b.astype(jnp.float32)
    return y.astype(x.dtype)


def _rope(x, pos):
    half = HEAD_DIM // 2
    inv_freq = ROPE_THETA ** (-jnp.arange(half, dtype=jnp.float32) * (2.0 / HEAD_DIM))
    ang = pos.astype(jnp.float32)[:, None] * inv_freq[None, :]
    cos = jnp.concatenate([jnp.cos(ang), jnp.cos(ang)], -1)[None, :, None, :]
    sin = jnp.concatenate([jnp.sin(ang), jnp.sin(ang)], -1)[None, :, None, :]
    xf = x.astype(jnp.float32)
    rot = jnp.concatenate([-xf[..., half:], xf[..., :half]], -1)
    return (xf * cos + rot * sin).astype(x.dtype)


def _sink_attention(q, k, v, q_pos, k_pos, sinks):
    s = jnp.einsum('...qhgd,...khd->...hgqk', q, k).astype(jnp.float32) * (HEAD_DIM ** -0.5)
    kp = k_pos[..., None, :]
    qp = q_pos[..., :, None]
    mask = (kp <= qp) & (kp > qp - WINDOW) & (kp >= 0)
    s = jnp.where(mask[..., None, None, :, :], s, MASK_VALUE)
    sink = sinks.astype(jnp.float32).reshape(N_KV_HEADS, GROUP, 1)
    m = jnp.maximum(jnp.max(s, axis=-1), sink)
    p = jnp.exp(s - m[..., None])
    denom = jnp.sum(p, axis=-1) + jnp.exp(sink - m)
    probs = (p / denom[..., None]).astype(v.dtype)
    return jnp.einsum('...hgqk,...khd->...qhgd', probs, v)


def _causal_dwconv(buf, a, w, b):
    xcat = jnp.concatenate([buf, a], axis=1)
    out = lax.conv_general_dilated(
        xcat, w.astype(xcat.dtype)[:, None, :], window_strides=(1,), padding='VALID',
        dimension_numbers=('NWC', 'WIO', 'NWC'), feature_group_count=D_CONV)
    return out + b.astype(out.dtype), xcat[:, -CONV_BUF:]


def _pool_mix(buf, v, pos0, pool_w, pool_scale):
    B, T, _ = v.shape
    xcat = jnp.concatenate([buf, v], axis=1)
    cs = jnp.cumsum(xcat.astype(jnp.float32), axis=1)
    cs0 = jnp.concatenate([jnp.zeros((B, 1, D_POOL), jnp.float32), cs], axis=1)
    pos = (pos0 + jnp.arange(T)).astype(jnp.float32)
    end = cs0[:, POOL_BUF + 1:POOL_BUF + 1 + T]
    means = []
    for gi, w in enumerate(POOL_WINDOWS):
        sl = slice(gi * POOL_GROUP_DIM, (gi + 1) * POOL_GROUP_DIM)
        start = cs0[:, POOL_BUF + 1 - w:POOL_BUF + 1 - w + T, sl]
        cnt = jnp.minimum(jnp.float32(w), pos + 1.0)[None, :, None]
        means.append((end[..., sl] - start) / cnt)
    pooled = (jnp.concatenate(means, -1) - v.astype(jnp.float32)).astype(v.dtype)
    y = jnp.einsum('btgc,gcd->btgd', pooled.reshape(B, T, N_POOL_GROUPS, POOL_GROUP_DIM), pool_w)
    return y.reshape(B, T, D_POOL) * pool_scale, xcat[:, -POOL_BUF:]


def _layer(x, c, conv_buf, pool_buf, k_buf, v_buf, pos0,
           w_in, w_out, conv_w, conv_b, cnorm_g, cnorm_b, pool_w, pool_scale,
           sinks, w_mod, b_mod, ln_g, ln_b):
    B, T, _ = x.shape
    mod = jax.nn.silu(c) @ w_mod + b_mod
    shift, scale, gate = jnp.split(mod, 3, axis=-1)
    h = _layernorm(x) * (1.0 + scale[:, None]) + shift[:, None]
    proj = h @ w_in
    u, g, zc, pv, zp, q, k, v, za = jnp.split(proj, IN_SPLITS, axis=-1)

    a = u * jax.nn.sigmoid(g)
    conv, new_conv = _causal_dwconv(conv_buf, a, conv_w, conv_b)
    ya = jax.nn.silu(_layernorm(conv, cnorm_g, cnorm_b)) * jax.nn.silu(zc)

    pooled, new_pool = _pool_mix(pool_buf, pv, pos0, pool_w, pool_scale)
    yb = pooled * jax.nn.silu(zp)

    pos = pos0 + jnp.arange(T, dtype=jnp.int32)
    q = _rope(q.reshape(B, T, N_HEADS, HEAD_DIM), pos).reshape(B, T, N_KV_HEADS, GROUP, HEAD_DIM)
    k = _rope(k.reshape(B, T, N_KV_HEADS, HEAD_DIM), pos)
    v = v.reshape(B, T, N_KV_HEADS, HEAD_DIM)
    if k_buf is None:
        nb = T // BLOCK
        qb = q.reshape(B, nb, BLOCK, N_KV_HEADS, GROUP, HEAD_DIM)
        kb = k.reshape(B, nb, BLOCK, N_KV_HEADS, HEAD_DIM)
        vb = v.reshape(B, nb, BLOCK, N_KV_HEADS, HEAD_DIM)
        kk = jnp.concatenate([jnp.concatenate([jnp.zeros_like(kb[:, :1]), kb[:, :-1]], 1), kb], 2)
        vv = jnp.concatenate([jnp.concatenate([jnp.zeros_like(vb[:, :1]), vb[:, :-1]], 1), vb], 2)
        pos_b = pos.reshape(nb, BLOCK)
        kpos = jnp.concatenate([pos_b - BLOCK, pos_b], axis=1)
        o = _sink_attention(qb, kk, vv, pos_b, kpos, sinks)
        new_k, new_v = k[:, -WINDOW:], v[:, -WINDOW:]
    else:
        wb = k_buf.shape[1]
        kk = jnp.concatenate([k_buf, k], axis=1)
        vv = jnp.concatenate([v_buf, v], axis=1)
        kpos = pos0 - wb + jnp.arange(wb + T, dtype=jnp.int32)
        o = _sink_attention(q, kk, vv, pos, kpos, sinks)
        new_k, new_v = kk[:, -wb:], vv[:, -wb:]
    yc = o.reshape(B, T, D_ATTN) * jax.nn.silu(za)

    mix = jnp.concatenate([ya, yb, yc], axis=-1) @ w_out
    x = _layernorm(ALPHA * x + (1.0 + gate[:, None]) * mix, ln_g, ln_b)
    return x, new_conv, new_pool, new_k, new_v


def setup_inputs(seed: int = 0) -> dict:
    key = jax.random.key(seed)
    ks = jax.random.split(key, 24)
    f = jnp.float32
    wb = min(WINDOW, PAST_LEN)
    n = lambda i, shape, s=1.0: s * jax.random.normal(ks[i], shape, f)
    return {
        'x_prompt': n(0, (BATCH, SEQ, D_MODEL)),
        'x_sample': n(1, (DEC_BATCH, DEC_SEQ, D_MODEL)),
        'cache_conv': n(2, (DEPTH, DEC_BATCH, CONV_BUF, D_CONV), 0.5),
        'cache_pool': n(3, (DEPTH, DEC_BATCH, POOL_BUF, D_POOL)),
        'cache_k': n(4, (DEPTH, DEC_BATCH, wb, N_KV_HEADS, HEAD_DIM)),
        'cache_v': n(5, (DEPTH, DEC_BATCH, wb, N_KV_HEADS, HEAD_DIM)),
        'c_prompt': n(6, (BATCH, D_MODEL)),
        'c_sample': n(7, (DEC_BATCH, D_MODEL)),
        'w_in': n(8, (DEPTH, D_MODEL, D_IN), D_MODEL ** -0.5),
        'w_out': n(9, (DEPTH, D_MIX, D_MODEL), BETA * D_MIX ** -0.5),
        'conv_w': n(10, (DEPTH, CONV_WIDTH, D_CONV), CONV_WIDTH ** -0.5),
        'conv_b': n(11, (DEPTH, D_CONV), 0.02),
        'cnorm_g': 1.0 + n(12, (DEPTH, D_CONV), 0.02),
        'cnorm_b': n(13, (DEPTH, D_CONV), 0.02),
        'pool_w': n(14, (DEPTH, N_POOL_GROUPS, POOL_GROUP_DIM, POOL_GROUP_DIM), POOL_GROUP_DIM ** -0.5),
        'pool_scale': 1.0 + n(15, (DEPTH, D_POOL), 0.1),
        'sinks': n(16, (DEPTH, N_HEADS), 0.5),
        'w_mod': n(17, (DEPTH, D_MODEL, 3 * D_MODEL), 0.5 * D_MODEL ** -0.5),
        'b_mod': n(18, (DEPTH, 3 * D_MODEL), 0.02),
        'ln_g': 1.0 + n(19, (DEPTH, D_MODEL), 0.02),
        'ln_b': n(20, (DEPTH, D_MODEL), 0.02),
    }


def reference(x_prompt, x_sample, cache_conv, cache_pool, cache_k, cache_v, c_prompt, c_sample,
              w_in, w_out, conv_w, conv_b, cnorm_g, cnorm_b, pool_w, pool_scale, sinks,
              w_mod, b_mod, ln_g, ln_b):
    xp, xs = x_prompt, x_sample
    B = xp.shape[0]
    conv_p, pool_p, k_p, v_p = [], [], [], []
    conv_s, pool_s, k_s, v_s = [], [], [], []
    for l in range(DEPTH):
        params = (w_in[l], w_out[l], conv_w[l], conv_b[l], cnorm_g[l], cnorm_b[l],
                  pool_w[l], pool_scale[l], sinks[l], w_mod[l], b_mod[l], ln_g[l], ln_b[l])
        zc = jnp.zeros((B, CONV_BUF, D_CONV), xp.dtype)
        zp = jnp.zeros((B, POOL_BUF, D_POOL), xp.dtype)
        xp, cp, pp, kp, vp = _layer(xp, c_prompt, zc, zp, None, None, 0, *params)
        xs, cs, ps, ksn, vsn = _layer(xs, c_sample, cache_conv[l], cache_pool[l],
                                      cache_k[l], cache_v[l], PAST_LEN, *params)
        conv_p.append(cp); pool_p.append(pp); k_p.append(kp); v_p.append(vp)
        conv_s.append(cs); pool_s.append(ps); k_s.append(ksn); v_s.append(vsn)
    return (xp, xs,
            jnp.stack(conv_p), jnp.stack(pool_p), jnp.stack(k_p), jnp.stack(v_p),
            jnp.stack(conv_s), jnp.stack(pool_s), jnp.stack(k_s), jnp.stack(v_s))
```

```python
import functools

import numpy as np
import jax
import jax.numpy as jnp
from jax import lax
from jax.experimental import pallas as pl
from jax.experimental.pallas import tpu as pltpu

F32 = jnp.float32
BF16 = jnp.bfloat16

D_MODEL = 1024
DEPTH = 4
D_CONV = 256
D_POOL = 256
D_ATTN = 512
HEAD_DIM = 64
N_HEADS = 8
N_KV_HEADS = 2
GROUP = 4
D_KV = 128
WINDOW = 128
BLOCK = 128
CONV_WIDTH = 31
CONV_BUF = 30
POOL_WINDOWS = (2, 4, 8, 16)
POOL_GROUP_DIM = 64
POOL_BUF = 15
ROPE_THETA = 10000.0
LN_EPS = 1e-5
ALPHA = (2.0 * DEPTH) ** 0.25
MASK_VALUE = -1e30
PAST_LEN = 8192

O_U, O_G, O_ZC, O_PV, O_ZP, O_Q, O_K, O_V, O_ZA, D_IN = (
    0, 256, 512, 768, 1024, 1280, 1792, 1920, 2048, 2560)

CARRY = 32
TM = 256
DEC_GROUP = 64
SUB = 8
VMEM_LIMIT = 48 * 1024 * 1024


def _sigmoid(x):
    return 1.0 / (1.0 + jnp.exp(-x))


def _silu(x):
    return x * _sigmoid(x)


def _ln(x):
    mu = jnp.mean(x, axis=-1, keepdims=True)
    xc = x - mu
    var = jnp.mean(xc * xc, axis=-1, keepdims=True)
    return xc * lax.rsqrt(var + LN_EPS)


def _rope(x, cos, sin_signed):
    lane = lax.broadcasted_iota(jnp.int32, x.shape, 1)
    first_half = (lane & 63) < 32
    rot = jnp.where(first_half, pltpu.roll(x, 96, axis=1), pltpu.roll(x, 32, axis=1))
    return x * cos + rot * sin_signed


def _mod_kernel(c_ref, w_ref, b_ref, o_ref):
    c = c_ref[...]
    sc = _silu(c).astype(BF16)
    o_ref[0] = jnp.dot(sc, w_ref[0].astype(BF16), preferred_element_type=F32) + b_ref[0]


def _modulation(c_all, w_mod, b_mod):
    n = c_all.shape[0]
    tn = 768
    return pl.pallas_call(
        _mod_kernel,
        grid=(DEPTH, 3 * D_MODEL // tn),
        in_specs=[
            pl.BlockSpec((n, D_MODEL), lambda l, j: (0, 0)),
            pl.BlockSpec((1, D_MODEL, tn), lambda l, j: (l, 0, j)),
            pl.BlockSpec((1, 1, tn), lambda l, j: (l, 0, j)),
        ],
        out_specs=pl.BlockSpec((1, n, tn), lambda l, j: (l, 0, j)),
        out_shape=jax.ShapeDtypeStruct((DEPTH, n, 3 * D_MODEL), F32),
        compiler_params=pltpu.CompilerParams(
            dimension_semantics=("arbitrary", "arbitrary"), vmem_limit_bytes=VMEM_LIMIT),
    )(c_all, w_mod, b_mod.reshape(DEPTH, 1, 3 * D_MODEL))


def _prompt_kernel(l_ref, sink_ref,
                   x_ref, mod_ref, win_ref, wout_ref, convw_ref, convb_ref, cng_ref, cnb_ref,
                   wpool_ref, pscale_ref, lng_ref, lnb_ref, cos_ref, sin_ref,
                   xo_ref, convo_ref, poolo_ref, ko_ref, vo_ref,
                   acat, pcat, ps_a, ps_b, kcat, vcat, mixcat):
    l = l_ref[0]
    t = pl.program_id(1)
    last = pl.num_programs(1) - 1
    nb = TM // BLOCK

    @pl.when(t == 0)
    def _():
        acat[0:CARRY, :] = jnp.zeros((CARRY, D_CONV), F32)
        pcat[0:CARRY, :] = jnp.zeros((CARRY, D_POOL), F32)
        kcat[0:BLOCK, :] = jnp.zeros((BLOCK, D_KV), BF16)
        vcat[0:BLOCK, :] = jnp.zeros((BLOCK, D_KV), BF16)

    x = x_ref[0]
    shift = mod_ref[0, 0:1, :]
    scale = mod_ref[0, 1:2, :]
    gate = mod_ref[0, 2:3, :]
    hb = (_ln(x) * (1.0 + scale) + shift).astype(BF16)

    def proj(lo, hi):
        return jnp.dot(hb, win_ref[0, :, lo:hi], preferred_element_type=F32)

    a = proj(O_U, O_G) * _sigmoid(proj(O_G, O_ZC))
    acat[CARRY:CARRY + TM, :] = a
    zc = proj(O_ZC, O_PV)
    rows = 64
    for c in range(TM // rows):
        base = c * rows + CARRY - CONV_BUF
        acc = jnp.zeros((rows, D_CONV), F32)
        for j in range(CONV_WIDTH):
            acc = acc + acat[base + j:base + j + rows, :] * convw_ref[0, j:j + 1, :]
        conv = acc + convb_ref[0]
        ya = _silu(_ln(conv) * cng_ref[0] + cnb_ref[0])
        mixcat[c * rows:(c + 1) * rows, 0:D_CONV] = (
            ya * _silu(zc[c * rows:(c + 1) * rows])).astype(BF16)

    @pl.when(t == last)
    def _():
        convo_ref[0] = acat[CARRY + TM - CONV_BUF:CARRY + TM, :]

    acat[0:CARRY, :] = acat[TM:TM + CARRY, :]

    pv = proj(O_PV, O_ZP)
    pcat[CARRY:CARRY + TM, :] = pv
    n = CARRY + TM
    ps_a[8:n, :] = pcat[8:n, :] + pcat[7:n - 1, :]
    ps_b[16:n, :] = ps_a[16:n, :] + ps_a[14:n - 2, :]
    s2 = ps_a[CARRY:n, :]
    s4 = ps_b[CARRY:n, :]
    ps_a[24:n, :] = ps_b[24:n, :] + ps_b[20:n - 4, :]
    s8 = ps_a[CARRY:n, :]
    s16 = s8 + ps_a[CARRY - 8:n - 8, :]
    lane = lax.broadcasted_iota(jnp.int32, (TM, D_POOL), 1)
    sums = jnp.where(lane < 64, s2, jnp.where(lane < 128, s4, jnp.where(lane < 192, s8, s16)))
    wlen = jnp.where(lane < 64, 2.0, jnp.where(lane < 128, 4.0, jnp.where(lane < 192, 8.0, 16.0)))
    pos1 = (lax.broadcasted_iota(jnp.int32, (TM, D_POOL), 0) + (t * TM + 1)).astype(F32)
    cnt = jnp.minimum(wlen, pos1)
    pooled = (sums / cnt - pv).astype(BF16)
    yb = jnp.dot(pooled, wpool_ref[0], preferred_element_type=F32) * pscale_ref[0]
    mixcat[:, D_CONV:D_CONV + D_POOL] = (yb * _silu(proj(O_ZP, O_Q))).astype(BF16)

    @pl.when(t == last)
    def _():
        poolo_ref[0] = pcat[CARRY + TM - POOL_BUF:CARRY + TM, :]

    pcat[0:CARRY, :] = pcat[TM:TM + CARRY, :]

    cos = cos_ref[...]
    sin = sin_ref[...]
    k = _rope(proj(O_K, O_V), cos, sin)
    v = proj(O_V, O_ZA)
    kcat[BLOCK:BLOCK + TM, :] = k.astype(BF16)
    vcat[BLOCK:BLOCK + TM, :] = v.astype(BF16)

    @pl.when(t == last)
    def _():
        ko_ref[0] = k[TM - WINDOW:TM, :]
        vo_ref[0] = v[TM - WINDOW:TM, :]

    row = lax.broadcasted_iota(jnp.int32, (BLOCK, 2 * BLOCK), 0)
    col = lax.broadcasted_iota(jnp.int32, (BLOCK, 2 * BLOCK), 1)
    rel = col - BLOCK - row
    band = (rel <= 0) & (rel > -WINDOW)
    band_first = (rel <= 0) & (rel > jnp.where(t > 0, jnp.full_like(row, -WINDOW), -row - 1))
    lane_q = lax.broadcasted_iota(jnp.int32, (BLOCK, 2 * HEAD_DIM), 1)
    low = lane_q < HEAD_DIM

    q = proj(O_Q, O_K)
    za = proj(O_ZA, D_IN)
    for g in range(GROUP):
        qg = _rope(q[:, 128 * g:128 * (g + 1)], cos, sin)
        for i in range(nb):
            kk = kcat[i * BLOCK:(i + 2) * BLOCK, :]
            vv = vcat[i * BLOCK:(i + 2) * BLOCK, :]
            mask = band if i > 0 else band_first
            qb = qg[i * BLOCK:(i + 1) * BLOCK]
            lhs = jnp.concatenate(
                [jnp.where(low, qb, 0.0), jnp.where(low, 0.0, qb)], axis=0).astype(BF16)
            s_all = lax.dot_general(lhs, kk, (((1,), (1,)), ((), ())),
                                    preferred_element_type=F32)
            ps, rs = [], []
            for hh in range(N_KV_HEADS):
                s = jnp.where(mask, s_all[hh * BLOCK:(hh + 1) * BLOCK], MASK_VALUE)
                sink = sink_ref[l * N_HEADS + g * N_KV_HEADS + hh]
                m = jnp.maximum(jnp.max(s, axis=-1, keepdims=True), sink)
                p = jnp.exp(s - m)
                denom = jnp.sum(p, axis=-1, keepdims=True) + jnp.exp(sink - m)
                ps.append(p.astype(BF16))
                rs.append(1.0 / denom)
            o_all = jnp.dot(jnp.concatenate(ps, axis=0), vv, preferred_element_type=F32)
            o = jnp.where(low, o_all[0:BLOCK] * rs[0], o_all[BLOCK:2 * BLOCK] * rs[1])
            zg = za[i * BLOCK:(i + 1) * BLOCK, 128 * g:128 * (g + 1)]
            c0 = D_CONV + D_POOL + 128 * g
            mixcat[i * BLOCK:(i + 1) * BLOCK, c0:c0 + 128] = (o * _silu(zg)).astype(BF16)

    kcat[0:BLOCK, :] = kcat[TM:TM + BLOCK, :]
    vcat[0:BLOCK, :] = vcat[TM:TM + BLOCK, :]

    mix = jnp.dot(mixcat[...], wout_ref[0], preferred_element_type=F32)
    y = ALPHA * x + (1.0 + gate) * mix
    xo_ref[0] = _ln(y) * lng_ref[0] + lnb_ref[0]


def _prompt_layer(lidx, sinks_perm, x, mod_p, win, wout, conv_w, conv_b, cnorm_g, cnorm_b,
                  wpool, pool_scale, ln_g, ln_b, cos_t, sin_t):
    B, T, _ = x.shape
    nt = T // TM
    per_layer = lambda shape: pl.BlockSpec((1,) + shape, lambda b, t, l: (l[0],) + (0,) * len(shape))
    grid_spec = pltpu.PrefetchScalarGridSpec(
        num_scalar_prefetch=1,
        grid=(B, nt),
        in_specs=[
            pl.BlockSpec(memory_space=pltpu.SMEM),
            pl.BlockSpec((1, TM, D_MODEL), lambda b, t, l: (b, t, 0)),
            pl.BlockSpec((1, 3, D_MODEL), lambda b, t, l: (b, 0, 0)),
            per_layer((D_MODEL, D_IN)),
            per_layer((D_MODEL, D_MODEL)),
            per_layer((CONV_WIDTH, D_CONV)),
            per_layer((1, D_CONV)),
            per_layer((1, D_CONV)),
            per_layer((1, D_CONV)),
            per_layer((D_POOL, D_POOL)),
            per_layer((1, D_POOL)),
            per_layer((1, D_MODEL)),
            per_layer((1, D_MODEL)),
            pl.BlockSpec((TM, 128), lambda b, t, l: (t, 0)),
            pl.BlockSpec((TM, 128), lambda b, t, l: (t, 0)),
        ],
        out_specs=[
            pl.BlockSpec((1, TM, D_MODEL), lambda b, t, l: (b, t, 0)),
            pl.BlockSpec((1, CONV_BUF, D_CONV), lambda b, t, l: (b, 0, 0)),
            pl.BlockSpec((1, POOL_BUF, D_POOL), lambda b, t, l: (b, 0, 0)),
            pl.BlockSpec((1, WINDOW, D_KV), lambda b, t, l: (b, 0, 0)),
            pl.BlockSpec((1, WINDOW, D_KV), lambda b, t, l: (b, 0, 0)),
        ],
        scratch_shapes=[
            pltpu.VMEM((CARRY + TM, D_CONV), F32),
            pltpu.VMEM((CARRY + TM, D_POOL), F32),
            pltpu.VMEM((CARRY + TM, D_POOL), F32),
            pltpu.VMEM((CARRY + TM, D_POOL), F32),
            pltpu.VMEM((BLOCK + TM, D_KV), BF16),
            pltpu.VMEM((BLOCK + TM, D_KV), BF16),
            pltpu.VMEM((TM, D_MODEL), BF16),
        ],
    )
    out_shape = [
        jax.ShapeDtypeStruct((B, T, D_MODEL), F32),
        jax.ShapeDtypeStruct((B, CONV_BUF, D_CONV), F32),
        jax.ShapeDtypeStruct((B, POOL_BUF, D_POOL), F32),
        jax.ShapeDtypeStruct((B, WINDOW, D_KV), F32),
        jax.ShapeDtypeStruct((B, WINDOW, D_KV), F32),
    ]
    return pl.pallas_call(
        _prompt_kernel,
        grid_spec=grid_spec,
        out_shape=out_shape,
        compiler_params=pltpu.CompilerParams(
            dimension_semantics=("arbitrary", "arbitrary"), vmem_limit_bytes=VMEM_LIMIT),
    )(lidx, sinks_perm, x, mod_p, win, wout, conv_w, conv_b, cnorm_g, cnorm_b,
      wpool, pool_scale, ln_g, ln_b, cos_t, sin_t)


def _decode_kernel(l_ref,
                   x_ref, mod_ref, win_ref, wout_ref, convw_ref, convb_ref, cng_ref, cnb_ref,
                   wpool_ref, pscale_ref, lng_ref, lnb_ref, cos_ref, sin_ref, sinkrow_ref,
                   cconv_ref, cpool_ref, ck_ref, cv_ref,
                   xo_ref, ao_ref, pvo_ref, ko_ref, vo_ref,
                   qs, ocat):
    gb = DEC_GROUP
    nt = x_ref.shape[0]
    x3 = x_ref[...]
    shift = mod_ref[:, 0:D_MODEL]
    scale = mod_ref[:, D_MODEL:2 * D_MODEL]
    gate = mod_ref[:, 2 * D_MODEL:3 * D_MODEL]
    h3 = _ln(x3) * (1.0 + scale)[None] + shift[None]
    hb = h3.reshape(nt * gb, D_MODEL).astype(BF16)

    def proj(lo, hi):
        return jnp.dot(hb, win_ref[0, :, lo:hi], preferred_element_type=F32)

    a = proj(O_U, O_G) * _sigmoid(proj(O_G, O_ZC))
    ao_ref[...] = a.reshape(nt, gb, D_CONV)
    zc = proj(O_ZC, O_PV)
    ya_rows = []
    for tt in range(nt):
        acc = jnp.zeros((gb, D_CONV), F32)
        for j in range(CONV_WIDTH):
            i = tt + j
            xi = cconv_ref[0, i] if i < CONV_BUF else a[(i - CONV_BUF) * gb:(i - CONV_BUF + 1) * gb]
            acc = acc + xi * convw_ref[0, j:j + 1, :]
        conv = acc + convb_ref[0]
        ya_rows.append(_silu(_ln(conv) * cng_ref[0] + cnb_ref[0]))
    ya = jnp.concatenate(ya_rows, axis=0) * _silu(zc)

    pv = proj(O_PV, O_ZP)
    pvo_ref[...] = pv.reshape(nt, gb, D_POOL)
    lane = lax.broadcasted_iota(jnp.int32, (gb, D_POOL), 1)
    wlen = jnp.where(lane < 64, 2, jnp.where(lane < 128, 4, jnp.where(lane < 192, 8, 16)))
    pooled_rows = []
    for tt in range(nt):
        acc = jnp.zeros((gb, D_POOL), F32)
        for d in range(max(POOL_WINDOWS)):
            i = POOL_BUF + tt - d
            xi = cpool_ref[0, i] if i < POOL_BUF else pv[(i - POOL_BUF) * gb:(i - POOL_BUF + 1) * gb]
            acc = acc + (xi if d < min(POOL_WINDOWS) else jnp.where(wlen > d, xi, 0.0))
        cnt = jnp.minimum(wlen, PAST_LEN + tt + 1).astype(F32)
        pooled_rows.append(acc / cnt - pv[tt * gb:(tt + 1) * gb])
    pooled = jnp.concatenate(pooled_rows, axis=0).astype(BF16)
    yb = jnp.dot(pooled, wpool_ref[0], preferred_element_type=F32) * pscale_ref[0]
    yb = yb * _silu(proj(O_ZP, O_Q))

    cos3 = cos_ref[...]
    sin3 = sin_ref[...]
    cos = jnp.broadcast_to(cos3, (nt, gb, 128)).reshape(nt * gb, 128)
    sin = jnp.broadcast_to(sin3, (nt, gb, 128)).reshape(nt * gb, 128)
    k = _rope(proj(O_K, O_V), cos, sin)
    v = proj(O_V, O_ZA)
    ko_ref[...] = k.reshape(nt, gb, D_KV)
    vo_ref[...] = v.reshape(nt, gb, D_KV)
    q = proj(O_Q, O_K)
    for g in range(GROUP):
        qs[:, 128 * g:128 * (g + 1)] = _rope(q[:, 128 * g:128 * (g + 1)], cos, sin)

    nrow = nt * GROUP * N_KV_HEADS * SUB
    r1 = lax.broadcasted_iota(jnp.int32, (nrow, SUB * WINDOW), 0)
    c1 = lax.broadcasted_iota(jnp.int32, (nrow, SUB * WINDOW), 1)
    mask1 = ((c1 >> 7) == (r1 & (SUB - 1))) & ((c1 & (WINDOW - 1)) > (r1 >> 6))
    r2 = lax.broadcasted_iota(jnp.int32, (nrow, nt * SUB), 0)
    c2 = lax.broadcasted_iota(jnp.int32, (nrow, nt * SUB), 1)
    mask2 = ((c2 & (SUB - 1)) == (r2 & (SUB - 1))) & ((c2 >> 3) <= (r2 >> 6))
    lane_q = lax.broadcasted_iota(jnp.int32, (SUB, 128), 1)
    low = lane_q < HEAD_DIM
    sink = sinkrow_ref[0]

    def sub_block(sb, carry):
        b0 = pl.multiple_of(sb * SUB, SUB)
        pieces = []
        for tt in range(nt):
            for g in range(GROUP):
                qp = qs[pl.ds(tt * gb + b0, SUB), 128 * g:128 * (g + 1)]
                pieces.append(jnp.where(low, qp, 0.0))
                pieces.append(jnp.where(low, 0.0, qp))
        lhs = jnp.concatenate(pieces, axis=0).astype(BF16)
        kc = ck_ref[0, pl.ds(b0, SUB)].reshape(SUB * WINDOW, D_KV).astype(BF16)
        vc = cv_ref[0, pl.ds(b0, SUB)].reshape(SUB * WINDOW, D_KV).astype(BF16)
        kn = jnp.concatenate(
            [ko_ref[tt, pl.ds(b0, SUB), :] for tt in range(nt)], axis=0).astype(BF16)
        vn = jnp.concatenate(
            [vo_ref[tt, pl.ds(b0, SUB), :] for tt in range(nt)], axis=0).astype(BF16)
        nt_dims = (((1,), (1,)), ((), ()))
        s1 = jnp.where(mask1, lax.dot_general(lhs, kc, nt_dims, preferred_element_type=F32),
                       MASK_VALUE)
        s2 = jnp.where(mask2, lax.dot_general(lhs, kn, nt_dims, preferred_element_type=F32),
                       MASK_VALUE)
        m = jnp.maximum(jnp.maximum(jnp.max(s1, axis=-1, keepdims=True),
                                    jnp.max(s2, axis=-1, keepdims=True)), sink)
        p1 = jnp.exp(s1 - m)
        p2 = jnp.exp(s2 - m)
        denom = (jnp.sum(p1, axis=-1, keepdims=True) + jnp.sum(p2, axis=-1, keepdims=True)
                 + jnp.exp(sink - m))
        o_all = (jnp.dot(p1.astype(BF16), vc, preferred_element_type=F32)
                 + jnp.dot(p2.astype(BF16), vn, preferred_element_type=F32)) * (1.0 / denom)
        for tt in range(nt):
            for g in range(GROUP):
                r0 = ((tt * GROUP + g) * N_KV_HEADS) * SUB
                o = jnp.where(low, o_all[r0:r0 + SUB], o_all[r0 + SUB:r0 + 2 * SUB])
                ocat[pl.ds(tt * gb + b0, SUB), 128 * g:128 * (g + 1)] = o
        return carry

    lax.fori_loop(0, gb // SUB, sub_block, 0)
    yc = ocat[...] * _silu(proj(O_ZA, D_IN))

    mixcat = jnp.concatenate([ya, yb, yc], axis=-1).astype(BF16)
    mix = jnp.dot(mixcat, wout_ref[0], preferred_element_type=F32).reshape(nt, gb, D_MODEL)
    y = ALPHA * x3 + (1.0 + gate)[None] * mix
    xo_ref[...] = _ln(y) * lng_ref[0][None] + lnb_ref[0][None]


def _decode_layer(lidx, x, mod_s, win, wout, conv_w, conv_b, cnorm_g, cnorm_b, wpool, pool_scale,
                  ln_g, ln_b, cos_s, sin_s, sink_rows, cconv_t, cpool_t, cache_k, cache_v):
    nt, nbatch, _ = x.shape
    gb = DEC_GROUP
    per_layer = lambda shape: pl.BlockSpec((1,) + shape, lambda g, l: (l[0],) + (0,) * len(shape))
    tok = lambda c: pl.BlockSpec((nt, gb, c), lambda g, l: (0, g, 0))
    nrow = nt * GROUP * N_KV_HEADS * SUB
    grid_spec = pltpu.PrefetchScalarGridSpec(
        num_scalar_prefetch=1,
        grid=(nbatch // gb,),
        in_specs=[
            tok(D_MODEL),
            pl.BlockSpec((gb, 3 * D_MODEL), lambda g, l: (g, 0)),
            per_layer((D_MODEL, D_IN)),
            per_layer((D_MODEL, D_MODEL)),
            per_layer((CONV_WIDTH, D_CONV)),
            per_layer((1, D_CONV)),
            per_layer((1, D_CONV)),
            per_layer((1, D_CONV)),
            per_layer((D_POOL, D_POOL)),
            per_layer((1, D_POOL)),
            per_layer((1, D_MODEL)),
            per_layer((1, D_MODEL)),
            pl.BlockSpec((nt, 1, 128), lambda g, l: (0, 0, 0)),
            pl.BlockSpec((nt, 1, 128), lambda g, l: (0, 0, 0)),
            per_layer((nrow, 1)),
            pl.BlockSpec((1, CONV_BUF, gb, D_CONV), lambda g, l: (l[0], 0, g, 0)),
            pl.BlockSpec((1, POOL_BUF, gb, D_POOL), lambda g, l: (l[0], 0, g, 0)),
            pl.BlockSpec((1, gb, WINDOW, D_KV), lambda g, l: (l[0], g, 0, 0)),
            pl.BlockSpec((1, gb, WINDOW, D_KV), lambda g, l: (l[0], g, 0, 0)),
        ],
        out_specs=[tok(D_MODEL), tok(D_CONV), tok(D_POOL), tok(D_KV), tok(D_KV)],
        scratch_shapes=[
            pltpu.VMEM((nt * gb, D_ATTN), F32),
            pltpu.VMEM((nt * gb, D_ATTN), F32),
        ],
    )
    out_shape = [jax.ShapeDtypeStruct((nt, nbatch, c), F32)
                 for c in (D_MODEL, D_CONV, D_POOL, D_KV, D_KV)]
    return pl.pallas_call(
        _decode_kernel,
        grid_spec=grid_spec,
        out_shape=out_shape,
        compiler_params=pltpu.CompilerParams(
            dimension_semantics=("arbitrary",), vmem_limit_bytes=VMEM_LIMIT),
    )(lidx, x, mod_s, win, wout, conv_w, conv_b, cnorm_g, cnorm_b, wpool, pool_scale,
      ln_g, ln_b, cos_s, sin_s, sink_rows, cconv_t, cpool_t, cache_k, cache_v)


def _attn_perm():
    idx = []
    for g in range(GROUP):
        for h in range(N_KV_HEADS):
            head = h * GROUP + g
            idx.extend(range(head * HEAD_DIM, (head + 1) * HEAD_DIM))
    return np.asarray(idx, np.int32)


def _rope_tables(pos):
    half = HEAD_DIM // 2
    inv_freq = ROPE_THETA ** (-jnp.arange(half, dtype=F32) * (2.0 / HEAD_DIM))
    ang = pos.astype(F32)[:, None] * inv_freq[None, :]
    cos = jnp.cos(ang)
    sin = jnp.sin(ang)
    cos_t = jnp.concatenate([cos, cos, cos, cos], axis=-1)
    sin_t = jnp.concatenate([-sin, sin, -sin, sin], axis=-1)
    return cos_t, sin_t


def kernel(x_prompt, x_sample, cache_conv, cache_pool, cache_k, cache_v, c_prompt, c_sample,
           w_in, w_out, conv_w, conv_b, cnorm_g, cnorm_b, pool_w, pool_scale, sinks,
           w_mod, b_mod, ln_g, ln_b):
    B, T, _ = x_prompt.shape
    nbatch, nt, _ = x_sample.shape
    wb = cache_k.shape[2]

    perm = _attn_perm()
    wq = w_in[:, :, O_Q:O_K][:, :, perm] * (HEAD_DIM ** -0.5)
    wza = w_in[:, :, O_ZA:D_IN][:, :, perm]
    win = jnp.concatenate([w_in[:, :, :O_Q], wq, w_in[:, :, O_K:O_ZA], wza], axis=-1).astype(BF16)
    wout = jnp.concatenate(
        [w_out[:, :D_CONV + D_POOL], w_out[:, D_CONV + D_POOL:][:, perm]], axis=1).astype(BF16)
    wpool = jnp.zeros((DEPTH, D_POOL, D_POOL), F32)
    for gi in range(len(POOL_WINDOWS)):
        sl = slice(gi * POOL_GROUP_DIM, (gi + 1) * POOL_GROUP_DIM)
        wpool = wpool.at[:, sl, sl].set(pool_w[:, gi])
    wpool = wpool.astype(BF16)
    sinks_gh = sinks.reshape(DEPTH, N_KV_HEADS, GROUP).transpose(0, 2, 1)
    sinks_flat = sinks_gh.reshape(DEPTH * N_HEADS)
    sink_rows = jnp.broadcast_to(
        sinks_gh[:, None, :, :, None], (DEPTH, nt, GROUP, N_KV_HEADS, SUB)
    ).reshape(DEPTH, nt * GROUP * N_KV_HEADS * SUB, 1)

    r3 = lambda p: p.reshape(DEPTH, 1, -1)
    conv_b3, cng3, cnb3, pscale3, lng3, lnb3 = map(r3, (conv_b, cnorm_g, cnorm_b, pool_scale, ln_g, ln_b))

    cos_p, sin_p = _rope_tables(jnp.arange(T, dtype=jnp.int32))
    cos_s, sin_s = _rope_tables(PAST_LEN + jnp.arange(nt, dtype=jnp.int32))
    cos_s = cos_s.reshape(nt, 1, 128)
    sin_s = sin_s.reshape(nt, 1, 128)

    mod = _modulation(jnp.concatenate([c_prompt, c_sample], axis=0), w_mod, b_mod)
    mod_p = mod[:, :B].reshape(DEPTH, B, 3, D_MODEL)
    mod_s = mod[:, B:]

    cconv_t = cache_conv.transpose(0, 2, 1, 3)
    cpool_t = cache_pool.transpose(0, 2, 1, 3)
    ck = cache_k.reshape(DEPTH, nbatch, wb, D_KV)
    cv = cache_v.reshape(DEPTH, nbatch, wb, D_KV)

    xp = x_prompt
    xs = x_sample.transpose(1, 0, 2)
    conv_p, pool_p, k_p, v_p, a_s, pv_s, k_s, v_s = ([] for _ in range(8))
    for l in range(DEPTH):
        lidx = jnp.full((1,), l, jnp.int32)
        xp, cp, pp, kp, vp = _prompt_layer(
            lidx, sinks_flat, xp, mod_p[l], win, wout, conv_w, conv_b3, cng3, cnb3, wpool, pscale3,
            lng3, lnb3, cos_p, sin_p)
        xs, an, pn, kn, vn = _decode_layer(
            lidx, xs, mod_s[l], win, wout, conv_w, conv_b3, cng3, cnb3, wpool, pscale3,
            lng3, lnb3, cos_s, sin_s, sink_rows, cconv_t, cpool_t, ck, cv)
        conv_p.append(cp); pool_p.append(pp); k_p.append(kp); v_p.append(vp)
        a_s.append(an); pv_s.append(pn); k_s.append(kn); v_s.append(vn)

    kv5 = lambda z: z.reshape(z.shape[:-1] + (N_KV_HEADS, HEAD_DIM))
    tail = lambda cache, new, keep: jnp.concatenate(
        [cache[:, :, cache.shape[2] - keep:], jnp.stack(new).transpose(0, 2, 1, 3)], axis=2)
    conv_s = tail(cache_conv, a_s, CONV_BUF - nt)
    pool_s = tail(cache_pool, pv_s, POOL_BUF - nt)
    k_samp = kv5(tail(ck, k_s, wb - nt))
    v_samp = kv5(tail(cv, v_s, wb - nt))
    return (xp, xs.transpose(1, 0, 2),
            jnp.stack(conv_p), jnp.stack(pool_p), kv5(jnp.stack(k_p)), kv5(jnp.stack(v_p)),
            conv_s, pool_s, k_samp, v_samp)
```

```python
import functools

import numpy as np
import jax
import jax.numpy as jnp
from jax import lax
from jax.experimental import pallas as pl
from jax.experimental.pallas import tpu as pltpu

F32 = jnp.float32
BF16 = jnp.bfloat16

D_MODEL = 1024
DEPTH = 4
D_CONV = 256
D_POOL = 256
D_ATTN = 512
HEAD_DIM = 64
N_HEADS = 8
N_KV_HEADS = 2
GROUP = 4
D_KV = 128
WINDOW = 128
BLOCK = 128
CONV_WIDTH = 31
CONV_BUF = 30
POOL_WINDOWS = (2, 4, 8, 16)
POOL_GROUP_DIM = 64
POOL_BUF = 15
ROPE_THETA = 10000.0
LN_EPS = 1e-5
ALPHA = (2.0 * DEPTH) ** 0.25
MASK_VALUE = -1e30
PAST_LEN = 8192

O_U, O_G, O_ZC, O_PV, O_ZP, O_Q, O_K, O_V, O_ZA, D_IN = (
    0, 256, 512, 768, 1024, 1280, 1792, 1920, 2048, 2560)

SUBLANES = 8
CARRY = 32
TM = 256
DEC_GROUP = 64
SUB = 8
VMEM_LIMIT = 48 * 1024 * 1024


def _sigmoid(x):
    return 1.0 / (1.0 + jnp.exp(-x))


def _silu(x):
    return x * _sigmoid(x)


def _ln(x):
    mu = jnp.mean(x, axis=-1, keepdims=True)
    xc = x - mu
    var = jnp.mean(xc * xc, axis=-1, keepdims=True)
    return xc * lax.rsqrt(var + LN_EPS)


def _rope(x, cos, sin_signed):
    lane = lax.broadcasted_iota(jnp.int32, x.shape, 1)
    first_half = (lane & 63) < 32
    rot = jnp.where(first_half, pltpu.roll(x, 96, axis=1), pltpu.roll(x, 32, axis=1))
    return x * cos + rot * sin_signed


def _mod_kernel(c_ref, w_ref, b_ref, o_ref):
    c = c_ref[...]
    sc = _silu(c).astype(BF16)
    o_ref[0] = jnp.dot(sc, w_ref[0].astype(BF16), preferred_element_type=F32) + b_ref[0]


def _modulation(c_all, w_mod, b_mod):
    n = c_all.shape[0]
    tn = 768
    return pl.pallas_call(
        _mod_kernel,
        grid=(DEPTH, 3 * D_MODEL // tn),
        in_specs=[
            pl.BlockSpec((n, D_MODEL), lambda l, j: (0, 0)),
            pl.BlockSpec((1, D_MODEL, tn), lambda l, j: (l, 0, j)),
            pl.BlockSpec((1, 1, tn), lambda l, j: (l, 0, j)),
        ],
        out_specs=pl.BlockSpec((1, n, tn), lambda l, j: (l, 0, j)),
        out_shape=jax.ShapeDtypeStruct((DEPTH, n, 3 * D_MODEL), F32),
        compiler_params=pltpu.CompilerParams(
            dimension_semantics=("arbitrary", "arbitrary"), vmem_limit_bytes=VMEM_LIMIT),
    )(c_all, w_mod, b_mod.reshape(DEPTH, 1, 3 * D_MODEL))


def _prompt_kernel(l_ref, sink_ref,
                   x_ref, mod_ref, win_ref, wout_ref, convw_ref, convb_ref, cng_ref, cnb_ref,
                   wpool_ref, pscale_ref, lng_ref, lnb_ref, cos_ref, sin_ref,
                   xo_ref, convo_ref, poolo_ref, ko_ref, vo_ref,
                   acat, ashift, pcat, ps_a, ps_b, kcat, vcat, mixcat):
    l = l_ref[0]
    t = pl.program_id(1)
    last = pl.num_programs(1) - 1
    nb = TM // BLOCK

    @pl.when(t == 0)
    def _():
        acat[0:CARRY, :] = jnp.zeros((CARRY, D_CONV), F32)
        pcat[0:CARRY, :] = jnp.zeros((CARRY, D_POOL), F32)
        kcat[0:BLOCK, :] = jnp.zeros((BLOCK, D_KV), BF16)
        vcat[0:BLOCK, :] = jnp.zeros((BLOCK, D_KV), BF16)

    x = x_ref[0]
    shift = mod_ref[0, 0:1, :]
    scale = mod_ref[0, 1:2, :]
    gate = mod_ref[0, 2:3, :]
    hb = (_ln(x) * (1.0 + scale) + shift).astype(BF16)

    def proj(lo, hi):
        return jnp.dot(hb, win_ref[0, :, lo:hi], preferred_element_type=F32)

    a = proj(O_U, O_G) * _sigmoid(proj(O_G, O_ZC))
    acat[CARRY:CARRY + TM, :] = a
    zc = proj(O_ZC, O_PV)
    for r in range(1, SUBLANES):
        ashift[r - 1] = acat[r:r + TM + CARRY - SUBLANES, :]
    rows = 64
    for c in range(TM // rows):
        acc = jnp.zeros((rows, D_CONV), F32)
        for j in range(CONV_WIDTH):
            off = CARRY - CONV_BUF + j
            r, lo = off % SUBLANES, c * rows + off - off % SUBLANES
            tap = acat[lo:lo + rows, :] if r == 0 else ashift[r - 1, lo:lo + rows, :]
            acc = acc + tap * convw_ref[0, j:j + 1, :]
        conv = acc + convb_ref[0]
        ya = _silu(_ln(conv) * cng_ref[0] + cnb_ref[0])
        mixcat[c * rows:(c + 1) * rows, 0:D_CONV] = (
            ya * _silu(zc[c * rows:(c + 1) * rows])).astype(BF16)

    @pl.when(t == last)
    def _():
        convo_ref[0] = acat[CARRY + TM - CONV_BUF:CARRY + TM, :]

    acat[0:CARRY, :] = acat[TM:TM + CARRY, :]

    pv = proj(O_PV, O_ZP)
    pcat[CARRY:CARRY + TM, :] = pv
    n = CARRY + TM
    ps_a[8:n, :] = pcat[8:n, :] + pcat[7:n - 1, :]
    ps_b[16:n, :] = ps_a[16:n, :] + ps_a[14:n - 2, :]
    s2 = ps_a[CARRY:n, :]
    s4 = ps_b[CARRY:n, :]
    ps_a[24:n, :] = ps_b[24:n, :] + ps_b[20:n - 4, :]
    s8 = ps_a[CARRY:n, :]
    s16 = s8 + ps_a[CARRY - 8:n - 8, :]
    lane = lax.broadcasted_iota(jnp.int32, (TM, D_POOL), 1)
    sums = jnp.where(lane < 64, s2, jnp.where(lane < 128, s4, jnp.where(lane < 192, s8, s16)))
    wlen = jnp.where(lane < 64, 2.0, jnp.where(lane < 128, 4.0, jnp.where(lane < 192, 8.0, 16.0)))
    pos1 = (lax.broadcasted_iota(jnp.int32, (TM, D_POOL), 0) + (t * TM + 1)).astype(F32)
    cnt = jnp.minimum(wlen, pos1)
    pooled = (sums / cnt - pv).astype(BF16)
    yb = jnp.dot(pooled, wpool_ref[0], preferred_element_type=F32) * pscale_ref[0]
    mixcat[:, D_CONV:D_CONV + D_POOL] = (yb * _silu(proj(O_ZP, O_Q))).astype(BF16)

    @pl.when(t == last)
    def _():
        poolo_ref[0] = pcat[CARRY + TM - POOL_BUF:CARRY + TM, :]

    pcat[0:CARRY, :] = pcat[TM:TM + CARRY, :]

    cos = cos_ref[...]
    sin = sin_ref[...]
    k = _rope(proj(O_K, O_V), cos, sin)
    v = proj(O_V, O_ZA)
    kcat[BLOCK:BLOCK + TM, :] = k.astype(BF16)
    vcat[BLOCK:BLOCK + TM, :] = v.astype(BF16)

    @pl.when(t == last)
    def _():
        ko_ref[0] = k[TM - WINDOW:TM, :]
        vo_ref[0] = v[TM - WINDOW:TM, :]

    row = lax.broadcasted_iota(jnp.int32, (BLOCK, 2 * BLOCK), 0)
    col = lax.broadcasted_iota(jnp.int32, (BLOCK, 2 * BLOCK), 1)
    rel = col - BLOCK - row
    band = (rel <= 0) & (rel > -WINDOW)
    band_first = (rel <= 0) & (rel > jnp.where(t > 0, jnp.full_like(row, -WINDOW), -row - 1))
    lane_q = lax.broadcasted_iota(jnp.int32, (BLOCK, 2 * HEAD_DIM), 1)
    low = lane_q < HEAD_DIM

    q = proj(O_Q, O_K)
    za = proj(O_ZA, D_IN)
    for g in range(GROUP):
        qg = _rope(q[:, 128 * g:128 * (g + 1)], cos, sin)
        for i in range(nb):
            kk = kcat[i * BLOCK:(i + 2) * BLOCK, :]
            vv = vcat[i * BLOCK:(i + 2) * BLOCK, :]
            mask = band if i > 0 else band_first
            qb = qg[i * BLOCK:(i + 1) * BLOCK]
            lhs = jnp.concatenate(
                [jnp.where(low, qb, 0.0), jnp.where(low, 0.0, qb)], axis=0).astype(BF16)
            s_all = lax.dot_general(lhs, kk, (((1,), (1,)), ((), ())),
                                    preferred_element_type=F32)
            ps, rs = [], []
            for hh in range(N_KV_HEADS):
                s = jnp.where(mask, s_all[hh * BLOCK:(hh + 1) * BLOCK], MASK_VALUE)
                sink = sink_ref[l * N_HEADS + g * N_KV_HEADS + hh]
                m = jnp.maximum(jnp.max(s, axis=-1, keepdims=True), sink)
                p = jnp.exp(s - m)
                denom = jnp.sum(p, axis=-1, keepdims=True) + jnp.exp(sink - m)
                ps.append(p.astype(BF16))
                rs.append(1.0 / denom)
            o_all = jnp.dot(jnp.concatenate(ps, axis=0), vv, preferred_element_type=F32)
            o = jnp.where(low, o_all[0:BLOCK] * rs[0], o_all[BLOCK:2 * BLOCK] * rs[1])
            zg = za[i * BLOCK:(i + 1) * BLOCK, 128 * g:128 * (g + 1)]
            c0 = D_CONV + D_POOL + 128 * g
            mixcat[i * BLOCK:(i + 1) * BLOCK, c0:c0 + 128] = (o * _silu(zg)).astype(BF16)

    kcat[0:BLOCK, :] = kcat[TM:TM + BLOCK, :]
    vcat[0:BLOCK, :] = vcat[TM:TM + BLOCK, :]

    mix = jnp.dot(mixcat[...], wout_ref[0], preferred_element_type=F32)
    y = ALPHA * x + (1.0 + gate) * mix
    xo_ref[0] = _ln(y) * lng_ref[0] + lnb_ref[0]


def _prompt_layer(lidx, sinks_perm, x, mod_p, win, wout, conv_w, conv_b, cnorm_g, cnorm_b,
                  wpool, pool_scale, ln_g, ln_b, cos_t, sin_t):
    B, T, _ = x.shape
    nt = T // TM
    per_layer = lambda shape: pl.BlockSpec((1,) + shape, lambda b, t, l: (l[0],) + (0,) * len(shape))
    grid_spec = pltpu.PrefetchScalarGridSpec(
        num_scalar_prefetch=1,
        grid=(B, nt),
        in_specs=[
            pl.BlockSpec(memory_space=pltpu.SMEM),
            pl.BlockSpec((1, TM, D_MODEL), lambda b, t, l: (b, t, 0)),
            pl.BlockSpec((1, 3, D_MODEL), lambda b, t, l: (b, 0, 0)),
            per_layer((D_MODEL, D_IN)),
            per_layer((D_MODEL, D_MODEL)),
            per_layer((CONV_WIDTH, D_CONV)),
            per_layer((1, D_CONV)),
            per_layer((1, D_CONV)),
            per_layer((1, D_CONV)),
            per_layer((D_POOL, D_POOL)),
            per_layer((1, D_POOL)),
            per_layer((1, D_MODEL)),
            per_layer((1, D_MODEL)),
            pl.BlockSpec((TM, 128), lambda b, t, l: (t, 0)),
            pl.BlockSpec((TM, 128), lambda b, t, l: (t, 0)),
        ],
        out_specs=[
            pl.BlockSpec((1, TM, D_MODEL), lambda b, t, l: (b, t, 0)),
            pl.BlockSpec((1, CONV_BUF, D_CONV), lambda b, t, l: (b, 0, 0)),
            pl.BlockSpec((1, POOL_BUF, D_POOL), lambda b, t, l: (b, 0, 0)),
            pl.BlockSpec((1, WINDOW, D_KV), lambda b, t, l: (b, 0, 0)),
            pl.BlockSpec((1, WINDOW, D_KV), lambda b, t, l: (b, 0, 0)),
        ],
        scratch_shapes=[
            pltpu.VMEM((CARRY + TM, D_CONV), F32),
            pltpu.VMEM((SUBLANES - 1, CARRY + TM - SUBLANES, D_CONV), F32),
            pltpu.VMEM((CARRY + TM, D_POOL), F32),
            pltpu.VMEM((CARRY + TM, D_POOL), F32),
            pltpu.VMEM((CARRY + TM, D_POOL), F32),
            pltpu.VMEM((BLOCK + TM, D_KV), BF16),
            pltpu.VMEM((BLOCK + TM, D_KV), BF16),
            pltpu.VMEM((TM, D_MODEL), BF16),
        ],
    )
    out_shape = [
        jax.ShapeDtypeStruct((B, T, D_MODEL), F32),
        jax.ShapeDtypeStruct((B, CONV_BUF, D_CONV), F32),
        jax.ShapeDtypeStruct((B, POOL_BUF, D_POOL), F32),
        jax.ShapeDtypeStruct((B, WINDOW, D_KV), F32),
        jax.ShapeDtypeStruct((B, WINDOW, D_KV), F32),
    ]
    return pl.pallas_call(
        _prompt_kernel,
        grid_spec=grid_spec,
        out_shape=out_shape,
        compiler_params=pltpu.CompilerParams(
            dimension_semantics=("arbitrary", "arbitrary"), vmem_limit_bytes=VMEM_LIMIT),
    )(lidx, sinks_perm, x, mod_p, win, wout, conv_w, conv_b, cnorm_g, cnorm_b,
      wpool, pool_scale, ln_g, ln_b, cos_t, sin_t)


def _decode_kernel(l_ref,
                   x_ref, mod_ref, win_ref, wout_ref, convw_ref, convb_ref, cng_ref, cnb_ref,
                   wpool_ref, pscale_ref, lng_ref, lnb_ref, cos_ref, sin_ref, sinkrow_ref,
                   cconv_ref, cpool_ref, ck_ref, cv_ref, *rest):
    xo_ref, convo_ref, poolo_ref, ko_ref, vo_ref, qs, ocat = rest[-7:]
    gb = DEC_GROUP
    nt = x_ref.shape[0]
    x3 = x_ref[...]
    shift = mod_ref[:, 0:D_MODEL]
    scale = mod_ref[:, D_MODEL:2 * D_MODEL]
    gate = mod_ref[:, 2 * D_MODEL:3 * D_MODEL]
    h3 = _ln(x3) * (1.0 + scale)[None] + shift[None]
    hb = h3.reshape(nt * gb, D_MODEL).astype(BF16)

    def proj(lo, hi):
        return jnp.dot(hb, win_ref[0, :, lo:hi], preferred_element_type=F32)

    a = proj(O_U, O_G) * _sigmoid(proj(O_G, O_ZC))
    convo_ref[0, :, 0:CONV_BUF - nt, :] = cconv_ref[0, :, nt:CONV_BUF, :]
    for tt in range(nt):
        convo_ref[0, :, CONV_BUF - nt + tt, :] = a[tt * gb:(tt + 1) * gb]
    zc = proj(O_ZC, O_PV)
    ya_rows = []
    for tt in range(nt):
        acc = jnp.zeros((gb, D_CONV), F32)
        for j in range(CONV_WIDTH):
            i = tt + j
            xi = (cconv_ref[0, :, i, :] if i < CONV_BUF
                  else a[(i - CONV_BUF) * gb:(i - CONV_BUF + 1) * gb])
            acc = acc + xi * convw_ref[0, j:j + 1, :]
        conv = acc + convb_ref[0]
        ya_rows.append(_silu(_ln(conv) * cng_ref[0] + cnb_ref[0]))
    ya = jnp.concatenate(ya_rows, axis=0) * _silu(zc)

    pv = proj(O_PV, O_ZP)
    poolo_ref[0, :, 0:POOL_BUF - nt, :] = cpool_ref[0, :, nt:POOL_BUF, :]
    for tt in range(nt):
        poolo_ref[0, :, POOL_BUF - nt + tt, :] = pv[tt * gb:(tt + 1) * gb]
    lane = lax.broadcasted_iota(jnp.int32, (gb, D_POOL), 1)
    wlen = jnp.where(lane < 64, 2, jnp.where(lane < 128, 4, jnp.where(lane < 192, 8, 16)))
    pooled_rows = []
    for tt in range(nt):
        acc = jnp.zeros((gb, D_POOL), F32)
        for d in range(max(POOL_WINDOWS)):
            i = POOL_BUF + tt - d
            xi = (cpool_ref[0, :, i, :] if i < POOL_BUF
                  else pv[(i - POOL_BUF) * gb:(i - POOL_BUF + 1) * gb])
            acc = acc + (xi if d < min(POOL_WINDOWS) else jnp.where(wlen > d, xi, 0.0))
        cnt = jnp.minimum(wlen, PAST_LEN + tt + 1).astype(F32)
        pooled_rows.append(acc / cnt - pv[tt * gb:(tt + 1) * gb])
    pooled = jnp.concatenate(pooled_rows, axis=0).astype(BF16)
    yb = jnp.dot(pooled, wpool_ref[0], preferred_element_type=F32) * pscale_ref[0]
    yb = yb * _silu(proj(O_ZP, O_Q))

    cos3 = cos_ref[...]
    sin3 = sin_ref[...]
    cos = jnp.broadcast_to(cos3, (nt, gb, 128)).reshape(nt * gb, 128)
    sin = jnp.broadcast_to(sin3, (nt, gb, 128)).reshape(nt * gb, 128)
    k = _rope(proj(O_K, O_V), cos, sin)
    v = proj(O_V, O_ZA)
    ko_ref[...] = k.reshape(nt, gb, D_KV)
    vo_ref[...] = v.reshape(nt, gb, D_KV)
    q = proj(O_Q, O_K)
    for g in range(GROUP):
        qs[:, 128 * g:128 * (g + 1)] = _rope(q[:, 128 * g:128 * (g + 1)], cos, sin)

    nrow = nt * GROUP * N_KV_HEADS * SUB
    r1 = lax.broadcasted_iota(jnp.int32, (nrow, SUB * WINDOW), 0)
    c1 = lax.broadcasted_iota(jnp.int32, (nrow, SUB * WINDOW), 1)
    mask1 = ((c1 >> 7) == (r1 & (SUB - 1))) & ((c1 & (WINDOW - 1)) > (r1 >> 6))
    r2 = lax.broadcasted_iota(jnp.int32, (nrow, nt * SUB), 0)
    c2 = lax.broadcasted_iota(jnp.int32, (nrow, nt * SUB), 1)
    mask2 = ((c2 & (SUB - 1)) == (r2 & (SUB - 1))) & ((c2 >> 3) <= (r2 >> 6))
    lane_q = lax.broadcasted_iota(jnp.int32, (SUB, 128), 1)
    low = lane_q < HEAD_DIM
    sink = sinkrow_ref[0]

    def sub_block(sb, carry):
        b0 = pl.multiple_of(sb * SUB, SUB)
        pieces = []
        for tt in range(nt):
            for g in range(GROUP):
                qp = qs[pl.ds(tt * gb + b0, SUB), 128 * g:128 * (g + 1)]
                pieces.append(jnp.where(low, qp, 0.0))
                pieces.append(jnp.where(low, 0.0, qp))
        lhs = jnp.concatenate(pieces, axis=0).astype(BF16)
        kc = ck_ref[0, pl.ds(b0, SUB)].reshape(SUB * WINDOW, D_KV).astype(BF16)
        vc = cv_ref[0, pl.ds(b0, SUB)].reshape(SUB * WINDOW, D_KV).astype(BF16)
        kn = jnp.concatenate(
            [ko_ref[tt, pl.ds(b0, SUB), :] for tt in range(nt)], axis=0).astype(BF16)
        vn = jnp.concatenate(
            [vo_ref[tt, pl.ds(b0, SUB), :] for tt in range(nt)], axis=0).astype(BF16)
        nt_dims = (((1,), (1,)), ((), ()))
        s1 = jnp.where(mask1, lax.dot_general(lhs, kc, nt_dims, preferred_element_type=F32),
                       MASK_VALUE)
        s2 = jnp.where(mask2, lax.dot_general(lhs, kn, nt_dims, preferred_element_type=F32),
                       MASK_VALUE)
        m = jnp.maximum(jnp.maximum(jnp.max(s1, axis=-1, keepdims=True),
                                    jnp.max(s2, axis=-1, keepdims=True)), sink)
        p1 = jnp.exp(s1 - m)
        p2 = jnp.exp(s2 - m)
        denom = (jnp.sum(p1, axis=-1, keepdims=True) + jnp.sum(p2, axis=-1, keepdims=True)
                 + jnp.exp(sink - m))
        o_all = (jnp.dot(p1.astype(BF16), vc, preferred_element_type=F32)
                 + jnp.dot(p2.astype(BF16), vn, preferred_element_type=F32)) * (1.0 / denom)
        for tt in range(nt):
            for g in range(GROUP):
                r0 = ((tt * GROUP + g) * N_KV_HEADS) * SUB
                o = jnp.where(low, o_all[r0:r0 + SUB], o_all[r0 + SUB:r0 + 2 * SUB])
                ocat[pl.ds(tt * gb + b0, SUB), 128 * g:128 * (g + 1)] = o
        return carry

    lax.fori_loop(0, gb // SUB, sub_block, 0)
    yc = ocat[...] * _silu(proj(O_ZA, D_IN))

    mixcat = jnp.concatenate([ya, yb, yc], axis=-1).astype(BF16)
    mix = jnp.dot(mixcat, wout_ref[0], preferred_element_type=F32).reshape(nt, gb, D_MODEL)
    y = ALPHA * x3 + (1.0 + gate)[None] * mix
    xo_ref[...] = _ln(y) * lng_ref[0][None] + lnb_ref[0][None]


def _decode_layer(lidx, x, mod_s, win, wout, conv_w, conv_b, cnorm_g, cnorm_b, wpool, pool_scale,
                  ln_g, ln_b, cos_s, sin_s, sink_rows, cache_conv, cache_pool, cache_k, cache_v,
                  conv_out, pool_out):
    nt, nbatch, _ = x.shape
    gb = DEC_GROUP
    carried = [] if conv_out is None else [conv_out, pool_out]
    per_layer = lambda shape: pl.BlockSpec((1,) + shape, lambda g, l: (l[0],) + (0,) * len(shape))
    tok = lambda c: pl.BlockSpec((nt, gb, c), lambda g, l: (0, g, 0))
    cache_blk = lambda r, c: pl.BlockSpec((1, gb, r, c), lambda g, l: (l[0], g, 0, 0))
    nrow = nt * GROUP * N_KV_HEADS * SUB
    grid_spec = pltpu.PrefetchScalarGridSpec(
        num_scalar_prefetch=1,
        grid=(nbatch // gb,),
        in_specs=[
            tok(D_MODEL),
            pl.BlockSpec((gb, 3 * D_MODEL), lambda g, l: (g, 0)),
            per_layer((D_MODEL, D_IN)),
            per_layer((D_MODEL, D_MODEL)),
            per_layer((CONV_WIDTH, D_CONV)),
            per_layer((1, D_CONV)),
            per_layer((1, D_CONV)),
            per_layer((1, D_CONV)),
            per_layer((D_POOL, D_POOL)),
            per_layer((1, D_POOL)),
            per_layer((1, D_MODEL)),
            per_layer((1, D_MODEL)),
            pl.BlockSpec((nt, 1, 128), lambda g, l: (0, 0, 0)),
            pl.BlockSpec((nt, 1, 128), lambda g, l: (0, 0, 0)),
            per_layer((nrow, 1)),
            cache_blk(CONV_BUF, D_CONV),
            cache_blk(POOL_BUF, D_POOL),
            cache_blk(WINDOW, D_KV),
            cache_blk(WINDOW, D_KV),
        ] + [pl.BlockSpec(memory_space=pl.ANY)] * len(carried),
        out_specs=[tok(D_MODEL), cache_blk(CONV_BUF, D_CONV), cache_blk(POOL_BUF, D_POOL),
                   tok(D_KV), tok(D_KV)],
        scratch_shapes=[
            pltpu.VMEM((nt * gb, D_ATTN), F32),
            pltpu.VMEM((nt * gb, D_ATTN), F32),
        ],
    )
    out_shape = [
        jax.ShapeDtypeStruct((nt, nbatch, D_MODEL), F32),
        jax.ShapeDtypeStruct(cache_conv.shape, F32),
        jax.ShapeDtypeStruct(cache_pool.shape, F32),
        jax.ShapeDtypeStruct((nt, nbatch, D_KV), F32),
        jax.ShapeDtypeStruct((nt, nbatch, D_KV), F32),
    ]
    operands = (lidx, x, mod_s, win, wout, conv_w, conv_b, cnorm_g, cnorm_b, wpool, pool_scale,
                ln_g, ln_b, cos_s, sin_s, sink_rows, cache_conv, cache_pool, cache_k, cache_v)
    n_in = len(operands)
    return pl.pallas_call(
        _decode_kernel,
        grid_spec=grid_spec,
        out_shape=out_shape,
        input_output_aliases={n_in: 1, n_in + 1: 2} if carried else {},
        compiler_params=pltpu.CompilerParams(
            dimension_semantics=("arbitrary",), vmem_limit_bytes=VMEM_LIMIT),
    )(*operands, *carried)


def _head_order():
    return [h * GROUP + g for g in range(GROUP) for h in range(N_KV_HEADS)]


def _rope_tables(pos):
    half = HEAD_DIM // 2
    inv_freq = ROPE_THETA ** (-jnp.arange(half, dtype=F32) * (2.0 / HEAD_DIM))
    ang = pos.astype(F32)[:, None] * inv_freq[None, :]
    cos = jnp.cos(ang)
    sin = jnp.sin(ang)
    cos_t = jnp.concatenate([cos, cos, cos, cos], axis=-1)
    sin_t = jnp.concatenate([-sin, sin, -sin, sin], axis=-1)
    return cos_t, sin_t


def kernel(x_prompt, x_sample, cache_conv, cache_pool, cache_k, cache_v, c_prompt, c_sample,
           w_in, w_out, conv_w, conv_b, cnorm_g, cnorm_b, pool_w, pool_scale, sinks,
           w_mod, b_mod, ln_g, ln_b):
    B, T, _ = x_prompt.shape
    nbatch, nt, _ = x_sample.shape
    wb = cache_k.shape[2]

    heads = _head_order()
    col = lambda w, o, hd: w[:, :, o + hd * HEAD_DIM:o + (hd + 1) * HEAD_DIM]
    win = jnp.concatenate(
        [w_in[:, :, :O_Q]] + [col(w_in, O_Q, hd) * (HEAD_DIM ** -0.5) for hd in heads]
        + [w_in[:, :, O_K:O_ZA]] + [col(w_in, O_ZA, hd) for hd in heads], axis=-1).astype(BF16)
    o_att = D_CONV + D_POOL
    wout = jnp.concatenate(
        [w_out[:, :o_att]]
        + [w_out[:, o_att + hd * HEAD_DIM:o_att + (hd + 1) * HEAD_DIM] for hd in heads],
        axis=1).astype(BF16)
    wpool = jnp.zeros((DEPTH, D_POOL, D_POOL), F32)
    for gi in range(len(POOL_WINDOWS)):
        sl = slice(gi * POOL_GROUP_DIM, (gi + 1) * POOL_GROUP_DIM)
        wpool = wpool.at[:, sl, sl].set(pool_w[:, gi])
    wpool = wpool.astype(BF16)
    sinks_gh = sinks.reshape(DEPTH, N_KV_HEADS, GROUP).transpose(0, 2, 1)
    sinks_flat = sinks_gh.reshape(DEPTH * N_HEADS)
    sink_rows = jnp.broadcast_to(
        sinks_gh[:, None, :, :, None], (DEPTH, nt, GROUP, N_KV_HEADS, SUB)
    ).reshape(DEPTH, nt * GROUP * N_KV_HEADS * SUB, 1)

    r3 = lambda p: p.reshape(DEPTH, 1, -1)
    conv_b3, cng3, cnb3, pscale3, lng3, lnb3 = map(r3, (conv_b, cnorm_g, cnorm_b, pool_scale, ln_g, ln_b))

    cos_p, sin_p = _rope_tables(jnp.arange(T, dtype=jnp.int32))
    cos_s, sin_s = _rope_tables(PAST_LEN + jnp.arange(nt, dtype=jnp.int32))
    cos_s = cos_s.reshape(nt, 1, 128)
    sin_s = sin_s.reshape(nt, 1, 128)

    mod = _modulation(jnp.concatenate([c_prompt, c_sample], axis=0), w_mod, b_mod)
    mod_p = mod[:, :B].reshape(DEPTH, B, 3, D_MODEL)
    mod_s = mod[:, B:]

    ck = cache_k.reshape(DEPTH, nbatch, wb, D_KV)
    cv = cache_v.reshape(DEPTH, nbatch, wb, D_KV)

    xp = x_prompt
    xs = x_sample.transpose(1, 0, 2)
    conv_p, pool_p, k_p, v_p, k_s, v_s = ([] for _ in range(6))
    conv_s = pool_s = None
    for l in range(DEPTH):
        lidx = jnp.full((1,), l, jnp.int32)
        xp, cp, pp, kp, vp = _prompt_layer(
            lidx, sinks_flat, xp, mod_p[l], win, wout, conv_w, conv_b3, cng3, cnb3, wpool, pscale3,
            lng3, lnb3, cos_p, sin_p)
        xs, conv_s, pool_s, kn, vn = _decode_layer(
            lidx, xs, mod_s[l], win, wout, conv_w, conv_b3, cng3, cnb3, wpool, pscale3,
            lng3, lnb3, cos_s, sin_s, sink_rows, cache_conv, cache_pool, ck, cv, conv_s, pool_s)
        conv_p.append(cp); pool_p.append(pp); k_p.append(kp); v_p.append(vp)
        k_s.append(kn); v_s.append(vn)

    kv5 = lambda z: z.reshape(z.shape[:-1] + (N_KV_HEADS, HEAD_DIM))
    roll5 = lambda cache, new: jnp.concatenate(
        [cache[:, :, nt:], kv5(jnp.stack(new).transpose(0, 2, 1, 3))], axis=2)
    k_samp = roll5(cache_k, k_s)
    v_samp = roll5(cache_v, v_s)
    return (xp, xs.transpose(1, 0, 2),
            jnp.stack(conv_p), jnp.stack(pool_p), kv5(jnp.stack(k_p)), kv5(jnp.stack(v_p)),
            conv_s, pool_s, k_samp, v_samp)
```

```python
import functools

import jax
import jax.numpy as jnp
from jax import lax
from jax.experimental import pallas as pl
from jax.experimental.pallas import tpu as pltpu

F32 = jnp.float32
BF16 = jnp.bfloat16

D_MODEL = 1024
DEPTH = 4
D_CONV = 256
D_POOL = 256
D_ATTN = 512
HEAD_DIM = 64
N_HEADS = 8
N_KV_HEADS = 2
GROUP = 4
D_KV = 128
WINDOW = 128
BLOCK = 128
CONV_WIDTH = 31
CONV_BUF = 30
POOL_WINDOWS = (2, 4, 8, 16)
POOL_GROUP_DIM = 64
POOL_BUF = 15
ROPE_THETA = 10000.0
LN_EPS = 1e-5
ALPHA = (2.0 * DEPTH) ** 0.25
MASK_VALUE = -1e30
PAST_LEN = 8192

O_U, O_G, O_ZC, O_PV, O_ZP, O_Q, O_K, O_V, O_ZA, D_IN = (
    0, 256, 512, 768, 1024, 1280, 1792, 1920, 2048, 2560)

SUBLANES = 8
CARRY = 32
TM = 512
DEC_GROUP = 32
SUB = 8
VMEM_LIMIT = 48 * 1024 * 1024


def _sigmoid(x):
    return 1.0 / (1.0 + jnp.exp(-x))


def _silu(x):
    return x * _sigmoid(x)


def _ln(x):
    mu = jnp.mean(x, axis=-1, keepdims=True)
    xc = x - mu
    var = jnp.mean(xc * xc, axis=-1, keepdims=True)
    return xc * lax.rsqrt(var + LN_EPS)


def _rope(x, cos, sin_signed):
    lane = lax.broadcasted_iota(jnp.int32, x.shape, 1)
    first_half = (lane & 63) < 32
    rot = jnp.where(first_half, pltpu.roll(x, 96, axis=1), pltpu.roll(x, 32, axis=1))
    return x * cos + rot * sin_signed


def _swap_halves(x):
    return pltpu.roll(x, HEAD_DIM, axis=1)


def _group_heads(nat):
    low = lax.broadcasted_iota(jnp.int32, nat[0].shape, 1) < HEAD_DIM
    out = []
    for g in range(GROUP):
        a, b = nat[g // 2], nat[GROUP // 2 + g // 2]
        out.append(jnp.where(low, a, _swap_halves(b)) if g % 2 == 0
                   else jnp.where(low, _swap_halves(a), b))
    return out


def _ungroup_heads(grouped):
    low = lax.broadcasted_iota(jnp.int32, grouped[0].shape, 1) < HEAD_DIM
    nat = [None] * GROUP
    for c in range(GROUP // 2):
        even, odd = grouped[2 * c], grouped[2 * c + 1]
        nat[c] = jnp.where(low, even, _swap_halves(odd))
        nat[GROUP // 2 + c] = jnp.where(low, _swap_halves(even), odd)
    return nat


def _mod_kernel(c_ref, w_ref, b_ref, o_ref):
    c = c_ref[...]
    sc = _silu(c).astype(BF16)
    o_ref[0] = jnp.dot(sc, w_ref[0].astype(BF16), preferred_element_type=F32) + b_ref[0]


def _modulation(c_all, w_mod, b_mod):
    n = c_all.shape[0]
    tn = 768
    return pl.pallas_call(
        _mod_kernel,
        grid=(DEPTH, 3 * D_MODEL // tn),
        in_specs=[
            pl.BlockSpec((n, D_MODEL), lambda l, j: (0, 0)),
            pl.BlockSpec((1, D_MODEL, tn), lambda l, j: (l, 0, j)),
            pl.BlockSpec((1, 1, tn), lambda l, j: (l, 0, j)),
        ],
        out_specs=pl.BlockSpec((1, n, tn), lambda l, j: (l, 0, j)),
        out_shape=jax.ShapeDtypeStruct((DEPTH, n, 3 * D_MODEL), F32),
        compiler_params=pltpu.CompilerParams(
            dimension_semantics=("arbitrary", "arbitrary"), vmem_limit_bytes=VMEM_LIMIT),
    )(c_all, w_mod, b_mod.reshape(DEPTH, 1, 3 * D_MODEL))


def _prompt_kernel(l_ref, sink_ref,
                   x_ref, mod_ref, win_ref, wout_ref, convw_ref, convb_ref, cng_ref, cnb_ref,
                   wpool_ref, pscale_ref, lng_ref, lnb_ref, cos_ref, sin_ref,
                   xo_ref, convo_ref, poolo_ref, ko_ref, vo_ref,
                   acat, ashift, pcat, ps_a, ps_b, kcat0, kcat1, vcat0, vcat1, mixcat):
    del l_ref
    kcat, vcat = (kcat0, kcat1), (vcat0, vcat1)
    t = pl.program_id(1)
    last = pl.num_programs(1) - 1
    nb = TM // BLOCK

    @pl.when(t == 0)
    def _():
        acat[0:CARRY, :] = jnp.zeros((CARRY, D_CONV), F32)
        pcat[0:CARRY, :] = jnp.zeros((CARRY, D_POOL), F32)
        for kc, vc in zip(kcat, vcat):
            kc[0:BLOCK, :] = jnp.zeros((BLOCK, D_KV), BF16)
            vc[0:BLOCK, 0:D_KV] = jnp.zeros((BLOCK, D_KV), BF16)
            vc[:, D_KV:2 * D_KV] = jnp.ones((BLOCK + TM, D_KV), BF16)

    kprev = [kc[0:BLOCK, :] for kc in kcat]
    vprev = [vc[0:BLOCK, :] for vc in vcat]
    x = x_ref[0]
    shift = mod_ref[0, 0, 0:1, :]
    scale = mod_ref[0, 0, 1:2, :]
    gate = mod_ref[0, 0, 2:3, :]
    hb = (_ln(x) * (1.0 + scale) + shift).astype(BF16)

    def proj(lo, hi):
        return jnp.dot(hb, win_ref[0, :, lo:hi], preferred_element_type=F32)

    a = proj(O_U, O_G) * _sigmoid(proj(O_G, O_ZC))
    acat[CARRY:CARRY + TM, :] = a
    zc = proj(O_ZC, O_PV)
    for r in range(1, SUBLANES):
        ashift[r - 1] = acat[r:r + TM + CARRY - SUBLANES, :]
    rows = 64
    for c in range(TM // rows):
        acc = jnp.zeros((rows, D_CONV), F32)
        for j in range(CONV_WIDTH):
            off = CARRY - CONV_BUF + j
            r, lo = off % SUBLANES, c * rows + off - off % SUBLANES
            tap = acat[lo:lo + rows, :] if r == 0 else ashift[r - 1, lo:lo + rows, :]
            acc = acc + tap * convw_ref[0, j:j + 1, :]
        conv = acc + convb_ref[0]
        ya = _silu(_ln(conv) * cng_ref[0] + cnb_ref[0])
        mixcat[c * rows:(c + 1) * rows, 0:D_CONV] = (
            ya * _silu(zc[c * rows:(c + 1) * rows])).astype(BF16)

    @pl.when(t == last)
    def _():
        convo_ref[0] = acat[CARRY + TM - CONV_BUF:CARRY + TM, :]

    acat[0:CARRY, :] = acat[TM:TM + CARRY, :]

    pv = proj(O_PV, O_ZP)
    pcat[CARRY:CARRY + TM, :] = pv
    n = CARRY + TM
    ps_a[8:n, :] = pcat[8:n, :] + pcat[7:n - 1, :]
    ps_b[16:n, :] = ps_a[16:n, :] + ps_a[14:n - 2, :]
    s2 = ps_a[CARRY:n, :]
    s4 = ps_b[CARRY:n, :]
    ps_a[24:n, :] = ps_b[24:n, :] + ps_b[20:n - 4, :]
    s8 = ps_a[CARRY:n, :]
    s16 = s8 + ps_a[CARRY - 8:n - 8, :]
    lane = lax.broadcasted_iota(jnp.int32, (TM, D_POOL), 1)
    sums = jnp.where(lane < 64, s2, jnp.where(lane < 128, s4, jnp.where(lane < 192, s8, s16)))
    wlen = jnp.where(lane < 64, 2.0, jnp.where(lane < 128, 4.0, jnp.where(lane < 192, 8.0, 16.0)))
    pos1 = (lax.broadcasted_iota(jnp.int32, (TM, D_POOL), 0) + (t * TM + 1)).astype(F32)
    cnt = jnp.minimum(wlen, pos1)
    pooled = (sums / cnt - pv).astype(BF16)
    yb = jnp.dot(pooled, wpool_ref[0], preferred_element_type=F32) * pscale_ref[0]
    mixcat[:, D_CONV:D_CONV + D_POOL] = (yb * _silu(proj(O_ZP, O_Q))).astype(BF16)

    @pl.when(t == last)
    def _():
        poolo_ref[0] = pcat[CARRY + TM - POOL_BUF:CARRY + TM, :]

    pcat[0:CARRY, :] = pcat[TM:TM + CARRY, :]

    cos = cos_ref[...]
    sin = sin_ref[...]
    k = _rope(proj(O_K, O_V), cos, sin)
    v = proj(O_V, O_ZA)
    kcat[0][BLOCK:BLOCK + TM, :] = k.astype(BF16)
    kcat[1][BLOCK:BLOCK + TM, :] = _swap_halves(k).astype(BF16)
    vcat[0][BLOCK:BLOCK + TM, 0:D_KV] = v.astype(BF16)
    vcat[1][BLOCK:BLOCK + TM, 0:D_KV] = _swap_halves(v).astype(BF16)

    @pl.when(t == last)
    def _():
        ko_ref[0] = k[TM - WINDOW:TM, :]
        vo_ref[0] = v[TM - WINDOW:TM, :]

    row = lax.broadcasted_iota(jnp.int32, (BLOCK, 2 * BLOCK), 0)
    col = lax.broadcasted_iota(jnp.int32, (BLOCK, 2 * BLOCK), 1)
    rel = col - BLOCK - row
    band = (rel <= 0) & (rel > -WINDOW)
    band_first = band & (col + jnp.minimum(t, 1) * BLOCK >= BLOCK)
    lane_q = lax.broadcasted_iota(jnp.int32, (BLOCK, 2 * HEAD_DIM), 1)
    low = lane_q < HEAD_DIM

    q = proj(O_Q, O_K) * (HEAD_DIM ** -0.5)
    za = proj(O_ZA, D_IN)
    nq = D_ATTN // 128
    qr = [_rope(q[:, 128 * c:128 * (c + 1)], cos, sin) for c in range(nq)]
    same = [(c, (2 * c) // GROUP) for c in range(nq)]
    other = [(c, 1 - (2 * c) // GROUP) for c in range(nq)]
    sink = sink_ref[0]
    nt_dims = (((1,), (1,)), ((), ()))
    for i in range(nb):
        mask = band if i > 0 else band_first
        scores = []
        for copy, heads in ((0, same), (1, other)):
            lhs = jnp.concatenate(
                [jnp.where(low == (half == 0), qr[c][i * BLOCK:(i + 1) * BLOCK], 0.0)
                 for c, half in heads], axis=0).astype(BF16)
            kk = (kcat[copy][i * BLOCK:(i + 2) * BLOCK, :] if i > 0 else
                  jnp.concatenate([kprev[copy], kcat[copy][BLOCK:2 * BLOCK, :]], axis=0))
            scores.append(lax.dot_general(lhs, kk, nt_dims, preferred_element_type=F32))
        s = jnp.concatenate(scores, axis=0).reshape(N_HEADS, BLOCK, 2 * BLOCK)
        s = jnp.where(mask[None], s, MASK_VALUE)
        m = jnp.maximum(jnp.max(s, axis=-1, keepdims=True), sink)
        mb = jnp.broadcast_to(m, (N_HEADS, BLOCK, BLOCK))
        p = jnp.concatenate([jnp.exp(s[:, :, 0:BLOCK] - mb), jnp.exp(s[:, :, BLOCK:] - mb)],
                            axis=-1).astype(BF16)
        sink_p = jnp.exp(sink - mb)
        half_rows = (N_HEADS // 2) * BLOCK
        outs = []
        for copy in range(2):
            pv = jnp.dot(p[copy * (N_HEADS // 2):(copy + 1) * (N_HEADS // 2)].reshape(
                half_rows, 2 * BLOCK),
                (vcat[copy][i * BLOCK:(i + 2) * BLOCK, :] if i > 0 else
                 jnp.concatenate([vprev[copy], vcat[copy][BLOCK:2 * BLOCK, :]], axis=0)),
                preferred_element_type=F32).reshape(N_HEADS // 2, BLOCK, 2 * D_KV)
            den = pv[:, :, D_KV:] + sink_p[copy * (N_HEADS // 2):(copy + 1) * (N_HEADS // 2)]
            outs.append(pv[:, :, 0:D_KV] / den)
        for c in range(nq):
            o = jnp.where(low == (same[c][1] == 0), outs[0][c], outs[1][c])
            zg = za[i * BLOCK:(i + 1) * BLOCK, 128 * c:128 * (c + 1)]
            c0 = D_CONV + D_POOL + 128 * c
            mixcat[i * BLOCK:(i + 1) * BLOCK, c0:c0 + 128] = (o * _silu(zg)).astype(BF16)

    kcat[0][0:BLOCK, :] = k[TM - BLOCK:TM].astype(BF16)
    kcat[1][0:BLOCK, :] = _swap_halves(k[TM - BLOCK:TM]).astype(BF16)
    vcat[0][0:BLOCK, 0:D_KV] = v[TM - BLOCK:TM].astype(BF16)
    vcat[1][0:BLOCK, 0:D_KV] = _swap_halves(v[TM - BLOCK:TM]).astype(BF16)

    mix = jnp.dot(mixcat[...], wout_ref[0], preferred_element_type=F32)
    y = ALPHA * x + (1.0 + gate) * mix
    xo_ref[0] = _ln(y) * lng_ref[0] + lnb_ref[0]


def _prompt_layer(lidx, sinks_perm, x, mod_p, win, wout, conv_w, conv_b, cnorm_g, cnorm_b,
                  wpool, pool_scale, ln_g, ln_b, cos_t, sin_t):
    B, T, _ = x.shape
    nt = T // TM
    per_layer = lambda shape: pl.BlockSpec((1,) + shape, lambda b, t, l: (l[0],) + (0,) * len(shape))
    grid_spec = pltpu.PrefetchScalarGridSpec(
        num_scalar_prefetch=1,
        grid=(B, nt),
        in_specs=[
            per_layer((N_HEADS, 1, 1)),
            pl.BlockSpec((1, TM, D_MODEL), lambda b, t, l: (b, t, 0)),
            pl.BlockSpec((1, 1, 3, D_MODEL), lambda b, t, l: (l[0], b, 0, 0)),
            per_layer((D_MODEL, D_IN)),
            per_layer((D_MODEL, D_MODEL)),
            per_layer((CONV_WIDTH, D_CONV)),
            per_layer((1, D_CONV)),
            per_layer((1, D_CONV)),
            per_layer((1, D_CONV)),
            per_layer((D_POOL, D_POOL)),
            per_layer((1, D_POOL)),
            per_layer((1, D_MODEL)),
            per_layer((1, D_MODEL)),
            pl.BlockSpec((TM, 128), lambda b, t, l: (t, 0)),
            pl.BlockSpec((TM, 128), lambda b, t, l: (t, 0)),
        ],
        out_specs=[
            pl.BlockSpec((1, TM, D_MODEL), lambda b, t, l: (b, t, 0)),
            pl.BlockSpec((1, CONV_BUF, D_CONV), lambda b, t, l: (b, 0, 0)),
            pl.BlockSpec((1, POOL_BUF, D_POOL), lambda b, t, l: (b, 0, 0)),
            pl.BlockSpec((1, WINDOW, D_KV), lambda b, t, l: (b, 0, 0)),
            pl.BlockSpec((1, WINDOW, D_KV), lambda b, t, l: (b, 0, 0)),
        ],
        scratch_shapes=[
            pltpu.VMEM((CARRY + TM, D_CONV), F32),
            pltpu.VMEM((SUBLANES - 1, CARRY + TM - SUBLANES, D_CONV), F32),
            pltpu.VMEM((CARRY + TM, D_POOL), F32),
            pltpu.VMEM((CARRY + TM, D_POOL), F32),
            pltpu.VMEM((CARRY + TM, D_POOL), F32),
            pltpu.VMEM((BLOCK + TM, D_KV), BF16),
            pltpu.VMEM((BLOCK + TM, D_KV), BF16),
            pltpu.VMEM((BLOCK + TM, 2 * D_KV), BF16),
            pltpu.VMEM((BLOCK + TM, 2 * D_KV), BF16),
            pltpu.VMEM((TM, D_MODEL), BF16),
        ],
    )
    out_shape = [
        jax.ShapeDtypeStruct((B, T, D_MODEL), F32),
        jax.ShapeDtypeStruct((B, CONV_BUF, D_CONV), F32),
        jax.ShapeDtypeStruct((B, POOL_BUF, D_POOL), F32),
        jax.ShapeDtypeStruct((B, WINDOW, D_KV), F32),
        jax.ShapeDtypeStruct((B, WINDOW, D_KV), F32),
    ]
    return pl.pallas_call(
        _prompt_kernel,
        grid_spec=grid_spec,
        out_shape=out_shape,
        compiler_params=pltpu.CompilerParams(
            dimension_semantics=("arbitrary", "arbitrary"), vmem_limit_bytes=VMEM_LIMIT),
    )(lidx, sinks_perm, x, mod_p, win, wout, conv_w, conv_b, cnorm_g, cnorm_b,
      wpool, pool_scale, ln_g, ln_b, cos_t, sin_t)


def _decode_kernel(l_ref,
                   x_ref, mod_ref, win_ref, wout_ref, convw_ref, convb_ref, cng_ref, cnb_ref,
                   wpool_ref, pscale_ref, lng_ref, lnb_ref, cos_ref, sin_ref, sinkrow_ref,
                   cconv_ref, cpool_ref, ck_ref, cv_ref, *rest):
    xo_ref, convo_ref, poolo_ref, ko_ref, vo_ref, qs, ocat, kn_s, vn_s, kbt, vbt, knt, vnt = rest[-13:]
    gb = DEC_GROUP
    nt = x_ref.shape[0]
    x3 = x_ref[...]
    shift = mod_ref[0, :, 0:D_MODEL]
    scale = mod_ref[0, :, D_MODEL:2 * D_MODEL]
    gate = mod_ref[0, :, 2 * D_MODEL:3 * D_MODEL]
    h3 = _ln(x3) * (1.0 + scale)[None] + shift[None]
    hb = h3.reshape(nt * gb, D_MODEL).astype(BF16)

    def proj(lo, hi):
        return jnp.dot(hb, win_ref[0, :, lo:hi], preferred_element_type=F32)

    a = proj(O_U, O_G) * _sigmoid(proj(O_G, O_ZC))
    convo_ref[0, 0:CONV_BUF - nt] = cconv_ref[0, nt:CONV_BUF]
    convo_ref[0, CONV_BUF - nt:CONV_BUF] = a.reshape(nt, gb, D_CONV)
    zc = proj(O_ZC, O_PV)
    ya_rows = []
    for tt in range(nt):
        acc = jnp.zeros((gb, D_CONV), F32)
        for j in range(CONV_WIDTH):
            i = tt + j
            xi = (cconv_ref[0, i] if i < CONV_BUF
                  else a[(i - CONV_BUF) * gb:(i - CONV_BUF + 1) * gb])
            acc = acc + xi * convw_ref[0, j:j + 1, :]
        conv = acc + convb_ref[0]
        ya_rows.append(_silu(_ln(conv) * cng_ref[0] + cnb_ref[0]))
    ya = jnp.concatenate(ya_rows, axis=0) * _silu(zc)

    pv = proj(O_PV, O_ZP)
    poolo_ref[0, 0:POOL_BUF - nt] = cpool_ref[0, nt:POOL_BUF]
    poolo_ref[0, POOL_BUF - nt:POOL_BUF] = pv.reshape(nt, gb, D_POOL)
    lane = lax.broadcasted_iota(jnp.int32, (gb, D_POOL), 1)
    wlen = jnp.where(lane < 64, 2, jnp.where(lane < 128, 4, jnp.where(lane < 192, 8, 16)))
    pooled_rows = []
    for tt in range(nt):
        acc = jnp.zeros((gb, D_POOL), F32)
        for d in range(max(POOL_WINDOWS)):
            i = POOL_BUF + tt - d
            xi = (cpool_ref[0, i] if i < POOL_BUF
                  else pv[(i - POOL_BUF) * gb:(i - POOL_BUF + 1) * gb])
            acc = acc + (xi if d < min(POOL_WINDOWS) else jnp.where(wlen > d, xi, 0.0))
        cnt = jnp.minimum(wlen, PAST_LEN + tt + 1).astype(F32)
        pooled_rows.append(acc / cnt - pv[tt * gb:(tt + 1) * gb])
    pooled = jnp.concatenate(pooled_rows, axis=0).astype(BF16)
    yb = jnp.dot(pooled, wpool_ref[0], preferred_element_type=F32) * pscale_ref[0]
    yb = yb * _silu(proj(O_ZP, O_Q))

    cos3 = cos_ref[...]
    sin3 = sin_ref[...]
    cos = jnp.broadcast_to(cos3, (nt, gb, 128)).reshape(nt * gb, 128)
    sin = jnp.broadcast_to(sin3, (nt, gb, 128)).reshape(nt * gb, 128)
    k = _rope(proj(O_K, O_V), cos, sin)
    v = proj(O_V, O_ZA)
    kn_s[...] = k
    vn_s[...] = v
    q = proj(O_Q, O_K) * (HEAD_DIM ** -0.5)
    for g, qg in enumerate(_group_heads([q[:, 128 * c:128 * (c + 1)] for c in range(GROUP)])):
        qs[:, 128 * g:128 * (g + 1)] = _rope(qg, cos, sin)

    for new_rows, by_batch, new_t in ((k, kbt, knt), (v, vbt, vnt)):
        for tt in range(nt):
            by_batch[pl.ds(tt, gb, stride=nt), :] = new_rows[tt * gb:(tt + 1) * gb]
        new_t[...] = by_batch[...].T
    per_blk = 128 // nt
    lane_kv = lax.broadcasted_iota(jnp.int32, (D_KV, WINDOW), 1)

    for blk in range(gb // per_blk):
        def roll_cache(j, carry, blk=blk):
            b = blk * per_blk + j
            shift = (WINDOW - nt - nt * j) & (WINDOW - 1)
            for new_t, cache_ref, out_ref in ((knt, ck_ref, ko_ref), (vnt, cv_ref, vo_ref)):
                old = pltpu.roll(cache_ref[0, b].reshape(D_KV, WINDOW), WINDOW - nt, axis=1)
                new = pltpu.roll(new_t[:, 128 * blk:128 * (blk + 1)], shift, axis=1)
                out_ref[0, b] = jnp.where(lane_kv >= WINDOW - nt, new, old).reshape(
                    N_KV_HEADS, HEAD_DIM, WINDOW)
            return carry
        lax.fori_loop(0, per_blk, roll_cache, 0)

    nrow = nt * GROUP * N_KV_HEADS * SUB
    r1 = lax.broadcasted_iota(jnp.int32, (nrow, SUB * WINDOW), 0)
    c1 = lax.broadcasted_iota(jnp.int32, (nrow, SUB * WINDOW), 1)
    mask1 = ((c1 >> 7) == (r1 & (SUB - 1))) & ((c1 & (WINDOW - 1)) > (r1 >> 6))
    r2 = lax.broadcasted_iota(jnp.int32, (nrow, nt * SUB), 0)
    c2 = lax.broadcasted_iota(jnp.int32, (nrow, nt * SUB), 1)
    mask2 = ((c2 & (SUB - 1)) == (r2 & (SUB - 1))) & ((c2 >> 3) <= (r2 >> 6))
    lane_q = lax.broadcasted_iota(jnp.int32, (SUB, 128), 1)
    low = lane_q < HEAD_DIM
    sink = sinkrow_ref[0]

    def sub_block(sb, carry):
        b0 = pl.multiple_of(sb * SUB, SUB)
        pieces = []
        for tt in range(nt):
            for g in range(GROUP):
                qp = qs[pl.ds(tt * gb + b0, SUB), 128 * g:128 * (g + 1)]
                pieces.append(jnp.where(low, qp, 0.0))
                pieces.append(jnp.where(low, 0.0, qp))
        lhs = jnp.concatenate(pieces, axis=0).astype(BF16)
        kblk = ck_ref[0, pl.ds(b0, SUB)]
        vblk = cv_ref[0, pl.ds(b0, SUB)]
        kc = jnp.concatenate(
            [kblk[bb].reshape(D_KV, WINDOW) for bb in range(SUB)], axis=1).astype(BF16)
        vc = jnp.concatenate(
            [vblk[bb].reshape(D_KV, WINDOW) for bb in range(SUB)], axis=1).astype(BF16)
        kn = jnp.concatenate(
            [kn_s[pl.ds(tt * gb + b0, SUB), :] for tt in range(nt)], axis=0).astype(BF16)
        vn = jnp.concatenate(
            [vn_s[pl.ds(tt * gb + b0, SUB), :] for tt in range(nt)], axis=0).astype(BF16)
        nt_dims = (((1,), (1,)), ((), ()))
        s1 = jnp.where(mask1, jnp.dot(lhs, kc, preferred_element_type=F32), MASK_VALUE)
        s2 = jnp.where(mask2, lax.dot_general(lhs, kn, nt_dims, preferred_element_type=F32),
                       MASK_VALUE)
        m = jnp.maximum(jnp.maximum(jnp.max(s1, axis=-1, keepdims=True),
                                    jnp.max(s2, axis=-1, keepdims=True)), sink)
        p1 = jnp.exp(s1 - m)
        p2 = jnp.exp(s2 - m)
        denom = (jnp.sum(p1, axis=-1, keepdims=True) + jnp.sum(p2, axis=-1, keepdims=True)
                 + jnp.exp(sink - m))
        o_all = (lax.dot_general(p1.astype(BF16), vc, nt_dims, preferred_element_type=F32)
                 + jnp.dot(p2.astype(BF16), vn, preferred_element_type=F32)) * (1.0 / denom)
        for tt in range(nt):
            for g in range(GROUP):
                r0 = ((tt * GROUP + g) * N_KV_HEADS) * SUB
                o = jnp.where(low, o_all[r0:r0 + SUB], o_all[r0 + SUB:r0 + 2 * SUB])
                ocat[pl.ds(tt * gb + b0, SUB), 128 * g:128 * (g + 1)] = o
        return carry

    lax.fori_loop(0, gb // SUB, sub_block, 0)
    o_nat = _ungroup_heads([ocat[:, 128 * g:128 * (g + 1)] for g in range(GROUP)])
    yc = jnp.concatenate(o_nat, axis=-1) * _silu(proj(O_ZA, D_IN))

    mixcat = jnp.concatenate([ya, yb, yc], axis=-1).astype(BF16)
    mix = jnp.dot(mixcat, wout_ref[0], preferred_element_type=F32).reshape(nt, gb, D_MODEL)
    y = ALPHA * x3 + (1.0 + gate)[None] * mix
    xo_ref[...] = _ln(y) * lng_ref[0][None] + lnb_ref[0][None]


def _decode_layer(lidx, x, mod_s, win, wout, conv_w, conv_b, cnorm_g, cnorm_b, wpool, pool_scale,
                  ln_g, ln_b, cos_s, sin_s, sink_rows, cache_conv, cache_pool, cache_k, cache_v,
                  rolled):
    nt, nbatch, _ = x.shape
    gb = DEC_GROUP
    carried = list(rolled)
    per_layer = lambda shape: pl.BlockSpec((1,) + shape, lambda g, l: (l[0],) + (0,) * len(shape))
    tok = lambda c: pl.BlockSpec((nt, gb, c), lambda g, l: (0, g, 0))
    time_major = lambda r, c: pl.BlockSpec((1, r, gb, c), lambda g, l: (l[0], 0, g, 0))
    kv_blk = pl.BlockSpec((1, gb, N_KV_HEADS, HEAD_DIM, WINDOW), lambda g, l: (l[0], g, 0, 0, 0))
    nrow = nt * GROUP * N_KV_HEADS * SUB
    grid_spec = pltpu.PrefetchScalarGridSpec(
        num_scalar_prefetch=1,
        grid=(nbatch // gb,),
        in_specs=[
            tok(D_MODEL),
            pl.BlockSpec((1, gb, 3 * D_MODEL), lambda g, l: (l[0], g, 0)),
            per_layer((D_MODEL, D_IN)),
            per_layer((D_MODEL, D_MODEL)),
            per_layer((CONV_WIDTH, D_CONV)),
            per_layer((1, D_CONV)),
            per_layer((1, D_CONV)),
            per_layer((1, D_CONV)),
            per_layer((D_POOL, D_POOL)),
            per_layer((1, D_POOL)),
            per_layer((1, D_MODEL)),
            per_layer((1, D_MODEL)),
            pl.BlockSpec((nt, 1, 128), lambda g, l: (0, 0, 0)),
            pl.BlockSpec((nt, 1, 128), lambda g, l: (0, 0, 0)),
            per_layer((nrow, 1)),
            time_major(CONV_BUF, D_CONV),
            time_major(POOL_BUF, D_POOL),
            kv_blk,
            kv_blk,
        ] + [pl.BlockSpec(memory_space=pl.ANY)] * len(carried),
        out_specs=[tok(D_MODEL), time_major(CONV_BUF, D_CONV), time_major(POOL_BUF, D_POOL),
                   kv_blk, kv_blk],
        scratch_shapes=[
            pltpu.VMEM((nt * gb, D_ATTN), F32),
            pltpu.VMEM((nt * gb, D_ATTN), F32),
            pltpu.VMEM((nt * gb, D_KV), F32),
            pltpu.VMEM((nt * gb, D_KV), F32),
            pltpu.VMEM((nt * gb, D_KV), F32),
            pltpu.VMEM((nt * gb, D_KV), F32),
            pltpu.VMEM((D_KV, nt * gb), F32),
            pltpu.VMEM((D_KV, nt * gb), F32),
        ],
    )
    out_shape = [jax.ShapeDtypeStruct((nt, nbatch, D_MODEL), F32)] + [
        jax.ShapeDtypeStruct(c.shape, F32) for c in (cache_conv, cache_pool, cache_k, cache_v)]
    operands = (lidx, x, mod_s, win, wout, conv_w, conv_b, cnorm_g, cnorm_b, wpool, pool_scale,
                ln_g, ln_b, cos_s, sin_s, sink_rows, cache_conv, cache_pool, cache_k, cache_v)
    n_in = len(operands)
    return pl.pallas_call(
        _decode_kernel,
        grid_spec=grid_spec,
        out_shape=out_shape,
        input_output_aliases={n_in + i: 1 + i for i in range(len(carried))},
        compiler_params=pltpu.CompilerParams(
            dimension_semantics=("arbitrary",), vmem_limit_bytes=VMEM_LIMIT),
    )(*operands, *carried)


def _rope_tables(pos):
    half = HEAD_DIM // 2
    inv_freq = ROPE_THETA ** (-jnp.arange(half, dtype=F32) * (2.0 / HEAD_DIM))
    ang = pos.astype(F32)[:, None] * inv_freq[None, :]
    cos = jnp.cos(ang)
    sin = jnp.sin(ang)
    cos_t = jnp.concatenate([cos, cos, cos, cos], axis=-1)
    sin_t = jnp.concatenate([-sin, sin, -sin, sin], axis=-1)
    return cos_t, sin_t


def kernel(x_prompt, x_sample, cache_conv, cache_pool, cache_k, cache_v, c_prompt, c_sample,
           w_in, w_out, conv_w, conv_b, cnorm_g, cnorm_b, pool_w, pool_scale, sinks,
           w_mod, b_mod, ln_g, ln_b):
    B, T, _ = x_prompt.shape
    nbatch, nt, _ = x_sample.shape

    win = w_in.astype(BF16)
    wout = w_out.astype(BF16)
    wpool = jnp.zeros((DEPTH, D_POOL, D_POOL), F32)
    for gi in range(len(POOL_WINDOWS)):
        sl = slice(gi * POOL_GROUP_DIM, (gi + 1) * POOL_GROUP_DIM)
        wpool = wpool.at[:, sl, sl].set(pool_w[:, gi])
    wpool = wpool.astype(BF16)
    sinks_gh = sinks.reshape(DEPTH, N_KV_HEADS, GROUP).transpose(0, 2, 1)
    nq = D_ATTN // 128
    order = ([2 * c + (2 * c) // GROUP for c in range(nq)]
             + [2 * c + 1 - (2 * c) // GROUP for c in range(nq)])
    sinks_flat = jnp.stack([sinks[:, hd] for hd in order], axis=1).reshape(DEPTH, N_HEADS, 1, 1)
    sink_rows = jnp.broadcast_to(
        sinks_gh[:, None, :, :, None], (DEPTH, nt, GROUP, N_KV_HEADS, SUB)
    ).reshape(DEPTH, nt * GROUP * N_KV_HEADS * SUB, 1)

    r3 = lambda p: p.reshape(DEPTH, 1, -1)
    conv_b3, cng3, cnb3, pscale3, lng3, lnb3 = map(r3, (conv_b, cnorm_g, cnorm_b, pool_scale, ln_g, ln_b))

    cos_p, sin_p = _rope_tables(jnp.arange(T, dtype=jnp.int32))
    cos_s, sin_s = _rope_tables(PAST_LEN + jnp.arange(nt, dtype=jnp.int32))
    cos_s = cos_s.reshape(nt, 1, 128)
    sin_s = sin_s.reshape(nt, 1, 128)

    mod = _modulation(jnp.concatenate([c_sample, c_prompt], axis=0), w_mod, b_mod)
    mod_p = mod[:, nbatch:].reshape(DEPTH, B, 3, D_MODEL)

    cconv_t = cache_conv.transpose(0, 2, 1, 3)
    cpool_t = cache_pool.transpose(0, 2, 1, 3)
    ck_t = cache_k.transpose(0, 1, 3, 4, 2)
    cv_t = cache_v.transpose(0, 1, 3, 4, 2)

    xp = x_prompt
    xs = x_sample.transpose(1, 0, 2)
    conv_p, pool_p, k_p, v_p = ([] for _ in range(4))
    rolled = ()
    for l in range(DEPTH):
        lidx = jnp.full((1,), l, jnp.int32)
        xp, cp, pp, kp, vp = _prompt_layer(
            lidx, sinks_flat, xp, mod_p, win, wout, conv_w, conv_b3, cng3, cnb3, wpool, pscale3,
            lng3, lnb3, cos_p, sin_p)
        xs, *rolled = _decode_layer(
            lidx, xs, mod, win, wout, conv_w, conv_b3, cng3, cnb3, wpool, pscale3,
            lng3, lnb3, cos_s, sin_s, sink_rows, cconv_t, cpool_t, ck_t, cv_t, rolled)
        conv_p.append(cp); pool_p.append(pp); k_p.append(kp); v_p.append(vp)

    kv5 = lambda z: z.reshape(z.shape[:-1] + (N_KV_HEADS, HEAD_DIM))
    conv_s, pool_s, k_s, v_s = rolled
    return (xp, xs.transpose(1, 0, 2),
            jnp.stack(conv_p), jnp.stack(pool_p), kv5(jnp.stack(k_p)), kv5(jnp.stack(v_p)),
            conv_s.transpose(0, 2, 1, 3), pool_s.transpose(0, 2, 1, 3),
            k_s.transpose(0, 1, 4, 2, 3), v_s.transpose(0, 1, 4, 2, 3))
```

```python
import functools

import jax
import jax.numpy as jnp
from jax import lax
from jax.experimental import pallas as pl
from jax.experimental.pallas import tpu as pltpu

F32 = jnp.float32
BF16 = jnp.bfloat16

D_MODEL = 1024
DEPTH = 4
D_CONV = 256
D_POOL = 256
D_ATTN = 512
HEAD_DIM = 64
N_HEADS = 8
N_KV_HEADS = 2
GROUP = 4
D_KV = 128
WINDOW = 128
BLOCK = 128
CONV_WIDTH = 31
CONV_BUF = 30
POOL_WINDOWS = (2, 4, 8, 16)
POOL_GROUP_DIM = 64
POOL_BUF = 15
ROPE_THETA = 10000.0
LN_EPS = 1e-5
ALPHA = (2.0 * DEPTH) ** 0.25
MASK_VALUE = -1e30
PAST_LEN = 8192

O_U, O_G, O_ZC, O_PV, O_ZP, O_Q, O_K, O_V, O_ZA, D_IN = (
    0, 256, 512, 768, 1024, 1280, 1792, 1920, 2048, 2560)

SUBLANES = 8
CARRY = 32
TM = 512
DEC_GROUP = 32
SUB = 8
VMEM_LIMIT = 48 * 1024 * 1024


def _sigmoid(x):
    return 1.0 / (1.0 + jnp.exp(-x))


def _silu(x):
    return x * _sigmoid(x)


def _ln(x):
    mu = jnp.mean(x, axis=-1, keepdims=True)
    xc = x - mu
    var = jnp.mean(xc * xc, axis=-1, keepdims=True)
    return xc * lax.rsqrt(var + LN_EPS)


def _rope(x, cos, sin_signed):
    lane = lax.broadcasted_iota(jnp.int32, x.shape, 1)
    first_half = (lane & 63) < 32
    rot = jnp.where(first_half, pltpu.roll(x, 96, axis=1), pltpu.roll(x, 32, axis=1))
    return x * cos + rot * sin_signed


def _swap_halves(x):
    return pltpu.roll(x, HEAD_DIM, axis=1)


def _group_heads(nat):
    low = lax.broadcasted_iota(jnp.int32, nat[0].shape, 1) < HEAD_DIM
    out = []
    for g in range(GROUP):
        a, b = nat[g // 2], nat[GROUP // 2 + g // 2]
        out.append(jnp.where(low, a, _swap_halves(b)) if g % 2 == 0
                   else jnp.where(low, _swap_halves(a), b))
    return out


def _ungroup_heads(grouped):
    low = lax.broadcasted_iota(jnp.int32, grouped[0].shape, 1) < HEAD_DIM
    nat = [None] * GROUP
    for c in range(GROUP // 2):
        even, odd = grouped[2 * c], grouped[2 * c + 1]
        nat[c] = jnp.where(low, even, _swap_halves(odd))
        nat[GROUP // 2 + c] = jnp.where(low, _swap_halves(even), odd)
    return nat


def _mod_kernel(c_ref, w_ref, b_ref, o_ref):
    c = c_ref[...]
    sc = _silu(c).astype(BF16)
    o_ref[0] = jnp.dot(sc, w_ref[0].astype(BF16), preferred_element_type=F32) + b_ref[0]


def _modulation(c_all, w_mod, b_mod):
    n = c_all.shape[0]
    tn = 768
    return pl.pallas_call(
        _mod_kernel,
        grid=(DEPTH, 3 * D_MODEL // tn),
        in_specs=[
            pl.BlockSpec((n, D_MODEL), lambda l, j: (0, 0)),
            pl.BlockSpec((1, D_MODEL, tn), lambda l, j: (l, 0, j)),
            pl.BlockSpec((1, 1, tn), lambda l, j: (l, 0, j)),
        ],
        out_specs=pl.BlockSpec((1, n, tn), lambda l, j: (l, 0, j)),
        out_shape=jax.ShapeDtypeStruct((DEPTH, n, 3 * D_MODEL), F32),
        compiler_params=pltpu.CompilerParams(
            dimension_semantics=("arbitrary", "arbitrary"), vmem_limit_bytes=VMEM_LIMIT),
    )(c_all, w_mod, b_mod.reshape(DEPTH, 1, 3 * D_MODEL))


def _prompt_kernel(l_ref, sink_ref,
                   x_ref, mod_ref, win_ref, wout_ref, convw_ref, convb_ref, cng_ref, cnb_ref,
                   wpool_ref, pscale_ref, lng_ref, lnb_ref, cos_ref, sin_ref,
                   xo_ref, convo_ref, poolo_ref, ko_ref, vo_ref,
                   acat, ashift, pcat, ps_a, ps_b, kcat0, kcat1, vcat0, vcat1, mixcat):
    del l_ref
    kcat, vcat = (kcat0, kcat1), (vcat0, vcat1)
    t = pl.program_id(1)
    last = pl.num_programs(1) - 1
    nb = TM // BLOCK

    @pl.when(t == 0)
    def _():
        acat[0:CARRY, :] = jnp.zeros((CARRY, D_CONV), F32)
        pcat[0:CARRY, :] = jnp.zeros((CARRY, D_POOL), F32)
        for kc, vc in zip(kcat, vcat):
            kc[0:BLOCK, :] = jnp.zeros((BLOCK, D_KV), BF16)
            vc[0:BLOCK, 0:D_KV] = jnp.zeros((BLOCK, D_KV), BF16)
            vc[:, D_KV:2 * D_KV] = jnp.ones((BLOCK + TM, D_KV), BF16)

    kprev = [kc[0:BLOCK, :] for kc in kcat]
    vprev = [vc[0:BLOCK, :] for vc in vcat]
    x = x_ref[0]
    shift = mod_ref[0, 0, 0:1, :]
    scale = mod_ref[0, 0, 1:2, :]
    gate = mod_ref[0, 0, 2:3, :]
    hb = (_ln(x) * (1.0 + scale) + shift).astype(BF16)

    def proj(lo, hi):
        return jnp.dot(hb, win_ref[0, :, lo:hi], preferred_element_type=F32)

    cos = cos_ref[...]
    sin = sin_ref[...]
    k = _rope(proj(O_K, O_V), cos, sin)
    v = proj(O_V, O_ZA)
    kcat[0][BLOCK:BLOCK + TM, :] = k.astype(BF16)
    kcat[1][BLOCK:BLOCK + TM, :] = _swap_halves(k).astype(BF16)
    vcat[0][BLOCK:BLOCK + TM, 0:D_KV] = v.astype(BF16)
    vcat[1][BLOCK:BLOCK + TM, 0:D_KV] = _swap_halves(v).astype(BF16)

    @pl.when(t == last)
    def _():
        ko_ref[0] = k[TM - WINDOW:TM, :]
        vo_ref[0] = v[TM - WINDOW:TM, :]

    row = lax.broadcasted_iota(jnp.int32, (BLOCK, 2 * BLOCK), 0)
    col = lax.broadcasted_iota(jnp.int32, (BLOCK, 2 * BLOCK), 1)
    rel = col - BLOCK - row
    band = (rel <= 0) & (rel > -WINDOW)
    band_first = band & (col + jnp.minimum(t, 1) * BLOCK >= BLOCK)
    lane_q = lax.broadcasted_iota(jnp.int32, (BLOCK, 2 * HEAD_DIM), 1)
    low = lane_q < HEAD_DIM

    q = proj(O_Q, O_K) * (HEAD_DIM ** -0.5)
    za = proj(O_ZA, D_IN)
    nq = D_ATTN // 128
    qr = [_rope(q[:, 128 * c:128 * (c + 1)], cos, sin) for c in range(nq)]
    same = [(c, (2 * c) // GROUP) for c in range(nq)]
    other = [(c, 1 - (2 * c) // GROUP) for c in range(nq)]
    sink = sink_ref[0]
    nt_dims = (((1,), (1,)), ((), ()))
    def attn_block(i):
        mask = band if i > 0 else band_first
        scores = []
        for copy, heads in ((0, same), (1, other)):
            lhs = jnp.concatenate(
                [jnp.where(low == (half == 0), qr[c][i * BLOCK:(i + 1) * BLOCK], 0.0)
                 for c, half in heads], axis=0).astype(BF16)
            kk = (kcat[copy][i * BLOCK:(i + 2) * BLOCK, :] if i > 0 else
                  jnp.concatenate([kprev[copy], kcat[copy][BLOCK:2 * BLOCK, :]], axis=0))
            scores.append(lax.dot_general(lhs, kk, nt_dims, preferred_element_type=F32))
        s = jnp.concatenate(scores, axis=0).reshape(N_HEADS, BLOCK, 2 * BLOCK)
        s = jnp.where(mask[None], s, MASK_VALUE)
        m = jnp.maximum(jnp.max(s, axis=-1, keepdims=True), sink)
        mb = jnp.broadcast_to(m, (N_HEADS, BLOCK, BLOCK))
        p = jnp.concatenate([jnp.exp(s[:, :, 0:BLOCK] - mb), jnp.exp(s[:, :, BLOCK:] - mb)],
                            axis=-1).astype(BF16)
        sink_p = jnp.exp(sink - mb)
        half_rows = (N_HEADS // 2) * BLOCK
        outs = []
        for copy in range(2):
            pv = jnp.dot(p[copy * (N_HEADS // 2):(copy + 1) * (N_HEADS // 2)].reshape(
                half_rows, 2 * BLOCK),
                (vcat[copy][i * BLOCK:(i + 2) * BLOCK, :] if i > 0 else
                 jnp.concatenate([vprev[copy], vcat[copy][BLOCK:2 * BLOCK, :]], axis=0)),
                preferred_element_type=F32).reshape(N_HEADS // 2, BLOCK, 2 * D_KV)
            den = pv[:, :, D_KV:] + sink_p[copy * (N_HEADS // 2):(copy + 1) * (N_HEADS // 2)]
            outs.append(pv[:, :, 0:D_KV] / den)
        for c in range(nq):
            o = jnp.where(low == (same[c][1] == 0), outs[0][c], outs[1][c])
            zg = za[i * BLOCK:(i + 1) * BLOCK, 128 * c:128 * (c + 1)]
            c0 = D_CONV + D_POOL + 128 * c
            mixcat[i * BLOCK:(i + 1) * BLOCK, c0:c0 + 128] = (o * _silu(zg)).astype(BF16)

    a = proj(O_U, O_G) * _sigmoid(proj(O_G, O_ZC))
    acat[CARRY:CARRY + TM, :] = a
    zc = proj(O_ZC, O_PV)
    for r in range(1, SUBLANES):
        ashift[r - 1] = acat[r:r + TM + CARRY - SUBLANES, :]
    rows = 64

    def conv_chunk(c):
        acc = jnp.zeros((rows, D_CONV), F32)
        for j in range(CONV_WIDTH):
            off = CARRY - CONV_BUF + j
            r, lo = off % SUBLANES, c * rows + off - off % SUBLANES
            tap = acat[lo:lo + rows, :] if r == 0 else ashift[r - 1, lo:lo + rows, :]
            acc = acc + tap * convw_ref[0, j:j + 1, :]
        conv = acc + convb_ref[0]
        ya = _silu(_ln(conv) * cng_ref[0] + cnb_ref[0])
        mixcat[c * rows:(c + 1) * rows, 0:D_CONV] = (
            ya * _silu(zc[c * rows:(c + 1) * rows])).astype(BF16)

    for i in range(nb):
        attn_block(i)
    for c in range(TM // rows):
        conv_chunk(c)

    kcat[0][0:BLOCK, :] = k[TM - BLOCK:TM].astype(BF16)
    kcat[1][0:BLOCK, :] = _swap_halves(k[TM - BLOCK:TM]).astype(BF16)
    vcat[0][0:BLOCK, 0:D_KV] = v[TM - BLOCK:TM].astype(BF16)
    vcat[1][0:BLOCK, 0:D_KV] = _swap_halves(v[TM - BLOCK:TM]).astype(BF16)

    @pl.when(t == last)
    def _():
        convo_ref[0] = acat[CARRY + TM - CONV_BUF:CARRY + TM, :]

    acat[0:CARRY, :] = acat[TM:TM + CARRY, :]

    pool_v = proj(O_PV, O_ZP)
    pcat[CARRY:CARRY + TM, :] = pool_v
    zp = proj(O_ZP, O_Q)
    n = CARRY + TM
    ps_a[8:n, :] = pcat[8:n, :] + pcat[7:n - 1, :]
    ps_b[16:n, :] = ps_a[16:n, :] + ps_a[14:n - 2, :]
    s2 = ps_a[CARRY:n, :]
    s4 = ps_b[CARRY:n, :]
    ps_a[24:n, :] = ps_b[24:n, :] + ps_b[20:n - 4, :]
    s8 = ps_a[CARRY:n, :]
    s16 = s8 + ps_a[CARRY - 8:n - 8, :]
    lane = lax.broadcasted_iota(jnp.int32, (TM, D_POOL), 1)
    sums = jnp.where(lane < 64, s2, jnp.where(lane < 128, s4, jnp.where(lane < 192, s8, s16)))
    wlen = jnp.where(lane < 64, 2.0, jnp.where(lane < 128, 4.0, jnp.where(lane < 192, 8.0, 16.0)))
    pos1 = (lax.broadcasted_iota(jnp.int32, (TM, D_POOL), 0) + (t * TM + 1)).astype(F32)
    cnt = jnp.minimum(wlen, pos1)
    pooled = (sums / cnt - pool_v).astype(BF16)
    yb = jnp.dot(pooled, wpool_ref[0], preferred_element_type=F32) * pscale_ref[0]
    mixcat[:, D_CONV:D_CONV + D_POOL] = (yb * _silu(zp)).astype(BF16)

    @pl.when(t == last)
    def _():
        poolo_ref[0] = pcat[CARRY + TM - POOL_BUF:CARRY + TM, :]

    pcat[0:CARRY, :] = pcat[TM:TM + CARRY, :]

    mix = jnp.dot(mixcat[...], wout_ref[0], preferred_element_type=F32)
    y = ALPHA * x + (1.0 + gate) * mix
    xo_ref[0] = _ln(y) * lng_ref[0] + lnb_ref[0]


def _prompt_layer(lidx, sinks_perm, x, mod_p, win, wout, conv_w, conv_b, cnorm_g, cnorm_b,
                  wpool, pool_scale, ln_g, ln_b, cos_t, sin_t):
    B, T, _ = x.shape
    nt = T // TM
    per_layer = lambda shape: pl.BlockSpec((1,) + shape, lambda b, t, l: (l[0],) + (0,) * len(shape))
    grid_spec = pltpu.PrefetchScalarGridSpec(
        num_scalar_prefetch=1,
        grid=(B, nt),
        in_specs=[
            per_layer((N_HEADS, 1, 1)),
            pl.BlockSpec((1, TM, D_MODEL), lambda b, t, l: (b, t, 0)),
            pl.BlockSpec((1, 1, 3, D_MODEL), lambda b, t, l: (l[0], b, 0, 0)),
            per_layer((D_MODEL, D_IN)),
            per_layer((D_MODEL, D_MODEL)),
            per_layer((CONV_WIDTH, D_CONV)),
            per_layer((1, D_CONV)),
            per_layer((1, D_CONV)),
            per_layer((1, D_CONV)),
            per_layer((D_POOL, D_POOL)),
            per_layer((1, D_POOL)),
            per_layer((1, D_MODEL)),
            per_layer((1, D_MODEL)),
            pl.BlockSpec((TM, 128), lambda b, t, l: (t, 0)),
            pl.BlockSpec((TM, 128), lambda b, t, l: (t, 0)),
        ],
        out_specs=[
            pl.BlockSpec((1, TM, D_MODEL), lambda b, t, l: (b, t, 0)),
            pl.BlockSpec((1, CONV_BUF, D_CONV), lambda b, t, l: (b, 0, 0)),
            pl.BlockSpec((1, POOL_BUF, D_POOL), lambda b, t, l: (b, 0, 0)),
            pl.BlockSpec((1, WINDOW, D_KV), lambda b, t, l: (b, 0, 0)),
            pl.BlockSpec((1, WINDOW, D_KV), lambda b, t, l: (b, 0, 0)),
        ],
        scratch_shapes=[
            pltpu.VMEM((CARRY + TM, D_CONV), F32),
            pltpu.VMEM((SUBLANES - 1, CARRY + TM - SUBLANES, D_CONV), F32),
            pltpu.VMEM((CARRY + TM, D_POOL), F32),
            pltpu.VMEM((CARRY + TM, D_POOL), F32),
            pltpu.VMEM((CARRY + TM, D_POOL), F32),
            pltpu.VMEM((BLOCK + TM, D_KV), BF16),
            pltpu.VMEM((BLOCK + TM, D_KV), BF16),
            pltpu.VMEM((BLOCK + TM, 2 * D_KV), BF16),
            pltpu.VMEM((BLOCK + TM, 2 * D_KV), BF16),
            pltpu.VMEM((TM, D_MODEL), BF16),
        ],
    )
    out_shape = [
        jax.ShapeDtypeStruct((B, T, D_MODEL), F32),
        jax.ShapeDtypeStruct((B, CONV_BUF, D_CONV), F32),
        jax.ShapeDtypeStruct((B, POOL_BUF, D_POOL), F32),
        jax.ShapeDtypeStruct((B, WINDOW, D_KV), F32),
        jax.ShapeDtypeStruct((B, WINDOW, D_KV), F32),
    ]
    return pl.pallas_call(
        _prompt_kernel,
        grid_spec=grid_spec,
        out_shape=out_shape,
        compiler_params=pltpu.CompilerParams(
            dimension_semantics=("arbitrary", "arbitrary"), vmem_limit_bytes=VMEM_LIMIT),
    )(lidx, sinks_perm, x, mod_p, win, wout, conv_w, conv_b, cnorm_g, cnorm_b,
      wpool, pool_scale, ln_g, ln_b, cos_t, sin_t)


def _decode_kernel(l_ref,
                   x_ref, mod_ref, win_ref, wout_ref, convw_ref, convb_ref, cng_ref, cnb_ref,
                   wpool_ref, pscale_ref, lng_ref, lnb_ref, cos_ref, sin_ref, sinkrow_ref,
                   cconv_ref, cpool_ref, ck_ref, cv_ref, *rest):
    xo_ref, convo_ref, poolo_ref, ko_ref, vo_ref, qs, ocat, kn_s, vn_s, kbt, vbt, knt, vnt = rest[-13:]
    gb = DEC_GROUP
    nt = x_ref.shape[0]
    x3 = x_ref[...]
    shift = mod_ref[0, :, 0:D_MODEL]
    scale = mod_ref[0, :, D_MODEL:2 * D_MODEL]
    gate = mod_ref[0, :, 2 * D_MODEL:3 * D_MODEL]
    h3 = _ln(x3) * (1.0 + scale)[None] + shift[None]
    hb = h3.reshape(nt * gb, D_MODEL).astype(BF16)

    def proj(lo, hi):
        return jnp.dot(hb, win_ref[0, :, lo:hi], preferred_element_type=F32)

    a = proj(O_U, O_G) * _sigmoid(proj(O_G, O_ZC))
    convo_ref[0, 0:CONV_BUF - nt] = cconv_ref[0, nt:CONV_BUF]
    convo_ref[0, CONV_BUF - nt:CONV_BUF] = a.reshape(nt, gb, D_CONV)
    zc = proj(O_ZC, O_PV)
    ya_rows = []
    for tt in range(nt):
        acc = jnp.zeros((gb, D_CONV), F32)
        for j in range(CONV_WIDTH):
            i = tt + j
            xi = (cconv_ref[0, i] if i < CONV_BUF
                  else a[(i - CONV_BUF) * gb:(i - CONV_BUF + 1) * gb])
            acc = acc + xi * convw_ref[0, j:j + 1, :]
        conv = acc + convb_ref[0]
        ya_rows.append(_silu(_ln(conv) * cng_ref[0] + cnb_ref[0]))
    ya = jnp.concatenate(ya_rows, axis=0) * _silu(zc)

    pv = proj(O_PV, O_ZP)
    poolo_ref[0, 0:POOL_BUF - nt] = cpool_ref[0, nt:POOL_BUF]
    poolo_ref[0, POOL_BUF - nt:POOL_BUF] = pv.reshape(nt, gb, D_POOL)
    lane = lax.broadcasted_iota(jnp.int32, (gb, D_POOL), 1)
    wlen = jnp.where(lane < 64, 2, jnp.where(lane < 128, 4, jnp.where(lane < 192, 8, 16)))
    pooled_rows = []
    for tt in range(nt):
        acc = jnp.zeros((gb, D_POOL), F32)
        for d in range(max(POOL_WINDOWS)):
            i = POOL_BUF + tt - d
            xi = (cpool_ref[0, i] if i < POOL_BUF
                  else pv[(i - POOL_BUF) * gb:(i - POOL_BUF + 1) * gb])
            acc = acc + (xi if d < min(POOL_WINDOWS) else jnp.where(wlen > d, xi, 0.0))
        cnt = jnp.minimum(wlen, PAST_LEN + tt + 1).astype(F32)
        pooled_rows.append(acc / cnt - pv[tt * gb:(tt + 1) * gb])
    pooled = jnp.concatenate(pooled_rows, axis=0).astype(BF16)
    yb = jnp.dot(pooled, wpool_ref[0], preferred_element_type=F32) * pscale_ref[0]
    yb = yb * _silu(proj(O_ZP, O_Q))

    cos3 = cos_ref[...]
    sin3 = sin_ref[...]
    cos = jnp.broadcast_to(cos3, (nt, gb, 128)).reshape(nt * gb, 128)
    sin = jnp.broadcast_to(sin3, (nt, gb, 128)).reshape(nt * gb, 128)
    k = _rope(proj(O_K, O_V), cos, sin)
    v = proj(O_V, O_ZA)
    kn_s[...] = k
    vn_s[...] = v
    q = proj(O_Q, O_K) * (HEAD_DIM ** -0.5)
    for g, qg in enumerate(_group_heads([q[:, 128 * c:128 * (c + 1)] for c in range(GROUP)])):
        qs[:, 128 * g:128 * (g + 1)] = _rope(qg, cos, sin)

    for new_rows, by_batch, new_t in ((k, kbt, knt), (v, vbt, vnt)):
        for tt in range(nt):
            by_batch[pl.ds(tt, gb, stride=nt), :] = new_rows[tt * gb:(tt + 1) * gb]
        new_t[...] = by_batch[...].T
    per_blk = 128 // nt
    lane_kv = lax.broadcasted_iota(jnp.int32, (D_KV, WINDOW), 1)

    for blk in range(gb // per_blk):
        def roll_cache(j, carry, blk=blk):
            b = blk * per_blk + j
            shift = (WINDOW - nt - nt * j) & (WINDOW - 1)
            for new_t, cache_ref, out_ref in ((knt, ck_ref, ko_ref), (vnt, cv_ref, vo_ref)):
                old = pltpu.roll(cache_ref[0, b].reshape(D_KV, WINDOW), WINDOW - nt, axis=1)
                new = pltpu.roll(new_t[:, 128 * blk:128 * (blk + 1)], shift, axis=1)
                out_ref[0, b] = jnp.where(lane_kv >= WINDOW - nt, new, old).reshape(
                    N_KV_HEADS, HEAD_DIM, WINDOW)
            return carry
        lax.fori_loop(0, per_blk, roll_cache, 0, unroll=4)

    nrow = nt * GROUP * N_KV_HEADS * SUB
    r1 = lax.broadcasted_iota(jnp.int32, (nrow, SUB * WINDOW), 0)
    c1 = lax.broadcasted_iota(jnp.int32, (nrow, SUB * WINDOW), 1)
    mask1 = ((c1 >> 7) == (r1 & (SUB - 1))) & ((c1 & (WINDOW - 1)) > (r1 >> 6))
    r2 = lax.broadcasted_iota(jnp.int32, (nrow, nt * SUB), 0)
    c2 = lax.broadcasted_iota(jnp.int32, (nrow, nt * SUB), 1)
    mask2 = ((c2 & (SUB - 1)) == (r2 & (SUB - 1))) & ((c2 >> 3) <= (r2 >> 6))
    lane_q = lax.broadcasted_iota(jnp.int32, (SUB, 128), 1)
    low = lane_q < HEAD_DIM
    sink = sinkrow_ref[0]

    def sub_block(sb, carry):
        b0 = pl.multiple_of(sb * SUB, SUB)
        pieces = []
        for tt in range(nt):
            for g in range(GROUP):
                qp = qs[pl.ds(tt * gb + b0, SUB), 128 * g:128 * (g + 1)]
                pieces.append(jnp.where(low, qp, 0.0))
                pieces.append(jnp.where(low, 0.0, qp))
        lhs = jnp.concatenate(pieces, axis=0).astype(BF16)
        kblk = ck_ref[0, pl.ds(b0, SUB)]
        vblk = cv_ref[0, pl.ds(b0, SUB)]
        kc = jnp.concatenate(
            [kblk[bb].reshape(D_KV, WINDOW) for bb in range(SUB)], axis=1).astype(BF16)
        vc = jnp.concatenate(
            [vblk[bb].reshape(D_KV, WINDOW) for bb in range(SUB)], axis=1).astype(BF16)
        kn = jnp.concatenate(
            [kn_s[pl.ds(tt * gb + b0, SUB), :] for tt in range(nt)], axis=0).astype(BF16)
        vn = jnp.concatenate(
            [vn_s[pl.ds(tt * gb + b0, SUB), :] for tt in range(nt)], axis=0).astype(BF16)
        nt_dims = (((1,), (1,)), ((), ()))
        s1 = jnp.where(mask1, jnp.dot(lhs, kc, preferred_element_type=F32), MASK_VALUE)
        s2 = jnp.where(mask2, lax.dot_general(lhs, kn, nt_dims, preferred_element_type=F32),
                       MASK_VALUE)
        m = jnp.maximum(jnp.maximum(jnp.max(s1, axis=-1, keepdims=True),
                                    jnp.max(s2, axis=-1, keepdims=True)), sink)
        p1 = jnp.exp(s1 - m)
        p2 = jnp.exp(s2 - m)
        denom = (jnp.sum(p1, axis=-1, keepdims=True) + jnp.sum(p2, axis=-1, keepdims=True)
                 + jnp.exp(sink - m))
        o_all = (lax.dot_general(p1.astype(BF16), vc, nt_dims, preferred_element_type=F32)
                 + jnp.dot(p2.astype(BF16), vn, preferred_element_type=F32)) * (1.0 / denom)
        for tt in range(nt):
            for g in range(GROUP):
                r0 = ((tt * GROUP + g) * N_KV_HEADS) * SUB
                o = jnp.where(low, o_all[r0:r0 + SUB], o_all[r0 + SUB:r0 + 2 * SUB])
                ocat[pl.ds(tt * gb + b0, SUB), 128 * g:128 * (g + 1)] = o
        return carry

    lax.fori_loop(0, gb // SUB, sub_block, 0, unroll=2)
    o_nat = _ungroup_heads([ocat[:, 128 * g:128 * (g + 1)] for g in range(GROUP)])
    yc = jnp.concatenate(o_nat, axis=-1) * _silu(proj(O_ZA, D_IN))

    mixcat = jnp.concatenate([ya, yb, yc], axis=-1).astype(BF16)
    mix = jnp.dot(mixcat, wout_ref[0], preferred_element_type=F32).reshape(nt, gb, D_MODEL)
    y = ALPHA * x3 + (1.0 + gate)[None] * mix
    xo_ref[...] = _ln(y) * lng_ref[0][None] + lnb_ref[0][None]


def _decode_layer(lidx, x, mod_s, win, wout, conv_w, conv_b, cnorm_g, cnorm_b, wpool, pool_scale,
                  ln_g, ln_b, cos_s, sin_s, sink_rows, cache_conv, cache_pool, cache_k, cache_v,
                  rolled):
    nt, nbatch, _ = x.shape
    gb = DEC_GROUP
    carried = list(rolled)
    per_layer = lambda shape: pl.BlockSpec((1,) + shape, lambda g, l: (l[0],) + (0,) * len(shape))
    tok = lambda c: pl.BlockSpec((nt, gb, c), lambda g, l: (0, g, 0))
    time_major = lambda r, c: pl.BlockSpec((1, r, gb, c), lambda g, l: (l[0], 0, g, 0))
    kv_blk = pl.BlockSpec((1, gb, N_KV_HEADS, HEAD_DIM, WINDOW), lambda g, l: (l[0], g, 0, 0, 0))
    nrow = nt * GROUP * N_KV_HEADS * SUB
    grid_spec = pltpu.PrefetchScalarGridSpec(
        num_scalar_prefetch=1,
        grid=(nbatch // gb,),
        in_specs=[
            tok(D_MODEL),
            pl.BlockSpec((1, gb, 3 * D_MODEL), lambda g, l: (l[0], g, 0)),
            per_layer((D_MODEL, D_IN)),
            per_layer((D_MODEL, D_MODEL)),
            per_layer((CONV_WIDTH, D_CONV)),
            per_layer((1, D_CONV)),
            per_layer((1, D_CONV)),
            per_layer((1, D_CONV)),
            per_layer((D_POOL, D_POOL)),
            per_layer((1, D_POOL)),
            per_layer((1, D_MODEL)),
            per_layer((1, D_MODEL)),
            pl.BlockSpec((nt, 1, 128), lambda g, l: (0, 0, 0)),
            pl.BlockSpec((nt, 1, 128), lambda g, l: (0, 0, 0)),
            per_layer((nrow, 1)),
            time_major(CONV_BUF, D_CONV),
            time_major(POOL_BUF, D_POOL),
            kv_blk,
            kv_blk,
        ] + [pl.BlockSpec(memory_space=pl.ANY)] * len(carried),
        out_specs=[tok(D_MODEL), time_major(CONV_BUF, D_CONV), time_major(POOL_BUF, D_POOL),
                   kv_blk, kv_blk],
        scratch_shapes=[
            pltpu.VMEM((nt * gb, D_ATTN), F32),
            pltpu.VMEM((nt * gb, D_ATTN), F32),
            pltpu.VMEM((nt * gb, D_KV), F32),
            pltpu.VMEM((nt * gb, D_KV), F32),
            pltpu.VMEM((nt * gb, D_KV), F32),
            pltpu.VMEM((nt * gb, D_KV), F32),
            pltpu.VMEM((D_KV, nt * gb), F32),
            pltpu.VMEM((D_KV, nt * gb), F32),
        ],
    )
    out_shape = [jax.ShapeDtypeStruct((nt, nbatch, D_MODEL), F32)] + [
        jax.ShapeDtypeStruct(c.shape, F32) for c in (cache_conv, cache_pool, cache_k, cache_v)]
    operands = (lidx, x, mod_s, win, wout, conv_w, conv_b, cnorm_g, cnorm_b, wpool, pool_scale,
                ln_g, ln_b, cos_s, sin_s, sink_rows, cache_conv, cache_pool, cache_k, cache_v)
    n_in = len(operands)
    return pl.pallas_call(
        _decode_kernel,
        grid_spec=grid_spec,
        out_shape=out_shape,
        input_output_aliases={n_in + i: 1 + i for i in range(len(carried))},
        compiler_params=pltpu.CompilerParams(
            dimension_semantics=("arbitrary",), vmem_limit_bytes=VMEM_LIMIT),
    )(*operands, *carried)


def _rope_tables(pos):
    half = HEAD_DIM // 2
    inv_freq = ROPE_THETA ** (-jnp.arange(half, dtype=F32) * (2.0 / HEAD_DIM))
    ang = pos.astype(F32)[:, None] * inv_freq[None, :]
    cos = jnp.cos(ang)
    sin = jnp.sin(ang)
    cos_t = jnp.concatenate([cos, cos, cos, cos], axis=-1)
    sin_t = jnp.concatenate([-sin, sin, -sin, sin], axis=-1)
    return cos_t, sin_t


def kernel(x_prompt, x_sample, cache_conv, cache_pool, cache_k, cache_v, c_prompt, c_sample,
           w_in, w_out, conv_w, conv_b, cnorm_g, cnorm_b, pool_w, pool_scale, sinks,
           w_mod, b_mod, ln_g, ln_b):
    B, T, _ = x_prompt.shape
    nbatch, nt, _ = x_sample.shape

    win = w_in.astype(BF16)
    wout = w_out.astype(BF16)
    wpool = jnp.zeros((DEPTH, D_POOL, D_POOL), F32)
    for gi in range(len(POOL_WINDOWS)):
        sl = slice(gi * POOL_GROUP_DIM, (gi + 1) * POOL_GROUP_DIM)
        wpool = wpool.at[:, sl, sl].set(pool_w[:, gi])
    wpool = wpool.astype(BF16)
    sinks_gh = sinks.reshape(DEPTH, N_KV_HEADS, GROUP).transpose(0, 2, 1)
    nq = D_ATTN // 128
    order = ([2 * c + (2 * c) // GROUP for c in range(nq)]
             + [2 * c + 1 - (2 * c) // GROUP for c in range(nq)])
    sinks_flat = jnp.stack([sinks[:, hd] for hd in order], axis=1).reshape(DEPTH, N_HEADS, 1, 1)
    sink_rows = jnp.broadcast_to(
        sinks_gh[:, None, :, :, None], (DEPTH, nt, GROUP, N_KV_HEADS, SUB)
    ).reshape(DEPTH, nt * GROUP * N_KV_HEADS * SUB, 1)

    r3 = lambda p: p.reshape(DEPTH, 1, -1)
    conv_b3, cng3, cnb3, pscale3, lng3, lnb3 = map(r3, (conv_b, cnorm_g, cnorm_b, pool_scale, ln_g, ln_b))

    cos_p, sin_p = _rope_tables(jnp.arange(T, dtype=jnp.int32))
    cos_s, sin_s = _rope_tables(PAST_LEN + jnp.arange(nt, dtype=jnp.int32))
    cos_s = cos_s.reshape(nt, 1, 128)
    sin_s = sin_s.reshape(nt, 1, 128)

    mod = _modulation(jnp.concatenate([c_sample, c_prompt], axis=0), w_mod, b_mod)
    mod_p = mod[:, nbatch:].reshape(DEPTH, B, 3, D_MODEL)

    cconv_t = cache_conv.transpose(0, 2, 1, 3)
    cpool_t = cache_pool.transpose(0, 2, 1, 3)
    ck_t = cache_k.transpose(0, 1, 3, 4, 2)
    cv_t = cache_v.transpose(0, 1, 3, 4, 2)

    xp = x_prompt
    xs = x_sample.transpose(1, 0, 2)
    conv_p, pool_p, k_p, v_p = ([] for _ in range(4))
    rolled = ()
    for l in range(DEPTH):
        lidx = jnp.full((1,), l, jnp.int32)
        xp, cp, pp, kp, vp = _prompt_layer(
            lidx, sinks_flat, xp, mod_p, win, wout, conv_w, conv_b3, cng3, cnb3, wpool, pscale3,
            lng3, lnb3, cos_p, sin_p)
        xs, *rolled = _decode_layer(
            lidx, xs, mod, win, wout, conv_w, conv_b3, cng3, cnb3, wpool, pscale3,
            lng3, lnb3, cos_s, sin_s, sink_rows, cconv_t, cpool_t, ck_t, cv_t, rolled)
        conv_p.append(cp); pool_p.append(pp); k_p.append(kp); v_p.append(vp)

    kv5 = lambda z: z.reshape(z.shape[:-1] + (N_KV_HEADS, HEAD_DIM))
    conv_s, pool_s, k_s, v_s = rolled
    return (xp, xs.transpose(1, 0, 2),
            jnp.stack(conv_p), jnp.stack(pool_p), kv5(jnp.stack(k_p)), kv5(jnp.stack(v_p)),
            conv_s.transpose(0, 2, 1, 3), pool_s.transpose(0, 2, 1, 3),
            k_s.transpose(0, 1, 4, 2, 3), v_s.transpose(0, 1, 4, 2, 3))
```

```python
import functools

import jax
import jax.numpy as jnp
from jax import lax
from jax.experimental import pallas as pl
from jax.experimental.pallas import tpu as pltpu

F32 = jnp.float32
BF16 = jnp.bfloat16

D_MODEL = 1024
DEPTH = 4
D_CONV = 256
D_POOL = 256
D_ATTN = 512
HEAD_DIM = 64
N_HEADS = 8
N_KV_HEADS = 2
GROUP = 4
D_KV = 128
WINDOW = 128
BLOCK = 128
CONV_WIDTH = 31
CONV_BUF = 30
POOL_WINDOWS = (2, 4, 8, 16)
POOL_GROUP_DIM = 64
POOL_BUF = 15
ROPE_THETA = 10000.0
LN_EPS = 1e-5
ALPHA = (2.0 * DEPTH) ** 0.25
MASK_VALUE = -1e30
PAST_LEN = 8192

O_U, O_G, O_ZC, O_PV, O_ZP, O_Q, O_K, O_V, O_ZA, D_IN = (
    0, 256, 512, 768, 1024, 1280, 1792, 1920, 2048, 2560)

SUBLANES = 8
CARRY = 32
TM = 512
DEC_GROUP = 32
SUB = 8
VMEM_LIMIT = 48 * 1024 * 1024


def _sigmoid(x):
    return 1.0 / (1.0 + jnp.exp(-x))


def _silu(x):
    return x * _sigmoid(x)


def _ln(x):
    mu = jnp.mean(x, axis=-1, keepdims=True)
    xc = x - mu
    var = jnp.mean(xc * xc, axis=-1, keepdims=True)
    return xc * lax.rsqrt(var + LN_EPS)


def _rope(x, cos, sin_signed):
    lane = lax.broadcasted_iota(jnp.int32, x.shape, 1)
    first_half = (lane & 63) < 32
    rot = jnp.where(first_half, pltpu.roll(x, 96, axis=1), pltpu.roll(x, 32, axis=1))
    return x * cos + rot * sin_signed


def _swap_halves(x):
    return pltpu.roll(x, HEAD_DIM, axis=1)


def _group_heads(nat):
    low = lax.broadcasted_iota(jnp.int32, nat[0].shape, 1) < HEAD_DIM
    out = []
    for g in range(GROUP):
        a, b = nat[g // 2], nat[GROUP // 2 + g // 2]
        out.append(jnp.where(low, a, _swap_halves(b)) if g % 2 == 0
                   else jnp.where(low, _swap_halves(a), b))
    return out


def _ungroup_heads(grouped):
    low = lax.broadcasted_iota(jnp.int32, grouped[0].shape, 1) < HEAD_DIM
    nat = [None] * GROUP
    for c in range(GROUP // 2):
        even, odd = grouped[2 * c], grouped[2 * c + 1]
        nat[c] = jnp.where(low, even, _swap_halves(odd))
        nat[GROUP // 2 + c] = jnp.where(low, _swap_halves(even), odd)
    return nat


def _mod_kernel(c_ref, w_ref, b_ref, o_ref):
    c = c_ref[...]
    sc = _silu(c).astype(BF16)
    o_ref[0] = jnp.dot(sc, w_ref[0].astype(BF16), preferred_element_type=F32) + b_ref[0]


def _modulation(c_all, w_mod, b_mod):
    n = c_all.shape[0]
    tn = 768
    return pl.pallas_call(
        _mod_kernel,
        grid=(DEPTH, 3 * D_MODEL // tn),
        in_specs=[
            pl.BlockSpec((n, D_MODEL), lambda l, j: (0, 0)),
            pl.BlockSpec((1, D_MODEL, tn), lambda l, j: (l, 0, j)),
            pl.BlockSpec((1, 1, tn), lambda l, j: (l, 0, j)),
        ],
        out_specs=pl.BlockSpec((1, n, tn), lambda l, j: (l, 0, j)),
        out_shape=jax.ShapeDtypeStruct((DEPTH, n, 3 * D_MODEL), F32),
        compiler_params=pltpu.CompilerParams(
            dimension_semantics=("arbitrary", "arbitrary"), vmem_limit_bytes=VMEM_LIMIT),
    )(c_all, w_mod, b_mod.reshape(DEPTH, 1, 3 * D_MODEL))


def _prompt_kernel(l_ref, sink_ref,
                   x_ref, mod_ref, win_ref, wout_ref, convw_ref, convb_ref, cng_ref, cnb_ref,
                   wpool_ref, pscale_ref, lng_ref, lnb_ref, cos_ref, sin_ref,
                   xo_ref, convo_ref, poolo_ref, ko_ref, vo_ref,
                   acat, ashift, pcat, ps_a, ps_b, kcat0, kcat1, vcat0, vcat1, mixcat):
    del l_ref
    kcat, vcat = (kcat0, kcat1), (vcat0, vcat1)
    t = pl.program_id(1)
    last = pl.num_programs(1) - 1
    nb = TM // BLOCK

    @pl.when(t == 0)
    def _():
        acat[0:CARRY, :] = jnp.zeros((CARRY, D_CONV), F32)
        pcat[0:CARRY, :] = jnp.zeros((CARRY, D_POOL), F32)
        for kc, vc in zip(kcat, vcat):
            kc[0:BLOCK, :] = jnp.zeros((BLOCK, D_KV), BF16)
            vc[0:BLOCK, 0:D_KV] = jnp.zeros((BLOCK, D_KV), BF16)
            vc[:, D_KV:2 * D_KV] = jnp.ones((BLOCK + TM, D_KV), BF16)

    kprev = [kc[0:BLOCK, :] for kc in kcat]
    vprev = [vc[0:BLOCK, :] for vc in vcat]
    x = x_ref[0]
    shift = mod_ref[0, 0, 0:1, :]
    scale = mod_ref[0, 0, 1:2, :]
    gate = mod_ref[0, 0, 2:3, :]
    hb = (_ln(x) * (1.0 + scale) + shift).astype(BF16)

    def proj(lo, hi):
        return jnp.dot(hb, win_ref[0, :, lo:hi], preferred_element_type=F32)

    cos = cos_ref[...]
    sin = sin_ref[...]
    k = _rope(proj(O_K, O_V), cos, sin)
    v = proj(O_V, O_ZA)
    kcat[0][BLOCK:BLOCK + TM, :] = k.astype(BF16)
    kcat[1][BLOCK:BLOCK + TM, :] = _swap_halves(k).astype(BF16)
    vcat[0][BLOCK:BLOCK + TM, 0:D_KV] = v.astype(BF16)
    vcat[1][BLOCK:BLOCK + TM, 0:D_KV] = _swap_halves(v).astype(BF16)

    @pl.when(t == last)
    def _():
        ko_ref[0] = k[TM - WINDOW:TM, :]
        vo_ref[0] = v[TM - WINDOW:TM, :]

    row = lax.broadcasted_iota(jnp.int32, (BLOCK, 2 * BLOCK), 0)
    col = lax.broadcasted_iota(jnp.int32, (BLOCK, 2 * BLOCK), 1)
    rel = col - BLOCK - row
    band = (rel <= 0) & (rel > -WINDOW)
    band_first = band & (col + jnp.minimum(t, 1) * BLOCK >= BLOCK)
    lane_q = lax.broadcasted_iota(jnp.int32, (BLOCK, 2 * HEAD_DIM), 1)
    low = lane_q < HEAD_DIM

    q = proj(O_Q, O_K) * (HEAD_DIM ** -0.5)
    za = proj(O_ZA, D_IN)
    nq = D_ATTN // 128
    qr = [_rope(q[:, 128 * c:128 * (c + 1)], cos, sin) for c in range(nq)]
    same = [(c, (2 * c) // GROUP) for c in range(nq)]
    other = [(c, 1 - (2 * c) // GROUP) for c in range(nq)]
    sink_fill = sink_ref[0]
    vrow = lax.broadcasted_iota(jnp.int32, (2 * BLOCK, 2 * D_KV), 0)
    vcol = lax.broadcasted_iota(jnp.int32, (2 * BLOCK, 2 * D_KV), 1)
    sink_row = (vrow == 0) & (vcol < D_KV)
    nt_dims = (((1,), (1,)), ((), ()))
    def attn_block(i):
        mask = band if i > 0 else band_first
        scores = []
        for copy, heads in ((0, same), (1, other)):
            lhs = jnp.concatenate(
                [jnp.where(low == (half == 0), qr[c][i * BLOCK:(i + 1) * BLOCK], 0.0)
                 for c, half in heads], axis=0).astype(BF16)
            kk = (kcat[copy][i * BLOCK:(i + 2) * BLOCK, :] if i > 0 else
                  jnp.concatenate([kprev[copy], kcat[copy][BLOCK:2 * BLOCK, :]], axis=0))
            scores.append(lax.dot_general(lhs, kk, nt_dims, preferred_element_type=F32))
        s = jnp.concatenate(scores, axis=0).reshape(N_HEADS, BLOCK, 2 * BLOCK)
        s = jnp.where(mask[None], s, sink_fill)
        m = jnp.max(s, axis=-1, keepdims=True)
        p = jnp.exp(s - m).astype(BF16)
        half_rows = (N_HEADS // 2) * BLOCK
        outs = []
        for copy in range(2):
            vv = (vcat[copy][i * BLOCK:(i + 2) * BLOCK, :] if i > 0 else
                  jnp.concatenate([vprev[copy], vcat[copy][BLOCK:2 * BLOCK, :]], axis=0))
            vv = jnp.where(sink_row, jnp.zeros((), BF16), vv)
            pv = jnp.dot(p[copy * (N_HEADS // 2):(copy + 1) * (N_HEADS // 2)].reshape(
                half_rows, 2 * BLOCK), vv,
                preferred_element_type=F32).reshape(N_HEADS // 2, BLOCK, 2 * D_KV)
            outs.append(pv[:, :, 0:D_KV] / pv[:, :, D_KV:])
        for c in range(nq):
            o = jnp.where(low == (same[c][1] == 0), outs[0][c], outs[1][c])
            zg = za[i * BLOCK:(i + 1) * BLOCK, 128 * c:128 * (c + 1)]
            c0 = D_CONV + D_POOL + 128 * c
            mixcat[i * BLOCK:(i + 1) * BLOCK, c0:c0 + 128] = (o * _silu(zg)).astype(BF16)

    a = proj(O_U, O_G) * _sigmoid(proj(O_G, O_ZC))
    acat[CARRY:CARRY + TM, :] = a
    zc = proj(O_ZC, O_PV)
    for r in range(1, SUBLANES):
        ashift[r - 1] = acat[r:r + TM + CARRY - SUBLANES, :]
    rows = 64

    def conv_chunk(c):
        acc = jnp.zeros((rows, D_CONV), F32)
        for j in range(CONV_WIDTH):
            off = CARRY - CONV_BUF + j
            r, lo = off % SUBLANES, c * rows + off - off % SUBLANES
            tap = acat[lo:lo + rows, :] if r == 0 else ashift[r - 1, lo:lo + rows, :]
            acc = acc + tap * convw_ref[0, j:j + 1, :]
        conv = acc + convb_ref[0]
        ya = _silu(_ln(conv) * cng_ref[0] + cnb_ref[0])
        mixcat[c * rows:(c + 1) * rows, 0:D_CONV] = (
            ya * _silu(zc[c * rows:(c + 1) * rows])).astype(BF16)

    for i in range(nb):
        attn_block(i)
    for c in range(TM // rows):
        conv_chunk(c)

    kcat[0][0:BLOCK, :] = k[TM - BLOCK:TM].astype(BF16)
    kcat[1][0:BLOCK, :] = _swap_halves(k[TM - BLOCK:TM]).astype(BF16)
    vcat[0][0:BLOCK, 0:D_KV] = v[TM - BLOCK:TM].astype(BF16)
    vcat[1][0:BLOCK, 0:D_KV] = _swap_halves(v[TM - BLOCK:TM]).astype(BF16)

    @pl.when(t == last)
    def _():
        convo_ref[0] = acat[CARRY + TM - CONV_BUF:CARRY + TM, :]

    acat[0:CARRY, :] = acat[TM:TM + CARRY, :]

    pool_v = proj(O_PV, O_ZP)
    pcat[CARRY:CARRY + TM, :] = pool_v
    zp = proj(O_ZP, O_Q)
    n = CARRY + TM
    ps_a[8:n, :] = pcat[8:n, :] + pcat[7:n - 1, :]
    ps_b[16:n, :] = ps_a[16:n, :] + ps_a[14:n - 2, :]
    s2 = ps_a[CARRY:n, :]
    s4 = ps_b[CARRY:n, :]
    ps_a[24:n, :] = ps_b[24:n, :] + ps_b[20:n - 4, :]
    s8 = ps_a[CARRY:n, :]
    s16 = s8 + ps_a[CARRY - 8:n - 8, :]
    lane = lax.broadcasted_iota(jnp.int32, (TM, D_POOL), 1)
    sums = jnp.where(lane < 64, s2, jnp.where(lane < 128, s4, jnp.where(lane < 192, s8, s16)))
    wlen = jnp.where(lane < 64, 2.0, jnp.where(lane < 128, 4.0, jnp.where(lane < 192, 8.0, 16.0)))
    pos1 = (lax.broadcasted_iota(jnp.int32, (TM, D_POOL), 0) + (t * TM + 1)).astype(F32)
    cnt = jnp.minimum(wlen, pos1)
    pooled = (sums / cnt - pool_v).astype(BF16)
    yb = jnp.dot(pooled, wpool_ref[0], preferred_element_type=F32) * pscale_ref[0]
    mixcat[:, D_CONV:D_CONV + D_POOL] = (yb * _silu(zp)).astype(BF16)

    @pl.when(t == last)
    def _():
        poolo_ref[0] = pcat[CARRY + TM - POOL_BUF:CARRY + TM, :]

    pcat[0:CARRY, :] = pcat[TM:TM + CARRY, :]

    mix = jnp.dot(mixcat[...], wout_ref[0], preferred_element_type=F32)
    y = ALPHA * x + (1.0 + gate) * mix
    xo_ref[0] = _ln(y) * lng_ref[0] + lnb_ref[0]


def _prompt_layer(lidx, sinks_perm, x, mod_p, win, wout, conv_w, conv_b, cnorm_g, cnorm_b,
                  wpool, pool_scale, ln_g, ln_b, cos_t, sin_t):
    B, T, _ = x.shape
    nt = T // TM
    per_layer = lambda shape: pl.BlockSpec((1,) + shape, lambda b, t, l: (l[0],) + (0,) * len(shape))
    grid_spec = pltpu.PrefetchScalarGridSpec(
        num_scalar_prefetch=1,
        grid=(B, nt),
        in_specs=[
            per_layer((N_HEADS, 1, 2 * BLOCK)),
            pl.BlockSpec((1, TM, D_MODEL), lambda b, t, l: (b, t, 0)),
            pl.BlockSpec((1, 1, 3, D_MODEL), lambda b, t, l: (l[0], b, 0, 0)),
            per_layer((D_MODEL, D_IN)),
            per_layer((D_MODEL, D_MODEL)),
            per_layer((CONV_WIDTH, D_CONV)),
            per_layer((1, D_CONV)),
            per_layer((1, D_CONV)),
            per_layer((1, D_CONV)),
            per_layer((D_POOL, D_POOL)),
            per_layer((1, D_POOL)),
            per_layer((1, D_MODEL)),
            per_layer((1, D_MODEL)),
            pl.BlockSpec((TM, 128), lambda b, t, l: (t, 0)),
            pl.BlockSpec((TM, 128), lambda b, t, l: (t, 0)),
        ],
        out_specs=[
            pl.BlockSpec((1, TM, D_MODEL), lambda b, t, l: (b, t, 0)),
            pl.BlockSpec((1, CONV_BUF, D_CONV), lambda b, t, l: (b, 0, 0)),
            pl.BlockSpec((1, POOL_BUF, D_POOL), lambda b, t, l: (b, 0, 0)),
            pl.BlockSpec((1, WINDOW, D_KV), lambda b, t, l: (b, 0, 0)),
            pl.BlockSpec((1, WINDOW, D_KV), lambda b, t, l: (b, 0, 0)),
        ],
        scratch_shapes=[
            pltpu.VMEM((CARRY + TM, D_CONV), F32),
            pltpu.VMEM((SUBLANES - 1, CARRY + TM - SUBLANES, D_CONV), F32),
            pltpu.VMEM((CARRY + TM, D_POOL), F32),
            pltpu.VMEM((CARRY + TM, D_POOL), F32),
            pltpu.VMEM((CARRY + TM, D_POOL), F32),
            pltpu.VMEM((BLOCK + TM, D_KV), BF16),
            pltpu.VMEM((BLOCK + TM, D_KV), BF16),
            pltpu.VMEM((BLOCK + TM, 2 * D_KV), BF16),
            pltpu.VMEM((BLOCK + TM, 2 * D_KV), BF16),
            pltpu.VMEM((TM, D_MODEL), BF16),
        ],
    )
    out_shape = [
        jax.ShapeDtypeStruct((B, T, D_MODEL), F32),
        jax.ShapeDtypeStruct((B, CONV_BUF, D_CONV), F32),
        jax.ShapeDtypeStruct((B, POOL_BUF, D_POOL), F32),
        jax.ShapeDtypeStruct((B, WINDOW, D_KV), F32),
        jax.ShapeDtypeStruct((B, WINDOW, D_KV), F32),
    ]
    return pl.pallas_call(
        _prompt_kernel,
        grid_spec=grid_spec,
        out_shape=out_shape,
        compiler_params=pltpu.CompilerParams(
            dimension_semantics=("arbitrary", "arbitrary"), vmem_limit_bytes=VMEM_LIMIT),
    )(lidx, sinks_perm, x, mod_p, win, wout, conv_w, conv_b, cnorm_g, cnorm_b,
      wpool, pool_scale, ln_g, ln_b, cos_t, sin_t)


def _decode_kernel(l_ref,
                   x_ref, mod_ref, win_ref, wout_ref, convw_ref, convb_ref, cng_ref, cnb_ref,
                   wpool_ref, pscale_ref, lng_ref, lnb_ref, cos_ref, sin_ref, sinkrow_ref,
                   cconv_ref, cpool_ref, ck_ref, cv_ref, *rest):
    xo_ref, convo_ref, poolo_ref, ko_ref, vo_ref, qs, ocat, kn_s, vn_s, kbt, vbt, knt, vnt = rest[-13:]
    gb = DEC_GROUP
    nt = x_ref.shape[0]
    x3 = x_ref[...]
    shift = mod_ref[0, :, 0:D_MODEL]
    scale = mod_ref[0, :, D_MODEL:2 * D_MODEL]
    gate = mod_ref[0, :, 2 * D_MODEL:3 * D_MODEL]
    h3 = _ln(x3) * (1.0 + scale)[None] + shift[None]
    hb = h3.reshape(nt * gb, D_MODEL).astype(BF16)

    def proj(lo, hi):
        return jnp.dot(hb, win_ref[0, :, lo:hi], preferred_element_type=F32)

    a = proj(O_U, O_G) * _sigmoid(proj(O_G, O_ZC))
    convo_ref[0, 0:CONV_BUF - nt] = cconv_ref[0, nt:CONV_BUF]
    convo_ref[0, CONV_BUF - nt:CONV_BUF] = a.reshape(nt, gb, D_CONV)
    zc = proj(O_ZC, O_PV)
    ya_rows = []
    for tt in range(nt):
        acc = jnp.zeros((gb, D_CONV), F32)
        for j in range(CONV_WIDTH):
            i = tt + j
            xi = (cconv_ref[0, i] if i < CONV_BUF
                  else a[(i - CONV_BUF) * gb:(i - CONV_BUF + 1) * gb])
            acc = acc + xi * convw_ref[0, j:j + 1, :]
        conv = acc + convb_ref[0]
        ya_rows.append(_silu(_ln(conv) * cng_ref[0] + cnb_ref[0]))
    ya = jnp.concatenate(ya_rows, axis=0) * _silu(zc)

    pv = proj(O_PV, O_ZP)
    poolo_ref[0, 0:POOL_BUF - nt] = cpool_ref[0, nt:POOL_BUF]
    poolo_ref[0, POOL_BUF - nt:POOL_BUF] = pv.reshape(nt, gb, D_POOL)
    lane = lax.broadcasted_iota(jnp.int32, (gb, D_POOL), 1)
    wlen = jnp.where(lane < 64, 2, jnp.where(lane < 128, 4, jnp.where(lane < 192, 8, 16)))
    pooled_rows = []
    for tt in range(nt):
        acc = jnp.zeros((gb, D_POOL), F32)
        for d in range(max(POOL_WINDOWS)):
            i = POOL_BUF + tt - d
            xi = (cpool_ref[0, i] if i < POOL_BUF
                  else pv[(i - POOL_BUF) * gb:(i - POOL_BUF + 1) * gb])
            acc = acc + (xi if d < min(POOL_WINDOWS) else jnp.where(wlen > d, xi, 0.0))
        cnt = jnp.minimum(wlen, PAST_LEN + tt + 1).astype(F32)
        pooled_rows.append(acc / cnt - pv[tt * gb:(tt + 1) * gb])
    pooled = jnp.concatenate(pooled_rows, axis=0).astype(BF16)
    yb = jnp.dot(pooled, wpool_ref[0], preferred_element_type=F32) * pscale_ref[0]
    yb = yb * _silu(proj(O_ZP, O_Q))

    cos3 = cos_ref[...]
    sin3 = sin_ref[...]
    cos = jnp.broadcast_to(cos3, (nt, gb, 128)).reshape(nt * gb, 128)
    sin = jnp.broadcast_to(sin3, (nt, gb, 128)).reshape(nt * gb, 128)
    k = _rope(proj(O_K, O_V), cos, sin)
    v = proj(O_V, O_ZA)
    kn_s[...] = k
    vn_s[...] = v
    q = proj(O_Q, O_K) * (HEAD_DIM ** -0.5)
    for g, qg in enumerate(_group_heads([q[:, 128 * c:128 * (c + 1)] for c in range(GROUP)])):
        qs[:, 128 * g:128 * (g + 1)] = _rope(qg, cos, sin)

    for new_rows, by_batch, new_t in ((k, kbt, knt), (v, vbt, vnt)):
        for tt in range(nt):
            by_batch[pl.ds(tt, gb, stride=nt), :] = new_rows[tt * gb:(tt + 1) * gb]
        new_t[...] = by_batch[...].T
    per_blk = 128 // nt
    lane_kv = lax.broadcasted_iota(jnp.int32, (D_KV, WINDOW), 1)

    for blk in range(gb // per_blk):
        def roll_cache(j, carry, blk=blk):
            b = blk * per_blk + j
            shift = (WINDOW - nt - nt * j) & (WINDOW - 1)
            for new_t, cache_ref, out_ref in ((knt, ck_ref, ko_ref), (vnt, cv_ref, vo_ref)):
                old = pltpu.roll(cache_ref[0, b].reshape(D_KV, WINDOW), WINDOW - nt, axis=1)
                new = pltpu.roll(new_t[:, 128 * blk:128 * (blk + 1)], shift, axis=1)
                out_ref[0, b] = jnp.where(lane_kv >= WINDOW - nt, new, old).reshape(
                    N_KV_HEADS, HEAD_DIM, WINDOW)
            return carry
        lax.fori_loop(0, per_blk, roll_cache, 0, unroll=4)

    nrow = nt * GROUP * N_KV_HEADS * SUB
    r1 = lax.broadcasted_iota(jnp.int32, (nrow, SUB * WINDOW), 0)
    c1 = lax.broadcasted_iota(jnp.int32, (nrow, SUB * WINDOW), 1)
    mask1 = ((c1 >> 7) == (r1 & (SUB - 1))) & ((c1 & (WINDOW - 1)) > (r1 >> 6))
    r2 = lax.broadcasted_iota(jnp.int32, (nrow, nt * SUB), 0)
    c2 = lax.broadcasted_iota(jnp.int32, (nrow, nt * SUB), 1)
    mask2 = ((c2 & (SUB - 1)) == (r2 & (SUB - 1))) & ((c2 >> 3) <= (r2 >> 6))
    lane_q = lax.broadcasted_iota(jnp.int32, (SUB, 128), 1)
    low = lane_q < HEAD_DIM
    sink = sinkrow_ref[0]

    def sub_block(sb, carry):
        b0 = pl.multiple_of(sb * SUB, SUB)
        pieces = []
        for tt in range(nt):
            for g in range(GROUP):
                qp = qs[pl.ds(tt * gb + b0, SUB), 128 * g:128 * (g + 1)]
                pieces.append(jnp.where(low, qp, 0.0))
                pieces.append(jnp.where(low, 0.0, qp))
        lhs = jnp.concatenate(pieces, axis=0).astype(BF16)
        kblk = ck_ref[0, pl.ds(b0, SUB)]
        vblk = cv_ref[0, pl.ds(b0, SUB)]
        kc = jnp.concatenate(
            [kblk[bb].reshape(D_KV, WINDOW) for bb in range(SUB)], axis=1).astype(BF16)
        vc = jnp.concatenate(
            [vblk[bb].reshape(D_KV, WINDOW) for bb in range(SUB)], axis=1).astype(BF16)
        kn = jnp.concatenate(
            [kn_s[pl.ds(tt * gb + b0, SUB), :] for tt in range(nt)], axis=0).astype(BF16)
        vn = jnp.concatenate(
            [vn_s[pl.ds(tt * gb + b0, SUB), :] for tt in range(nt)], axis=0).astype(BF16)
        nt_dims = (((1,), (1,)), ((), ()))
        s1 = jnp.where(mask1, jnp.dot(lhs, kc, preferred_element_type=F32), MASK_VALUE)
        s2 = jnp.where(mask2, lax.dot_general(lhs, kn, nt_dims, preferred_element_type=F32),
                       MASK_VALUE)
        m = jnp.maximum(jnp.maximum(jnp.max(s1, axis=-1, keepdims=True),
                                    jnp.max(s2, axis=-1, keepdims=True)), sink)
        p1 = jnp.exp(s1 - m)
        p2 = jnp.exp(s2 - m)
        denom = (jnp.sum(p1, axis=-1, keepdims=True) + jnp.sum(p2, axis=-1, keepdims=True)
                 + jnp.exp(sink - m))
        o_all = (lax.dot_general(p1.astype(BF16), vc, nt_dims, preferred_element_type=F32)
                 + jnp.dot(p2.astype(BF16), vn, preferred_element_type=F32)) * (1.0 / denom)
        for tt in range(nt):
            for g in range(GROUP):
                r0 = ((tt * GROUP + g) * N_KV_HEADS) * SUB
                o = jnp.where(low, o_all[r0:r0 + SUB], o_all[r0 + SUB:r0 + 2 * SUB])
                ocat[pl.ds(tt * gb + b0, SUB), 128 * g:128 * (g + 1)] = o
        return carry

    lax.fori_loop(0, gb // SUB, sub_block, 0, unroll=2)
    o_nat = _ungroup_heads([ocat[:, 128 * g:128 * (g + 1)] for g in range(GROUP)])
    yc = jnp.concatenate(o_nat, axis=-1) * _silu(proj(O_ZA, D_IN))

    mixcat = jnp.concatenate([ya, yb, yc], axis=-1).astype(BF16)
    mix = jnp.dot(mixcat, wout_ref[0], preferred_element_type=F32).reshape(nt, gb, D_MODEL)
    y = ALPHA * x3 + (1.0 + gate)[None] * mix
    xo_ref[...] = _ln(y) * lng_ref[0][None] + lnb_ref[0][None]


def _decode_layer(lidx, x, mod_s, win, wout, conv_w, conv_b, cnorm_g, cnorm_b, wpool, pool_scale,
                  ln_g, ln_b, cos_s, sin_s, sink_rows, cache_conv, cache_pool, cache_k, cache_v,
                  rolled):
    nt, nbatch, _ = x.shape
    gb = DEC_GROUP
    carried = list(rolled)
    per_layer = lambda shape: pl.BlockSpec((1,) + shape, lambda g, l: (l[0],) + (0,) * len(shape))
    tok = lambda c: pl.BlockSpec((nt, gb, c), lambda g, l: (0, g, 0))
    time_major = lambda r, c: pl.BlockSpec((1, r, gb, c), lambda g, l: (l[0], 0, g, 0))
    kv_blk = pl.BlockSpec((1, gb, N_KV_HEADS, HEAD_DIM, WINDOW), lambda g, l: (l[0], g, 0, 0, 0))
    nrow = nt * GROUP * N_KV_HEADS * SUB
    grid_spec = pltpu.PrefetchScalarGridSpec(
        num_scalar_prefetch=1,
        grid=(nbatch // gb,),
        in_specs=[
            tok(D_MODEL),
            pl.BlockSpec((1, gb, 3 * D_MODEL), lambda g, l: (l[0], g, 0)),
            per_layer((D_MODEL, D_IN)),
            per_layer((D_MODEL, D_MODEL)),
            per_layer((CONV_WIDTH, D_CONV)),
            per_layer((1, D_CONV)),
            per_layer((1, D_CONV)),
            per_layer((1, D_CONV)),
            per_layer((D_POOL, D_POOL)),
            per_layer((1, D_POOL)),
            per_layer((1, D_MODEL)),
            per_layer((1, D_MODEL)),
            pl.BlockSpec((nt, 1, 128), lambda g, l: (0, 0, 0)),
            pl.BlockSpec((nt, 1, 128), lambda g, l: (0, 0, 0)),
            per_layer((nrow, 1)),
            time_major(CONV_BUF, D_CONV),
            time_major(POOL_BUF, D_POOL),
            kv_blk,
            kv_blk,
        ] + [pl.BlockSpec(memory_space=pl.ANY)] * len(carried),
        out_specs=[tok(D_MODEL), time_major(CONV_BUF, D_CONV), time_major(POOL_BUF, D_POOL),
                   kv_blk, kv_blk],
        scratch_shapes=[
            pltpu.VMEM((nt * gb, D_ATTN), F32),
            pltpu.VMEM((nt * gb, D_ATTN), F32),
            pltpu.VMEM((nt * gb, D_KV), F32),
            pltpu.VMEM((nt * gb, D_KV), F32),
            pltpu.VMEM((nt * gb, D_KV), F32),
            pltpu.VMEM((nt * gb, D_KV), F32),
            pltpu.VMEM((D_KV, nt * gb), F32),
            pltpu.VMEM((D_KV, nt * gb), F32),
        ],
    )
    out_shape = [jax.ShapeDtypeStruct((nt, nbatch, D_MODEL), F32)] + [
        jax.ShapeDtypeStruct(c.shape, F32) for c in (cache_conv, cache_pool, cache_k, cache_v)]
    operands = (lidx, x, mod_s, win, wout, conv_w, conv_b, cnorm_g, cnorm_b, wpool, pool_scale,
                ln_g, ln_b, cos_s, sin_s, sink_rows, cache_conv, cache_pool, cache_k, cache_v)
    n_in = len(operands)
    return pl.pallas_call(
        _decode_kernel,
        grid_spec=grid_spec,
        out_shape=out_shape,
        input_output_aliases={n_in + i: 1 + i for i in range(len(carried))},
        compiler_params=pltpu.CompilerParams(
            dimension_semantics=("arbitrary",), vmem_limit_bytes=VMEM_LIMIT),
    )(*operands, *carried)


def _rope_tables(pos):
    half = HEAD_DIM // 2
    inv_freq = ROPE_THETA ** (-jnp.arange(half, dtype=F32) * (2.0 / HEAD_DIM))
    ang = pos.astype(F32)[:, None] * inv_freq[None, :]
    cos = jnp.cos(ang)
    sin = jnp.sin(ang)
    cos_t = jnp.concatenate([cos, cos, cos, cos], axis=-1)
    sin_t = jnp.concatenate([-sin, sin, -sin, sin], axis=-1)
    return cos_t, sin_t


def kernel(x_prompt, x_sample, cache_conv, cache_pool, cache_k, cache_v, c_prompt, c_sample,
           w_in, w_out, conv_w, conv_b, cnorm_g, cnorm_b, pool_w, pool_scale, sinks,
           w_mod, b_mod, ln_g, ln_b):
    B, T, _ = x_prompt.shape
    nbatch, nt, _ = x_sample.shape

    win = w_in.astype(BF16)
    wout = w_out.astype(BF16)
    wpool = jnp.zeros((DEPTH, D_POOL, D_POOL), F32)
    for gi in range(len(POOL_WINDOWS)):
        sl = slice(gi * POOL_GROUP_DIM, (gi + 1) * POOL_GROUP_DIM)
        wpool = wpool.at[:, sl, sl].set(pool_w[:, gi])
    wpool = wpool.astype(BF16)
    sinks_gh = sinks.reshape(DEPTH, N_KV_HEADS, GROUP).transpose(0, 2, 1)
    nq = D_ATTN // 128
    order = ([2 * c + (2 * c) // GROUP for c in range(nq)]
             + [2 * c + 1 - (2 * c) // GROUP for c in range(nq)])
    sinks_ord = jnp.stack([sinks[:, hd] for hd in order], axis=1)
    sinks_flat = jnp.concatenate(
        [sinks_ord[:, :, None, None],
         jnp.full((DEPTH, N_HEADS, 1, 2 * BLOCK - 1), MASK_VALUE, F32)], axis=-1)
    sink_rows = jnp.broadcast_to(
        sinks_gh[:, None, :, :, None], (DEPTH, nt, GROUP, N_KV_HEADS, SUB)
    ).reshape(DEPTH, nt * GROUP * N_KV_HEADS * SUB, 1)

    r3 = lambda p: p.reshape(DEPTH, 1, -1)
    conv_b3, cng3, cnb3, pscale3, lng3, lnb3 = map(r3, (conv_b, cnorm_g, cnorm_b, pool_scale, ln_g, ln_b))

    cos_p, sin_p = _rope_tables(jnp.arange(T, dtype=jnp.int32))
    cos_s, sin_s = _rope_tables(PAST_LEN + jnp.arange(nt, dtype=jnp.int32))
    cos_s = cos_s.reshape(nt, 1, 128)
    sin_s = sin_s.reshape(nt, 1, 128)

    mod = _modulation(jnp.concatenate([c_sample, c_prompt], axis=0), w_mod, b_mod)
    mod_p = mod[:, nbatch:].reshape(DEPTH, B, 3, D_MODEL)

    cconv_t = cache_conv.transpose(0, 2, 1, 3)
    cpool_t = cache_pool.transpose(0, 2, 1, 3)
    ck_t = cache_k.transpose(0, 1, 3, 4, 2)
    cv_t = cache_v.transpose(0, 1, 3, 4, 2)

    xp = x_prompt
    xs = x_sample.transpose(1, 0, 2)
    conv_p, pool_p, k_p, v_p = ([] for _ in range(4))
    rolled = ()
    for l in range(DEPTH):
        lidx = jnp.full((1,), l, jnp.int32)
        xp, cp, pp, kp, vp = _prompt_layer(
            lidx, sinks_flat, xp, mod_p, win, wout, conv_w, conv_b3, cng3, cnb3, wpool, pscale3,
            lng3, lnb3, cos_p, sin_p)
        xs, *rolled = _decode_layer(
            lidx, xs, mod, win, wout, conv_w, conv_b3, cng3, cnb3, wpool, pscale3,
            lng3, lnb3, cos_s, sin_s, sink_rows, cconv_t, cpool_t, ck_t, cv_t, rolled)
        conv_p.append(cp); pool_p.append(pp); k_p.append(kp); v_p.append(vp)

    kv5 = lambda z: z.reshape(z.shape[:-1] + (N_KV_HEADS, HEAD_DIM))
    conv_s, pool_s, k_s, v_s = rolled
    return (xp, xs.transpose(1, 0, 2),
            jnp.stack(conv_p), jnp.stack(pool_p), kv5(jnp.stack(k_p)), kv5(jnp.stack(v_p)),
            conv_s.transpose(0, 2, 1, 3), pool_s.transpose(0, 2, 1, 3),
            k_s.transpose(0, 1, 4, 2, 3), v_s.transpose(0, 1, 4, 2, 3))
```

```python
import functools

import jax
import jax.numpy as jnp
from jax import lax
from jax.experimental import pallas as pl
from jax.experimental.pallas import tpu as pltpu

F32 = jnp.float32
BF16 = jnp.bfloat16

D_MODEL = 1024
DEPTH = 4
D_CONV = 256
D_POOL = 256
D_ATTN = 512
HEAD_DIM = 64
N_HEADS = 8
N_KV_HEADS = 2
GROUP = 4
D_KV = 128
WINDOW = 128
BLOCK = 128
CONV_WIDTH = 31
CONV_BUF = 30
POOL_WINDOWS = (2, 4, 8, 16)
POOL_GROUP_DIM = 64
POOL_BUF = 15
ROPE_THETA = 10000.0
LN_EPS = 1e-5
ALPHA = (2.0 * DEPTH) ** 0.25
LOG2E = 1.4426950408889634
MASK_VALUE = -1e30
PAST_LEN = 8192

O_U, O_G, O_ZC, O_PV, O_ZP, O_Q, O_K, O_V, O_ZA, D_IN = (
    0, 256, 512, 768, 1024, 1280, 1792, 1920, 2048, 2560)

SUBLANES = 8
CARRY = 32
TM = 1024
DEC_GROUP = 32
SUB = 8
VMEM_LIMIT = 48 * 1024 * 1024


def _sigmoid(x):
    return 1.0 / (1.0 + jnp.exp(-x))


def _silu(x):
    return x * _sigmoid(x)


def _ln(x, eps=LN_EPS):
    mu = jnp.mean(x, axis=-1, keepdims=True)
    xc = x - mu
    var = jnp.mean(xc * xc, axis=-1, keepdims=True)
    return xc * lax.rsqrt(var + eps)


def _rope(x, cos, sin_signed):
    lane = lax.broadcasted_iota(jnp.int32, x.shape, 1)
    first_half = (lane & 63) < 32
    rot = jnp.where(first_half, pltpu.roll(x, 96, axis=1), pltpu.roll(x, 32, axis=1))
    return x * cos + rot * sin_signed


def _swap_halves(x):
    return pltpu.roll(x, HEAD_DIM, axis=1)


def _group_heads(nat):
    low = lax.broadcasted_iota(jnp.int32, nat[0].shape, 1) < HEAD_DIM
    out = []
    for g in range(GROUP):
        a, b = nat[g // 2], nat[GROUP // 2 + g // 2]
        out.append(jnp.where(low, a, _swap_halves(b)) if g % 2 == 0
                   else jnp.where(low, _swap_halves(a), b))
    return out


def _ungroup_heads(grouped):
    low = lax.broadcasted_iota(jnp.int32, grouped[0].shape, 1) < HEAD_DIM
    nat = [None] * GROUP
    for c in range(GROUP // 2):
        even, odd = grouped[2 * c], grouped[2 * c + 1]
        nat[c] = jnp.where(low, even, _swap_halves(odd))
        nat[GROUP // 2 + c] = jnp.where(low, _swap_halves(even), odd)
    return nat


def _mod_kernel(c_ref, w_ref, b_ref, o_ref):
    c = c_ref[...]
    sc = _silu(c).astype(BF16)
    o_ref[0] = jnp.dot(sc, w_ref[0].astype(BF16), preferred_element_type=F32) + b_ref[0]


def _modulation(c_all, w_mod, b_mod):
    n = c_all.shape[0]
    tn = 768
    return pl.pallas_call(
        _mod_kernel,
        grid=(DEPTH, 3 * D_MODEL // tn),
        in_specs=[
            pl.BlockSpec((n, D_MODEL), lambda l, j: (0, 0)),
            pl.BlockSpec((1, D_MODEL, tn), lambda l, j: (l, 0, j)),
            pl.BlockSpec((1, 1, tn), lambda l, j: (l, 0, j)),
        ],
        out_specs=pl.BlockSpec((1, n, tn), lambda l, j: (l, 0, j)),
        out_shape=jax.ShapeDtypeStruct((DEPTH, n, 3 * D_MODEL), F32),
        compiler_params=pltpu.CompilerParams(
            dimension_semantics=("arbitrary", "arbitrary"), vmem_limit_bytes=VMEM_LIMIT),
    )(c_all, w_mod, b_mod.reshape(DEPTH, 1, 3 * D_MODEL))


def _prompt_kernel(l_ref, sink_ref,
                   x_ref, mod_ref, win_ref, wout_ref, convw_ref, convb_ref, cng_ref, cnb_ref,
                   wpool_ref, pscale_ref, lng_ref, lnb_ref, cos_ref, sin_ref,
                   xo_ref, convo_ref, poolo_ref, ko_ref, vo_ref,
                   acat, ashift, pcat, ps_a, ps_b, kcat0, kcat1, vcat0, vcat1, mixcat):
    del l_ref
    kcat, vcat = (kcat0, kcat1), (vcat0, vcat1)
    t = pl.program_id(1)
    last = pl.num_programs(1) - 1
    nb = TM // BLOCK

    @pl.when(t == 0)
    def _():
        acat[0:CARRY, :] = jnp.zeros((CARRY, D_CONV), F32)
        pcat[0:CARRY, :] = jnp.zeros((CARRY, D_POOL), F32)
        for kc, vc in zip(kcat, vcat):
            kc[0:BLOCK, :] = jnp.zeros((BLOCK, D_KV), BF16)
            vc[0:BLOCK, 0:D_KV] = jnp.zeros((BLOCK, D_KV), BF16)
            vc[:, D_KV:2 * D_KV] = jnp.ones((BLOCK + TM, D_KV), BF16)

    kprev = [kc[0:BLOCK, :] for kc in kcat]
    vprev = [vc[0:BLOCK, :] for vc in vcat]
    x = x_ref[0]
    shift = mod_ref[0, 0, 0:1, :]
    scale = mod_ref[0, 0, 1:2, :]
    gate = mod_ref[0, 0, 2:3, :]
    hb = (_ln(x) * (1.0 + scale) + shift).astype(BF16)

    def proj(lo, hi):
        return jnp.dot(hb, win_ref[0, :, lo:hi], preferred_element_type=F32)

    cos = cos_ref[...]
    sin = sin_ref[...]
    k = _rope(proj(O_K, O_V), cos, sin)
    v = proj(O_V, O_ZA)
    kcat[0][BLOCK:BLOCK + TM, :] = k.astype(BF16)
    kcat[1][BLOCK:BLOCK + TM, :] = _swap_halves(k).astype(BF16)
    vcat[0][BLOCK:BLOCK + TM, 0:D_KV] = v.astype(BF16)
    vcat[1][BLOCK:BLOCK + TM, 0:D_KV] = _swap_halves(v).astype(BF16)

    @pl.when(t == last)
    def _():
        ko_ref[0] = k[TM - WINDOW:TM, :]
        vo_ref[0] = v[TM - WINDOW:TM, :]

    row = lax.broadcasted_iota(jnp.int32, (BLOCK, 2 * BLOCK), 0)
    col = lax.broadcasted_iota(jnp.int32, (BLOCK, 2 * BLOCK), 1)
    rel = col - BLOCK - row
    band = (rel <= 0) & (rel > -WINDOW)
    band_first = band & (col + jnp.minimum(t, 1) * BLOCK >= BLOCK)
    lane_q = lax.broadcasted_iota(jnp.int32, (BLOCK, 2 * HEAD_DIM), 1)
    low = lane_q < HEAD_DIM

    q = proj(O_Q, O_K) * (HEAD_DIM ** -0.5 * LOG2E)
    za = proj(O_ZA, D_IN)
    nq = D_ATTN // 128
    qr = [_rope(q[:, 128 * c:128 * (c + 1)], cos, sin) for c in range(nq)]
    same = [(c, (2 * c) // GROUP) for c in range(nq)]
    other = [(c, 1 - (2 * c) // GROUP) for c in range(nq)]
    sink_fill = sink_ref[0]
    vrow = lax.broadcasted_iota(jnp.int32, (2 * BLOCK, 2 * D_KV), 0)
    vcol = lax.broadcasted_iota(jnp.int32, (2 * BLOCK, 2 * D_KV), 1)
    sink_row = (vrow == 0) & (vcol < D_KV)
    nt_dims = (((1,), (1,)), ((), ()))
    def attn_block(i):
        mask = band if i > 0 else band_first
        scores = []
        for copy, heads in ((0, same), (1, other)):
            lhs = jnp.concatenate(
                [jnp.where(low == (half == 0), qr[c][i * BLOCK:(i + 1) * BLOCK], 0.0)
                 for c, half in heads], axis=0).astype(BF16)
            kk = (kcat[copy][i * BLOCK:(i + 2) * BLOCK, :] if i > 0 else
                  jnp.concatenate([kprev[copy], kcat[copy][BLOCK:2 * BLOCK, :]], axis=0))
            scores.append(lax.dot_general(lhs, kk, nt_dims, preferred_element_type=F32))
        s = jnp.concatenate(scores, axis=0).reshape(N_HEADS, BLOCK, 2 * BLOCK)
        s = jnp.where(mask[None], s, sink_fill)
        m = jnp.max(s, axis=-1, keepdims=True)
        p = jnp.exp2(s - m).astype(BF16)
        half_rows = (N_HEADS // 2) * BLOCK
        outs = []
        for copy in range(2):
            vv = (vcat[copy][i * BLOCK:(i + 2) * BLOCK, :] if i > 0 else
                  jnp.concatenate([vprev[copy], vcat[copy][BLOCK:2 * BLOCK, :]], axis=0))
            vv = jnp.where(sink_row, jnp.zeros((), BF16), vv)
            pv = jnp.dot(p[copy * (N_HEADS // 2):(copy + 1) * (N_HEADS // 2)].reshape(
                half_rows, 2 * BLOCK), vv,
                preferred_element_type=F32).reshape(N_HEADS // 2, BLOCK, 2 * D_KV)
            outs.append(pv[:, :, 0:D_KV] / pv[:, :, D_KV:])
        for c in range(nq):
            o = jnp.where(low == (same[c][1] == 0), outs[0][c], outs[1][c])
            zg = za[i * BLOCK:(i + 1) * BLOCK, 128 * c:128 * (c + 1)]
            c0 = D_CONV + D_POOL + 128 * c
            mixcat[i * BLOCK:(i + 1) * BLOCK, c0:c0 + 128] = (o * _silu(zg)).astype(BF16)

    a = proj(O_U, O_G) * _sigmoid(proj(O_G, O_ZC))
    acat[CARRY:CARRY + TM, :] = a
    zc = proj(O_ZC, O_PV)
    for r in range(1, SUBLANES):
        ashift[r - 1] = acat[r:r + TM + CARRY - SUBLANES, :]
    rows = 64

    def conv_chunk(c):
        acc = jnp.zeros((rows, D_CONV), F32)
        for j in range(CONV_WIDTH):
            off = CARRY - CONV_BUF + j
            r, lo = off % SUBLANES, c * rows + off - off % SUBLANES
            tap = acat[lo:lo + rows, :] if r == 0 else ashift[r - 1, lo:lo + rows, :]
            acc = acc + tap * convw_ref[0, j:j + 1, :]
        conv = acc + convb_ref[0]
        ya = _silu(_ln(conv) * cng_ref[0] + cnb_ref[0])
        mixcat[c * rows:(c + 1) * rows, 0:D_CONV] = (
            ya * _silu(zc[c * rows:(c + 1) * rows])).astype(BF16)

    for i in range(nb):
        attn_block(i)
    for c in range(TM // rows):
        conv_chunk(c)

    kcat[0][0:BLOCK, :] = k[TM - BLOCK:TM].astype(BF16)
    kcat[1][0:BLOCK, :] = _swap_halves(k[TM - BLOCK:TM]).astype(BF16)
    vcat[0][0:BLOCK, 0:D_KV] = v[TM - BLOCK:TM].astype(BF16)
    vcat[1][0:BLOCK, 0:D_KV] = _swap_halves(v[TM - BLOCK:TM]).astype(BF16)

    @pl.when(t == last)
    def _():
        convo_ref[0] = acat[CARRY + TM - CONV_BUF:CARRY + TM, :]

    acat[0:CARRY, :] = acat[TM:TM + CARRY, :]

    pool_v = proj(O_PV, O_ZP)
    pcat[CARRY:CARRY + TM, :] = pool_v
    zp = proj(O_ZP, O_Q)
    n = CARRY + TM
    ps_a[8:n, :] = pcat[8:n, :] + pcat[7:n - 1, :]
    ps_b[16:n, :] = ps_a[16:n, :] + ps_a[14:n - 2, :]
    s2 = ps_a[CARRY:n, :]
    s4 = ps_b[CARRY:n, :]
    ps_a[24:n, :] = ps_b[24:n, :] + ps_b[20:n - 4, :]
    s8 = ps_a[CARRY:n, :]
    s16 = s8 + ps_a[CARRY - 8:n - 8, :]
    lane = lax.broadcasted_iota(jnp.int32, (TM, D_POOL), 1)
    sums = jnp.where(lane < 64, s2, jnp.where(lane < 128, s4, jnp.where(lane < 192, s8, s16)))
    wlen = jnp.where(lane < 64, 2.0, jnp.where(lane < 128, 4.0, jnp.where(lane < 192, 8.0, 16.0)))
    pos1 = (lax.broadcasted_iota(jnp.int32, (TM, D_POOL), 0) + (t * TM + 1)).astype(F32)
    cnt = jnp.minimum(wlen, pos1)
    pooled = (sums / cnt - pool_v).astype(BF16)
    yb = jnp.dot(pooled, wpool_ref[0], preferred_element_type=F32) * pscale_ref[0]
    mixcat[:, D_CONV:D_CONV + D_POOL] = (yb * _silu(zp)).astype(BF16)

    @pl.when(t == last)
    def _():
        poolo_ref[0] = pcat[CARRY + TM - POOL_BUF:CARRY + TM, :]

    pcat[0:CARRY, :] = pcat[TM:TM + CARRY, :]

    mix = jnp.dot(mixcat[...], wout_ref[0], preferred_element_type=F32)
    y = x + ((1.0 + gate) * (1.0 / ALPHA)) * mix
    xo_ref[0] = _ln(y, LN_EPS / ALPHA ** 2) * lng_ref[0] + lnb_ref[0]


def _prompt_layer(lidx, sinks_perm, x, mod_p, win, wout, conv_w, conv_b, cnorm_g, cnorm_b,
                  wpool, pool_scale, ln_g, ln_b, cos_t, sin_t):
    B, T, _ = x.shape
    nt = T // TM
    per_layer = lambda shape: pl.BlockSpec((1,) + shape, lambda b, t, l: (l[0],) + (0,) * len(shape))
    grid_spec = pltpu.PrefetchScalarGridSpec(
        num_scalar_prefetch=1,
        grid=(B, nt),
        in_specs=[
            per_layer((N_HEADS, 1, 2 * BLOCK)),
            pl.BlockSpec((1, TM, D_MODEL), lambda b, t, l: (b, t, 0)),
            pl.BlockSpec((1, 1, 3, D_MODEL), lambda b, t, l: (l[0], b, 0, 0)),
            per_layer((D_MODEL, D_IN)),
            per_layer((D_MODEL, D_MODEL)),
            per_layer((CONV_WIDTH, D_CONV)),
            per_layer((1, D_CONV)),
            per_layer((1, D_CONV)),
            per_layer((1, D_CONV)),
            per_layer((D_POOL, D_POOL)),
            per_layer((1, D_POOL)),
            per_layer((1, D_MODEL)),
            per_layer((1, D_MODEL)),
            pl.BlockSpec((TM, 128), lambda b, t, l: (t, 0)),
            pl.BlockSpec((TM, 128), lambda b, t, l: (t, 0)),
        ],
        out_specs=[
            pl.BlockSpec((1, TM, D_MODEL), lambda b, t, l: (b, t, 0)),
            pl.BlockSpec((1, CONV_BUF, D_CONV), lambda b, t, l: (b, 0, 0)),
            pl.BlockSpec((1, POOL_BUF, D_POOL), lambda b, t, l: (b, 0, 0)),
            pl.BlockSpec((1, WINDOW, D_KV), lambda b, t, l: (b, 0, 0)),
            pl.BlockSpec((1, WINDOW, D_KV), lambda b, t, l: (b, 0, 0)),
        ],
        scratch_shapes=[
            pltpu.VMEM((CARRY + TM, D_CONV), F32),
            pltpu.VMEM((SUBLANES - 1, CARRY + TM - SUBLANES, D_CONV), F32),
            pltpu.VMEM((CARRY + TM, D_POOL), F32),
            pltpu.VMEM((CARRY + TM, D_POOL), F32),
            pltpu.VMEM((CARRY + TM, D_POOL), F32),
            pltpu.VMEM((BLOCK + TM, D_KV), BF16),
            pltpu.VMEM((BLOCK + TM, D_KV), BF16),
            pltpu.VMEM((BLOCK + TM, 2 * D_KV), BF16),
            pltpu.VMEM((BLOCK + TM, 2 * D_KV), BF16),
            pltpu.VMEM((TM, D_MODEL), BF16),
        ],
    )
    out_shape = [
        jax.ShapeDtypeStruct((B, T, D_MODEL), F32),
        jax.ShapeDtypeStruct((B, CONV_BUF, D_CONV), F32),
        jax.ShapeDtypeStruct((B, POOL_BUF, D_POOL), F32),
        jax.ShapeDtypeStruct((B, WINDOW, D_KV), F32),
        jax.ShapeDtypeStruct((B, WINDOW, D_KV), F32),
    ]
    return pl.pallas_call(
        _prompt_kernel,
        grid_spec=grid_spec,
        out_shape=out_shape,
        compiler_params=pltpu.CompilerParams(
            dimension_semantics=("arbitrary", "arbitrary"), vmem_limit_bytes=VMEM_LIMIT),
    )(lidx, sinks_perm, x, mod_p, win, wout, conv_w, conv_b, cnorm_g, cnorm_b,
      wpool, pool_scale, ln_g, ln_b, cos_t, sin_t)


def _decode_kernel(l_ref,
                   x_ref, mod_ref, win_ref, wout_ref, convw_ref, convb_ref, cng_ref, cnb_ref,
                   wpool_ref, pscale_ref, lng_ref, lnb_ref, cos_ref, sin_ref, sinkrow_ref,
                   cconv_ref, cpool_ref, ck_ref, cv_ref, *rest):
    xo_ref, convo_ref, poolo_ref, ko_ref, vo_ref, qs, ocat, kn_s, vn_s, kbt, vbt, knt, vnt = rest[-13:]
    gb = DEC_GROUP
    nt = x_ref.shape[0]
    x3 = x_ref[...]
    shift = mod_ref[0, :, 0:D_MODEL]
    scale = mod_ref[0, :, D_MODEL:2 * D_MODEL]
    gate = mod_ref[0, :, 2 * D_MODEL:3 * D_MODEL]
    h3 = _ln(x3) * (1.0 + scale)[None] + shift[None]
    hb = h3.reshape(nt * gb, D_MODEL).astype(BF16)

    def proj(lo, hi):
        return jnp.dot(hb, win_ref[0, :, lo:hi], preferred_element_type=F32)

    a = proj(O_U, O_G) * _sigmoid(proj(O_G, O_ZC))
    convo_ref[0, 0:CONV_BUF - nt] = cconv_ref[0, nt:CONV_BUF]
    convo_ref[0, CONV_BUF - nt:CONV_BUF] = a.reshape(nt, gb, D_CONV)
    zc = proj(O_ZC, O_PV)
    ya_rows = []
    for tt in range(nt):
        acc = jnp.zeros((gb, D_CONV), F32)
        for j in range(CONV_WIDTH):
            i = tt + j
            xi = (cconv_ref[0, i] if i < CONV_BUF
                  else a[(i - CONV_BUF) * gb:(i - CONV_BUF + 1) * gb])
            acc = acc + xi * convw_ref[0, j:j + 1, :]
        conv = acc + convb_ref[0]
        ya_rows.append(_silu(_ln(conv) * cng_ref[0] + cnb_ref[0]))
    ya = jnp.concatenate(ya_rows, axis=0) * _silu(zc)

    pv = proj(O_PV, O_ZP)
    poolo_ref[0, 0:POOL_BUF - nt] = cpool_ref[0, nt:POOL_BUF]
    poolo_ref[0, POOL_BUF - nt:POOL_BUF] = pv.reshape(nt, gb, D_POOL)
    lane = lax.broadcasted_iota(jnp.int32, (gb, D_POOL), 1)
    wlen = jnp.where(lane < 64, 2, jnp.where(lane < 128, 4, jnp.where(lane < 192, 8, 16)))
    pooled_rows = []
    for tt in range(nt):
        acc = jnp.zeros((gb, D_POOL), F32)
        for d in range(max(POOL_WINDOWS)):
            i = POOL_BUF + tt - d
            xi = (cpool_ref[0, i] if i < POOL_BUF
                  else pv[(i - POOL_BUF) * gb:(i - POOL_BUF + 1) * gb])
            acc = acc + (xi if d < min(POOL_WINDOWS) else jnp.where(wlen > d, xi, 0.0))
        cnt = jnp.minimum(wlen, PAST_LEN + tt + 1).astype(F32)
        pooled_rows.append(acc / cnt - pv[tt * gb:(tt + 1) * gb])
    pooled = jnp.concatenate(pooled_rows, axis=0).astype(BF16)
    yb = jnp.dot(pooled, wpool_ref[0], preferred_element_type=F32) * pscale_ref[0]
    yb = yb * _silu(proj(O_ZP, O_Q))

    cos3 = cos_ref[...]
    sin3 = sin_ref[...]
    cos = jnp.broadcast_to(cos3, (nt, gb, 128)).reshape(nt * gb, 128)
    sin = jnp.broadcast_to(sin3, (nt, gb, 128)).reshape(nt * gb, 128)
    k = _rope(proj(O_K, O_V), cos, sin)
    v = proj(O_V, O_ZA)
    kn_s[...] = k
    vn_s[...] = v
    q = proj(O_Q, O_K) * (HEAD_DIM ** -0.5)
    for g, qg in enumerate(_group_heads([q[:, 128 * c:128 * (c + 1)] for c in range(GROUP)])):
        qs[:, 128 * g:128 * (g + 1)] = _rope(qg, cos, sin)

    for new_rows, by_batch, new_t in ((k, kbt, knt), (v, vbt, vnt)):
        for tt in range(nt):
            by_batch[pl.ds(tt, gb, stride=nt), :] = new_rows[tt * gb:(tt + 1) * gb]
        new_t[...] = by_batch[...].T
    per_blk = 128 // nt
    lane_kv = lax.broadcasted_iota(jnp.int32, (D_KV, WINDOW), 1)

    for blk in range(gb // per_blk):
        def roll_cache(j, carry, blk=blk):
            b = blk * per_blk + j
            shift = (WINDOW - nt - nt * j) & (WINDOW - 1)
            for new_t, cache_ref, out_ref in ((knt, ck_ref, ko_ref), (vnt, cv_ref, vo_ref)):
                old = pltpu.roll(cache_ref[0, b].reshape(D_KV, WINDOW), WINDOW - nt, axis=1)
                new = pltpu.roll(new_t[:, 128 * blk:128 * (blk + 1)], shift, axis=1)
                out_ref[0, b] = jnp.where(lane_kv >= WINDOW - nt, new, old).reshape(
                    N_KV_HEADS, HEAD_DIM, WINDOW)
            return carry
        lax.fori_loop(0, per_blk, roll_cache, 0, unroll=4)

    nrow = nt * GROUP * N_KV_HEADS * SUB
    r1 = lax.broadcasted_iota(jnp.int32, (nrow, SUB * WINDOW), 0)
    c1 = lax.broadcasted_iota(jnp.int32, (nrow, SUB * WINDOW), 1)
    mask1 = ((c1 >> 7) == (r1 & (SUB - 1))) & ((c1 & (WINDOW - 1)) > (r1 >> 6))
    r2 = lax.broadcasted_iota(jnp.int32, (nrow, nt * SUB), 0)
    c2 = lax.broadcasted_iota(jnp.int32, (nrow, nt * SUB), 1)
    mask2 = ((c2 & (SUB - 1)) == (r2 & (SUB - 1))) & ((c2 >> 3) <= (r2 >> 6))
    lane_q = lax.broadcasted_iota(jnp.int32, (SUB, 128), 1)
    low = lane_q < HEAD_DIM
    sink = sinkrow_ref[0]

    def sub_block(sb, carry):
        b0 = pl.multiple_of(sb * SUB, SUB)
        pieces = []
        for tt in range(nt):
            for g in range(GROUP):
                qp = qs[pl.ds(tt * gb + b0, SUB), 128 * g:128 * (g + 1)]
                pieces.append(jnp.where(low, qp, 0.0))
                pieces.append(jnp.where(low, 0.0, qp))
        lhs = jnp.concatenate(pieces, axis=0).astype(BF16)
        kblk = ck_ref[0, pl.ds(b0, SUB)]
        vblk = cv_ref[0, pl.ds(b0, SUB)]
        kc = jnp.concatenate(
            [kblk[bb].reshape(D_KV, WINDOW) for bb in range(SUB)], axis=1).astype(BF16)
        vc = jnp.concatenate(
            [vblk[bb].reshape(D_KV, WINDOW) for bb in range(SUB)], axis=1).astype(BF16)
        kn = jnp.concatenate(
            [kn_s[pl.ds(tt * gb + b0, SUB), :] for tt in range(nt)], axis=0).astype(BF16)
        vn = jnp.concatenate(
            [vn_s[pl.ds(tt * gb + b0, SUB), :] for tt in range(nt)], axis=0).astype(BF16)
        nt_dims = (((1,), (1,)), ((), ()))
        s1 = jnp.where(mask1, jnp.dot(lhs, kc, preferred_element_type=F32), MASK_VALUE)
        s2 = jnp.where(mask2, lax.dot_general(lhs, kn, nt_dims, preferred_element_type=F32),
                       MASK_VALUE)
        m = jnp.maximum(jnp.maximum(jnp.max(s1, axis=-1, keepdims=True),
                                    jnp.max(s2, axis=-1, keepdims=True)), sink)
        p1 = jnp.exp(s1 - m)
        p2 = jnp.exp(s2 - m)
        denom = (jnp.sum(p1, axis=-1, keepdims=True) + jnp.sum(p2, axis=-1, keepdims=True)
                 + jnp.exp(sink - m))
        o_all = (lax.dot_general(p1.astype(BF16), vc, nt_dims, preferred_element_type=F32)
                 + jnp.dot(p2.astype(BF16), vn, preferred_element_type=F32)) * (1.0 / denom)
        for tt in range(nt):
            for g in range(GROUP):
                r0 = ((tt * GROUP + g) * N_KV_HEADS) * SUB
                o = jnp.where(low, o_all[r0:r0 + SUB], o_all[r0 + SUB:r0 + 2 * SUB])
                ocat[pl.ds(tt * gb + b0, SUB), 128 * g:128 * (g + 1)] = o
        return carry

    lax.fori_loop(0, gb // SUB, sub_block, 0, unroll=2)
    o_nat = _ungroup_heads([ocat[:, 128 * g:128 * (g + 1)] for g in range(GROUP)])
    yc = jnp.concatenate(o_nat, axis=-1) * _silu(proj(O_ZA, D_IN))

    mixcat = jnp.concatenate([ya, yb, yc], axis=-1).astype(BF16)
    mix = jnp.dot(mixcat, wout_ref[0], preferred_element_type=F32).reshape(nt, gb, D_MODEL)
    y = ALPHA * x3 + (1.0 + gate)[None] * mix
    xo_ref[...] = _ln(y) * lng_ref[0][None] + lnb_ref[0][None]


def _decode_layer(lidx, x, mod_s, win, wout, conv_w, conv_b, cnorm_g, cnorm_b, wpool, pool_scale,
                  ln_g, ln_b, cos_s, sin_s, sink_rows, cache_conv, cache_pool, cache_k, cache_v,
                  rolled):
    nt, nbatch, _ = x.shape
    gb = DEC_GROUP
    carried = list(rolled)
    per_layer = lambda shape: pl.BlockSpec((1,) + shape, lambda g, l: (l[0],) + (0,) * len(shape))
    tok = lambda c: pl.BlockSpec((nt, gb, c), lambda g, l: (0, g, 0))
    time_major = lambda r, c: pl.BlockSpec((1, r, gb, c), lambda g, l: (l[0], 0, g, 0))
    kv_blk = pl.BlockSpec((1, gb, N_KV_HEADS, HEAD_DIM, WINDOW), lambda g, l: (l[0], g, 0, 0, 0))
    nrow = nt * GROUP * N_KV_HEADS * SUB
    grid_spec = pltpu.PrefetchScalarGridSpec(
        num_scalar_prefetch=1,
        grid=(nbatch // gb,),
        in_specs=[
            tok(D_MODEL),
            pl.BlockSpec((1, gb, 3 * D_MODEL), lambda g, l: (l[0], g, 0)),
            per_layer((D_MODEL, D_IN)),
            per_layer((D_MODEL, D_MODEL)),
            per_layer((CONV_WIDTH, D_CONV)),
            per_layer((1, D_CONV)),
            per_layer((1, D_CONV)),
            per_layer((1, D_CONV)),
            per_layer((D_POOL, D_POOL)),
            per_layer((1, D_POOL)),
            per_layer((1, D_MODEL)),
            per_layer((1, D_MODEL)),
            pl.BlockSpec((nt, 1, 128), lambda g, l: (0, 0, 0)),
            pl.BlockSpec((nt, 1, 128), lambda g, l: (0, 0, 0)),
            per_layer((nrow, 1)),
            time_major(CONV_BUF, D_CONV),
            time_major(POOL_BUF, D_POOL),
            kv_blk,
            kv_blk,
        ] + [pl.BlockSpec(memory_space=pl.ANY)] * len(carried),
        out_specs=[tok(D_MODEL), time_major(CONV_BUF, D_CONV), time_major(POOL_BUF, D_POOL),
                   kv_blk, kv_blk],
        scratch_shapes=[
            pltpu.VMEM((nt * gb, D_ATTN), F32),
            pltpu.VMEM((nt * gb, D_ATTN), F32),
            pltpu.VMEM((nt * gb, D_KV), F32),
            pltpu.VMEM((nt * gb, D_KV), F32),
            pltpu.VMEM((nt * gb, D_KV), F32),
            pltpu.VMEM((nt * gb, D_KV), F32),
            pltpu.VMEM((D_KV, nt * gb), F32),
            pltpu.VMEM((D_KV, nt * gb), F32),
        ],
    )
    out_shape = [jax.ShapeDtypeStruct((nt, nbatch, D_MODEL), F32)] + [
        jax.ShapeDtypeStruct(c.shape, F32) for c in (cache_conv, cache_pool, cache_k, cache_v)]
    operands = (lidx, x, mod_s, win, wout, conv_w, conv_b, cnorm_g, cnorm_b, wpool, pool_scale,
                ln_g, ln_b, cos_s, sin_s, sink_rows, cache_conv, cache_pool, cache_k, cache_v)
    n_in = len(operands)
    return pl.pallas_call(
        _decode_kernel,
        grid_spec=grid_spec,
        out_shape=out_shape,
        input_output_aliases={n_in + i: 1 + i for i in range(len(carried))},
        compiler_params=pltpu.CompilerParams(
            dimension_semantics=("arbitrary",), vmem_limit_bytes=VMEM_LIMIT),
    )(*operands, *carried)


def _rope_tables(pos):
    half = HEAD_DIM // 2
    inv_freq = ROPE_THETA ** (-jnp.arange(half, dtype=F32) * (2.0 / HEAD_DIM))
    ang = pos.astype(F32)[:, None] * inv_freq[None, :]
    cos = jnp.cos(ang)
    sin = jnp.sin(ang)
    cos_t = jnp.concatenate([cos, cos, cos, cos], axis=-1)
    sin_t = jnp.concatenate([-sin, sin, -sin, sin], axis=-1)
    return cos_t, sin_t


def kernel(x_prompt, x_sample, cache_conv, cache_pool, cache_k, cache_v, c_prompt, c_sample,
           w_in, w_out, conv_w, conv_b, cnorm_g, cnorm_b, pool_w, pool_scale, sinks,
           w_mod, b_mod, ln_g, ln_b):
    B, T, _ = x_prompt.shape
    nbatch, nt, _ = x_sample.shape

    win = w_in.astype(BF16)
    wout = w_out.astype(BF16)
    wpool = jnp.zeros((DEPTH, D_POOL, D_POOL), F32)
    for gi in range(len(POOL_WINDOWS)):
        sl = slice(gi * POOL_GROUP_DIM, (gi + 1) * POOL_GROUP_DIM)
        wpool = wpool.at[:, sl, sl].set(pool_w[:, gi])
    wpool = wpool.astype(BF16)
    sinks_gh = sinks.reshape(DEPTH, N_KV_HEADS, GROUP).transpose(0, 2, 1)
    nq = D_ATTN // 128
    order = ([2 * c + (2 * c) // GROUP for c in range(nq)]
             + [2 * c + 1 - (2 * c) // GROUP for c in range(nq)])
    sinks_ord = jnp.stack([sinks[:, hd] for hd in order], axis=1)
    sinks_flat = jnp.concatenate(
        [sinks_ord[:, :, None, None] * LOG2E,
         jnp.full((DEPTH, N_HEADS, 1, 2 * BLOCK - 1), MASK_VALUE, F32)], axis=-1)
    sink_rows = jnp.broadcast_to(
        sinks_gh[:, None, :, :, None], (DEPTH, nt, GROUP, N_KV_HEADS, SUB)
    ).reshape(DEPTH, nt * GROUP * N_KV_HEADS * SUB, 1)

    r3 = lambda p: p.reshape(DEPTH, 1, -1)
    conv_b3, cng3, cnb3, pscale3, lng3, lnb3 = map(r3, (conv_b, cnorm_g, cnorm_b, pool_scale, ln_g, ln_b))

    cos_p, sin_p = _rope_tables(jnp.arange(T, dtype=jnp.int32))
    cos_s, sin_s = _rope_tables(PAST_LEN + jnp.arange(nt, dtype=jnp.int32))
    cos_s = cos_s.reshape(nt, 1, 128)
    sin_s = sin_s.reshape(nt, 1, 128)

    mod = _modulation(jnp.concatenate([c_sample, c_prompt], axis=0), w_mod, b_mod)
    mod_p = mod[:, nbatch:].reshape(DEPTH, B, 3, D_MODEL)

    cconv_t = cache_conv.transpose(0, 2, 1, 3)
    cpool_t = cache_pool.transpose(0, 2, 1, 3)
    ck_t = cache_k.transpose(0, 1, 3, 4, 2)
    cv_t = cache_v.transpose(0, 1, 3, 4, 2)

    xp = x_prompt
    xs = x_sample.transpose(1, 0, 2)
    conv_p, pool_p, k_p, v_p = ([] for _ in range(4))
    rolled = ()
    for l in range(DEPTH):
        lidx = jnp.full((1,), l, jnp.int32)
        xp, cp, pp, kp, vp = _prompt_layer(
            lidx, sinks_flat, xp, mod_p, win, wout, conv_w, conv_b3, cng3, cnb3, wpool, pscale3,
            lng3, lnb3, cos_p, sin_p)
        xs, *rolled = _decode_layer(
            lidx, xs, mod, win, wout, conv_w, conv_b3, cng3, cnb3, wpool, pscale3,
            lng3, lnb3, cos_s, sin_s, sink_rows, cconv_t, cpool_t, ck_t, cv_t, rolled)
        conv_p.append(cp); pool_p.append(pp); k_p.append(kp); v_p.append(vp)

    kv5 = lambda z: z.reshape(z.shape[:-1] + (N_KV_HEADS, HEAD_DIM))
    conv_s, pool_s, k_s, v_s = rolled
    return (xp, xs.transpose(1, 0, 2),
            jnp.stack(conv_p), jnp.stack(pool_p), kv5(jnp.stack(k_p)), kv5(jnp.stack(v_p)),
            conv_s.transpose(0, 2, 1, 3), pool_s.transpose(0, 2, 1, 3),
            k_s.transpose(0, 1, 4, 2, 3), v_s.transpose(0, 1, 4, 2, 3))
```

```python
import functools

import jax
import jax.numpy as jnp
from jax import lax
from jax.experimental import pallas as pl
from jax.experimental.pallas import tpu as pltpu

F32 = jnp.float32
BF16 = jnp.bfloat16

D_MODEL = 1024
DEPTH = 4
D_CONV = 256
D_POOL = 256
D_ATTN = 512
HEAD_DIM = 64
N_HEADS = 8
N_KV_HEADS = 2
GROUP = 4
D_KV = 128
WINDOW = 128
BLOCK = 128
CONV_WIDTH = 31
CONV_BUF = 30
POOL_WINDOWS = (2, 4, 8, 16)
POOL_GROUP_DIM = 64
POOL_BUF = 15
ROPE_THETA = 10000.0
LN_EPS = 1e-5
ALPHA = (2.0 * DEPTH) ** 0.25
LOG2E = 1.4426950408889634
MASK_VALUE = -1e30
PAST_LEN = 8192

O_U, O_G, O_ZC, O_PV, O_ZP, O_Q, O_K, O_V, O_ZA, D_IN = (
    0, 256, 512, 768, 1024, 1280, 1792, 1920, 2048, 2560)

SUBLANES = 8
CARRY = 32
TM = 1024
ROW_PIECES = 4
DEC_GROUP = 32
SUB = 8
VMEM_LIMIT = 48 * 1024 * 1024


def _sigmoid(x):
    return 1.0 / (1.0 + jnp.exp(-x))


def _silu(x):
    return x * _sigmoid(x)


def _ln(x, eps=LN_EPS):
    mu = jnp.mean(x, axis=-1, keepdims=True)
    xc = x - mu
    var = jnp.mean(xc * xc, axis=-1, keepdims=True)
    return xc * lax.rsqrt(var + eps)


def _rope(x, cos, sin_signed):
    lane = lax.broadcasted_iota(jnp.int32, x.shape, 1)
    first_half = (lane & 63) < 32
    rot = jnp.where(first_half, pltpu.roll(x, 96, axis=1), pltpu.roll(x, 32, axis=1))
    return x * cos + rot * sin_signed


def _swap_halves(x):
    return pltpu.roll(x, HEAD_DIM, axis=1)


def _group_heads(nat):
    low = lax.broadcasted_iota(jnp.int32, nat[0].shape, 1) < HEAD_DIM
    out = []
    for g in range(GROUP):
        a, b = nat[g // 2], nat[GROUP // 2 + g // 2]
        out.append(jnp.where(low, a, _swap_halves(b)) if g % 2 == 0
                   else jnp.where(low, _swap_halves(a), b))
    return out


def _ungroup_heads(grouped):
    low = lax.broadcasted_iota(jnp.int32, grouped[0].shape, 1) < HEAD_DIM
    nat = [None] * GROUP
    for c in range(GROUP // 2):
        even, odd = grouped[2 * c], grouped[2 * c + 1]
        nat[c] = jnp.where(low, even, _swap_halves(odd))
        nat[GROUP // 2 + c] = jnp.where(low, _swap_halves(even), odd)
    return nat


def _mod_kernel(c_ref, w_ref, b_ref, o_ref):
    c = c_ref[...]
    sc = _silu(c).astype(BF16)
    o_ref[0] = jnp.dot(sc, w_ref[0].astype(BF16), preferred_element_type=F32) + b_ref[0]


def _modulation(c_all, w_mod, b_mod):
    n = c_all.shape[0]
    tn = 1536
    return pl.pallas_call(
        _mod_kernel,
        grid=(DEPTH, 3 * D_MODEL // tn),
        in_specs=[
            pl.BlockSpec((n, D_MODEL), lambda l, j: (0, 0)),
            pl.BlockSpec((1, D_MODEL, tn), lambda l, j: (l, 0, j)),
            pl.BlockSpec((1, 1, tn), lambda l, j: (l, 0, j)),
        ],
        out_specs=pl.BlockSpec((1, n, tn), lambda l, j: (l, 0, j)),
        out_shape=jax.ShapeDtypeStruct((DEPTH, n, 3 * D_MODEL), F32),
        compiler_params=pltpu.CompilerParams(
            dimension_semantics=("arbitrary", "arbitrary"), vmem_limit_bytes=VMEM_LIMIT),
    )(c_all, w_mod, b_mod.reshape(DEPTH, 1, 3 * D_MODEL))


def _prompt_kernel(l_ref, sink_ref,
                   x_ref, mod_ref, win_ref, wout_ref, convw_ref, convb_ref, cng_ref, cnb_ref,
                   wpool_ref, pscale_ref, lng_ref, lnb_ref, cos_ref, sin_ref,
                   xo_ref, convo_ref, poolo_ref, ko_ref, vo_ref,
                   acat, ashift, pcat, ps_a, ps_b, kcat0, kcat1, vcat0, vcat1, mixcat):
    del l_ref
    kcat, vcat = (kcat0, kcat1), (vcat0, vcat1)
    t = pl.program_id(1)
    last = pl.num_programs(1) - 1
    nb = TM // BLOCK

    @pl.when(t == 0)
    def _():
        acat[0:CARRY, :] = jnp.zeros((CARRY, D_CONV), F32)
        pcat[0:CARRY, :] = jnp.zeros((CARRY, D_POOL), F32)
        for kc, vc in zip(kcat, vcat):
            kc[0:BLOCK, :] = jnp.zeros((BLOCK, D_KV), BF16)
            vc[0:BLOCK, 0:D_KV] = jnp.zeros((BLOCK, D_KV), BF16)
            vc[:, D_KV:2 * D_KV] = jnp.ones((BLOCK + TM, D_KV), BF16)

    kprev = [kc[0:BLOCK, :] for kc in kcat]
    vprev = [vc[0:BLOCK, :] for vc in vcat]
    x = x_ref[0]
    shift = mod_ref[0, 0, 0:1, :]
    scale = mod_ref[0, 0, 1:2, :]
    gate = mod_ref[0, 0, 2:3, :]
    half = TM // ROW_PIECES
    nq = D_ATTN // 128
    cos = cos_ref[...]
    sin = sin_ref[...]
    hbs, ks, vs, zas, qrs = [], [], [], [], []
    for r0 in range(0, TM, half):
        hbs.append((_ln(x[r0:r0 + half]) * (1.0 + scale) + shift).astype(BF16))
        att = jnp.dot(hbs[-1], win_ref[0, :, O_Q:D_IN], preferred_element_type=F32)
        cos_p, sin_p = cos[r0:r0 + half], sin[r0:r0 + half]
        k_p = _rope(att[:, O_K - O_Q:O_V - O_Q], cos_p, sin_p)
        v_p = att[:, O_V - O_Q:O_ZA - O_Q]
        kcat[0][BLOCK + r0:BLOCK + r0 + half, :] = k_p.astype(BF16)
        kcat[1][BLOCK + r0:BLOCK + r0 + half, :] = _swap_halves(k_p).astype(BF16)
        vcat[0][BLOCK + r0:BLOCK + r0 + half, 0:D_KV] = v_p.astype(BF16)
        vcat[1][BLOCK + r0:BLOCK + r0 + half, 0:D_KV] = _swap_halves(v_p).astype(BF16)
        q_p = att[:, 0:D_ATTN] * (HEAD_DIM ** -0.5 * LOG2E)
        qrs.append([_rope(q_p[:, 128 * c:128 * (c + 1)], cos_p, sin_p) for c in range(nq)])
        ks.append(k_p)
        vs.append(v_p)
        zas.append(att[:, O_ZA - O_Q:D_IN - O_Q])
    hb = jnp.concatenate(hbs, axis=0)
    k = jnp.concatenate(ks, axis=0)
    v = jnp.concatenate(vs, axis=0)
    za = jnp.concatenate(zas, axis=0)
    qr = [jnp.concatenate([qp[c] for qp in qrs], axis=0) for c in range(nq)]

    def proj(lo, hi):
        return jnp.dot(hb, win_ref[0, :, lo:hi], preferred_element_type=F32)


    @pl.when(t == last)
    def _():
        ko_ref[0] = k[TM - WINDOW:TM, :]
        vo_ref[0] = v[TM - WINDOW:TM, :]

    row = lax.broadcasted_iota(jnp.int32, (BLOCK, 2 * BLOCK), 0)
    col = lax.broadcasted_iota(jnp.int32, (BLOCK, 2 * BLOCK), 1)
    rel = col - BLOCK - row
    band = (rel <= 0) & (rel > -WINDOW)
    band_first = band & (col + jnp.minimum(t, 1) * BLOCK >= BLOCK)
    lane_q = lax.broadcasted_iota(jnp.int32, (BLOCK, 2 * HEAD_DIM), 1)
    low = lane_q < HEAD_DIM

    same = [(c, (2 * c) // GROUP) for c in range(nq)]
    other = [(c, 1 - (2 * c) // GROUP) for c in range(nq)]
    sink_fill = sink_ref[0]
    vrow = lax.broadcasted_iota(jnp.int32, (2 * BLOCK, 2 * D_KV), 0)
    vcol = lax.broadcasted_iota(jnp.int32, (2 * BLOCK, 2 * D_KV), 1)
    sink_row = (vrow == 0) & (vcol < D_KV)
    nt_dims = (((1,), (1,)), ((), ()))
    def attn_block(i):
        mask = band if i > 0 else band_first
        scores = []
        for copy, heads in ((0, same), (1, other)):
            lhs = jnp.concatenate(
                [jnp.where(low == (half == 0), qr[c][i * BLOCK:(i + 1) * BLOCK], 0.0)
                 for c, half in heads], axis=0).astype(BF16)
            kk = (kcat[copy][i * BLOCK:(i + 2) * BLOCK, :] if i > 0 else
                  jnp.concatenate([kprev[copy], kcat[copy][BLOCK:2 * BLOCK, :]], axis=0))
            scores.append(lax.dot_general(lhs, kk, nt_dims, preferred_element_type=F32))
        s = jnp.concatenate(scores, axis=0).reshape(N_HEADS, BLOCK, 2 * BLOCK)
        s = jnp.where(mask[None], s, sink_fill)
        m = jnp.max(s, axis=-1, keepdims=True)
        p = jnp.exp2(s - m).astype(BF16)
        half_rows = (N_HEADS // 2) * BLOCK
        outs = []
        for copy in range(2):
            vv = (vcat[copy][i * BLOCK:(i + 2) * BLOCK, :] if i > 0 else
                  jnp.concatenate([vprev[copy], vcat[copy][BLOCK:2 * BLOCK, :]], axis=0))
            vv = jnp.where(sink_row, jnp.zeros((), BF16), vv)
            pv = jnp.dot(p[copy * (N_HEADS // 2):(copy + 1) * (N_HEADS // 2)].reshape(
                half_rows, 2 * BLOCK), vv,
                preferred_element_type=F32).reshape(N_HEADS // 2, BLOCK, 2 * D_KV)
            outs.append(pv[:, :, 0:D_KV] / pv[:, :, D_KV:])
        for c in range(nq):
            o = jnp.where(low == (same[c][1] == 0), outs[0][c], outs[1][c])
            zg = za[i * BLOCK:(i + 1) * BLOCK, 128 * c:128 * (c + 1)]
            c0 = D_CONV + D_POOL + 128 * c
            mixcat[i * BLOCK:(i + 1) * BLOCK, c0:c0 + 128] = (o * _silu(zg)).astype(BF16)

    a = proj(O_U, O_G) * _sigmoid(proj(O_G, O_ZC))
    acat[CARRY:CARRY + TM, :] = a
    zc = proj(O_ZC, O_PV)
    for r in range(1, SUBLANES):
        ashift[r - 1] = acat[r:r + TM + CARRY - SUBLANES, :]
    rows = 64

    def conv_chunk(c):
        acc = jnp.zeros((rows, D_CONV), F32)
        for j in range(CONV_WIDTH):
            off = CARRY - CONV_BUF + j
            r, lo = off % SUBLANES, c * rows + off - off % SUBLANES
            tap = acat[lo:lo + rows, :] if r == 0 else ashift[r - 1, lo:lo + rows, :]
            acc = acc + tap * convw_ref[0, j:j + 1, :]
        conv = acc + convb_ref[0]
        ya = _silu(_ln(conv) * cng_ref[0] + cnb_ref[0])
        mixcat[c * rows:(c + 1) * rows, 0:D_CONV] = (
            ya * _silu(zc[c * rows:(c + 1) * rows])).astype(BF16)

    pool_v = proj(O_PV, O_ZP)
    pcat[CARRY:CARRY + TM, :] = pool_v
    zp = proj(O_ZP, O_Q)

    for i in range(nb):
        attn_block(i)
    for c in range(TM // rows):
        conv_chunk(c)

    kcat[0][0:BLOCK, :] = k[TM - BLOCK:TM].astype(BF16)
    kcat[1][0:BLOCK, :] = _swap_halves(k[TM - BLOCK:TM]).astype(BF16)
    vcat[0][0:BLOCK, 0:D_KV] = v[TM - BLOCK:TM].astype(BF16)
    vcat[1][0:BLOCK, 0:D_KV] = _swap_halves(v[TM - BLOCK:TM]).astype(BF16)

    @pl.when(t == last)
    def _():
        convo_ref[0] = acat[CARRY + TM - CONV_BUF:CARRY + TM, :]

    acat[0:CARRY, :] = acat[TM:TM + CARRY, :]

    n = CARRY + TM
    ps_a[8:n, :] = pcat[8:n, :] + pcat[7:n - 1, :]
    ps_b[16:n, :] = ps_a[16:n, :] + ps_a[14:n - 2, :]
    s2 = ps_a[CARRY:n, :]
    s4 = ps_b[CARRY:n, :]
    ps_a[24:n, :] = ps_b[24:n, :] + ps_b[20:n - 4, :]
    s8 = ps_a[CARRY:n, :]
    s16 = s8 + ps_a[CARRY - 8:n - 8, :]
    lane = lax.broadcasted_iota(jnp.int32, (TM, D_POOL), 1)
    sums = jnp.where(lane < 64, s2, jnp.where(lane < 128, s4, jnp.where(lane < 192, s8, s16)))
    wlen = jnp.where(lane < 64, 2.0, jnp.where(lane < 128, 4.0, jnp.where(lane < 192, 8.0, 16.0)))
    pos1 = (lax.broadcasted_iota(jnp.int32, (TM, D_POOL), 0) + (t * TM + 1)).astype(F32)
    cnt = jnp.minimum(wlen, pos1)
    pooled = (sums / cnt - pool_v).astype(BF16)
    yb = jnp.dot(pooled, wpool_ref[0], preferred_element_type=F32) * pscale_ref[0]
    mixcat[:, D_CONV:D_CONV + D_POOL] = (yb * _silu(zp)).astype(BF16)

    @pl.when(t == last)
    def _():
        poolo_ref[0] = pcat[CARRY + TM - POOL_BUF:CARRY + TM, :]

    pcat[0:CARRY, :] = pcat[TM:TM + CARRY, :]

    g_res = (1.0 + gate) * (1.0 / ALPHA)
    for r0 in range(0, TM, half):
        mix = jnp.dot(mixcat[r0:r0 + half, :], wout_ref[0], preferred_element_type=F32)
        y = x[r0:r0 + half] + g_res * mix
        xo_ref[0, r0:r0 + half, :] = _ln(y, LN_EPS / ALPHA ** 2) * lng_ref[0] + lnb_ref[0]


def _prompt_layer(lidx, sinks_perm, x, mod_p, win, wout, conv_w, conv_b, cnorm_g, cnorm_b,
                  wpool, pool_scale, ln_g, ln_b, cos_t, sin_t):
    B, T, _ = x.shape
    nt = T // TM
    per_layer = lambda shape: pl.BlockSpec((1,) + shape, lambda b, t, l: (l[0],) + (0,) * len(shape))
    grid_spec = pltpu.PrefetchScalarGridSpec(
        num_scalar_prefetch=1,
        grid=(B, nt),
        in_specs=[
            per_layer((N_HEADS, 1, 2 * BLOCK)),
            pl.BlockSpec((1, TM, D_MODEL), lambda b, t, l: (b, t, 0)),
            pl.BlockSpec((1, 1, 3, D_MODEL), lambda b, t, l: (l[0], b, 0, 0)),
            per_layer((D_MODEL, D_IN)),
            per_layer((D_MODEL, D_MODEL)),
            per_layer((CONV_WIDTH, D_CONV)),
            per_layer((1, D_CONV)),
            per_layer((1, D_CONV)),
            per_layer((1, D_CONV)),
            per_layer((D_POOL, D_POOL)),
            per_layer((1, D_POOL)),
            per_layer((1, D_MODEL)),
            per_layer((1, D_MODEL)),
            pl.BlockSpec((TM, 128), lambda b, t, l: (t, 0)),
            pl.BlockSpec((TM, 128), lambda b, t, l: (t, 0)),
        ],
        out_specs=[
            pl.BlockSpec((1, TM, D_MODEL), lambda b, t, l: (b, t, 0)),
            pl.BlockSpec((1, CONV_BUF, D_CONV), lambda b, t, l: (b, 0, 0)),
            pl.BlockSpec((1, POOL_BUF, D_POOL), lambda b, t, l: (b, 0, 0)),
            pl.BlockSpec((1, WINDOW, D_KV), lambda b, t, l: (b, 0, 0)),
            pl.BlockSpec((1, WINDOW, D_KV), lambda b, t, l: (b, 0, 0)),
        ],
        scratch_shapes=[
            pltpu.VMEM((CARRY + TM, D_CONV), F32),
            pltpu.VMEM((SUBLANES - 1, CARRY + TM - SUBLANES, D_CONV), F32),
            pltpu.VMEM((CARRY + TM, D_POOL), F32),
            pltpu.VMEM((CARRY + TM, D_POOL), F32),
            pltpu.VMEM((CARRY + TM, D_POOL), F32),
            pltpu.VMEM((BLOCK + TM, D_KV), BF16),
            pltpu.VMEM((BLOCK + TM, D_KV), BF16),
            pltpu.VMEM((BLOCK + TM, 2 * D_KV), BF16),
            pltpu.VMEM((BLOCK + TM, 2 * D_KV), BF16),
            pltpu.VMEM((TM, D_MODEL), BF16),
        ],
    )
    out_shape = [
        jax.ShapeDtypeStruct((B, T, D_MODEL), F32),
        jax.ShapeDtypeStruct((B, CONV_BUF, D_CONV), F32),
        jax.ShapeDtypeStruct((B, POOL_BUF, D_POOL), F32),
        jax.ShapeDtypeStruct((B, WINDOW, D_KV), F32),
        jax.ShapeDtypeStruct((B, WINDOW, D_KV), F32),
    ]
    return pl.pallas_call(
        _prompt_kernel,
        grid_spec=grid_spec,
        out_shape=out_shape,
        compiler_params=pltpu.CompilerParams(
            dimension_semantics=("arbitrary", "arbitrary"), vmem_limit_bytes=VMEM_LIMIT),
    )(lidx, sinks_perm, x, mod_p, win, wout, conv_w, conv_b, cnorm_g, cnorm_b,
      wpool, pool_scale, ln_g, ln_b, cos_t, sin_t)


def _decode_kernel(l_ref,
                   x_ref, mod_ref, win_ref, wout_ref, convw_ref, convb_ref, cng_ref, cnb_ref,
                   wpool_ref, pscale_ref, lng_ref, lnb_ref, cos_ref, sin_ref, sinkrow_ref,
                   cconv_ref, cpool_ref, ck_ref, cv_ref, *rest):
    xo_ref, convo_ref, poolo_ref, ko_ref, vo_ref, qs, ocat, kn_s, vn_s, kbt, vbt, knt, vnt = rest[-13:]
    gb = DEC_GROUP
    nt = x_ref.shape[0]
    x3 = x_ref[...]
    shift = mod_ref[0, :, 0:D_MODEL]
    scale = mod_ref[0, :, D_MODEL:2 * D_MODEL]
    gate = mod_ref[0, :, 2 * D_MODEL:3 * D_MODEL]
    h3 = _ln(x3) * (1.0 + scale)[None] + shift[None]
    hb = h3.reshape(nt * gb, D_MODEL).astype(BF16)

    def proj(lo, hi):
        return jnp.dot(hb, win_ref[0, :, lo:hi], preferred_element_type=F32)

    a = proj(O_U, O_G) * _sigmoid(proj(O_G, O_ZC))
    convo_ref[0, 0:CONV_BUF - nt] = cconv_ref[0, nt:CONV_BUF]
    convo_ref[0, CONV_BUF - nt:CONV_BUF] = a.reshape(nt, gb, D_CONV)
    zc = proj(O_ZC, O_PV)
    ya_rows = []
    for tt in range(nt):
        acc = jnp.zeros((gb, D_CONV), F32)
        for j in range(CONV_WIDTH):
            i = tt + j
            xi = (cconv_ref[0, i] if i < CONV_BUF
                  else a[(i - CONV_BUF) * gb:(i - CONV_BUF + 1) * gb])
            acc = acc + xi * convw_ref[0, j:j + 1, :]
        conv = acc + convb_ref[0]
        ya_rows.append(_silu(_ln(conv) * cng_ref[0] + cnb_ref[0]))
    ya = jnp.concatenate(ya_rows, axis=0) * _silu(zc)

    pv = proj(O_PV, O_ZP)
    poolo_ref[0, 0:POOL_BUF - nt] = cpool_ref[0, nt:POOL_BUF]
    poolo_ref[0, POOL_BUF - nt:POOL_BUF] = pv.reshape(nt, gb, D_POOL)
    lane = lax.broadcasted_iota(jnp.int32, (gb, D_POOL), 1)
    wlen = jnp.where(lane < 64, 2, jnp.where(lane < 128, 4, jnp.where(lane < 192, 8, 16)))
    pooled_rows = []
    for tt in range(nt):
        acc = jnp.zeros((gb, D_POOL), F32)
        for d in range(max(POOL_WINDOWS)):
            i = POOL_BUF + tt - d
            xi = (cpool_ref[0, i] if i < POOL_BUF
                  else pv[(i - POOL_BUF) * gb:(i - POOL_BUF + 1) * gb])
            acc = acc + (xi if d < min(POOL_WINDOWS) else jnp.where(wlen > d, xi, 0.0))
        cnt = jnp.minimum(wlen, PAST_LEN + tt + 1).astype(F32)
        pooled_rows.append(acc / cnt - pv[tt * gb:(tt + 1) * gb])
    pooled = jnp.concatenate(pooled_rows, axis=0).astype(BF16)
    yb = jnp.dot(pooled, wpool_ref[0], preferred_element_type=F32) * pscale_ref[0]
    yb = yb * _silu(proj(O_ZP, O_Q))

    cos3 = cos_ref[...]
    sin3 = sin_ref[...]
    cos = jnp.broadcast_to(cos3, (nt, gb, 128)).reshape(nt * gb, 128)
    sin = jnp.broadcast_to(sin3, (nt, gb, 128)).reshape(nt * gb, 128)
    k = _rope(proj(O_K, O_V), cos, sin)
    v = proj(O_V, O_ZA)
    kn_s[...] = k
    vn_s[...] = v
    q = proj(O_Q, O_K) * (HEAD_DIM ** -0.5)
    for g, qg in enumerate(_group_heads([q[:, 128 * c:128 * (c + 1)] for c in range(GROUP)])):
        qs[:, 128 * g:128 * (g + 1)] = _rope(qg, cos, sin)

    for new_rows, by_batch, new_t in ((k, kbt, knt), (v, vbt, vnt)):
        for tt in range(nt):
            by_batch[pl.ds(tt, gb, stride=nt), :] = new_rows[tt * gb:(tt + 1) * gb]
        new_t[...] = by_batch[...].T
    per_blk = 128 // nt
    lane_kv = lax.broadcasted_iota(jnp.int32, (D_KV, WINDOW), 1)

    for blk in range(gb // per_blk):
        def roll_cache(j, carry, blk=blk):
            b = blk * per_blk + j
            shift = (WINDOW - nt - nt * j) & (WINDOW - 1)
            for new_t, cache_ref, out_ref in ((knt, ck_ref, ko_ref), (vnt, cv_ref, vo_ref)):
                old = pltpu.roll(cache_ref[0, b].reshape(D_KV, WINDOW), WINDOW - nt, axis=1)
                new = pltpu.roll(new_t[:, 128 * blk:128 * (blk + 1)], shift, axis=1)
                out_ref[0, b] = jnp.where(lane_kv >= WINDOW - nt, new, old).reshape(
                    N_KV_HEADS, HEAD_DIM, WINDOW)
            return carry
        lax.fori_loop(0, per_blk, roll_cache, 0, unroll=4)

    nrow = nt * GROUP * N_KV_HEADS * SUB
    r1 = lax.broadcasted_iota(jnp.int32, (nrow, SUB * WINDOW), 0)
    c1 = lax.broadcasted_iota(jnp.int32, (nrow, SUB * WINDOW), 1)
    mask1 = ((c1 >> 7) == (r1 & (SUB - 1))) & ((c1 & (WINDOW - 1)) > (r1 >> 6))
    r2 = lax.broadcasted_iota(jnp.int32, (nrow, nt * SUB), 0)
    c2 = lax.broadcasted_iota(jnp.int32, (nrow, nt * SUB), 1)
    mask2 = ((c2 & (SUB - 1)) == (r2 & (SUB - 1))) & ((c2 >> 3) <= (r2 >> 6))
    lane_q = lax.broadcasted_iota(jnp.int32, (SUB, 128), 1)
    low = lane_q < HEAD_DIM
    sink = sinkrow_ref[0]

    def sub_block(sb, carry):
        b0 = pl.multiple_of(sb * SUB, SUB)
        pieces = []
        for tt in range(nt):
            for g in range(GROUP):
                qp = qs[pl.ds(tt * gb + b0, SUB), 128 * g:128 * (g + 1)]
                pieces.append(jnp.where(low, qp, 0.0))
                pieces.append(jnp.where(low, 0.0, qp))
        lhs = jnp.concatenate(pieces, axis=0).astype(BF16)
        kblk = ck_ref[0, pl.ds(b0, SUB)]
        vblk = cv_ref[0, pl.ds(b0, SUB)]
        kc = jnp.concatenate(
            [kblk[bb].reshape(D_KV, WINDOW) for bb in range(SUB)], axis=1).astype(BF16)
        vc = jnp.concatenate(
            [vblk[bb].reshape(D_KV, WINDOW) for bb in range(SUB)], axis=1).astype(BF16)
        kn = jnp.concatenate(
            [kn_s[pl.ds(tt * gb + b0, SUB), :] for tt in range(nt)], axis=0).astype(BF16)
        vn = jnp.concatenate(
            [vn_s[pl.ds(tt * gb + b0, SUB), :] for tt in range(nt)], axis=0).astype(BF16)
        nt_dims = (((1,), (1,)), ((), ()))
        s1 = jnp.where(mask1, jnp.dot(lhs, kc, preferred_element_type=F32), MASK_VALUE)
        s2 = jnp.where(mask2, lax.dot_general(lhs, kn, nt_dims, preferred_element_type=F32),
                       MASK_VALUE)
        m = jnp.maximum(jnp.maximum(jnp.max(s1, axis=-1, keepdims=True),
                                    jnp.max(s2, axis=-1, keepdims=True)), sink)
        p1 = jnp.exp(s1 - m)
        p2 = jnp.exp(s2 - m)
        denom = (jnp.sum(p1, axis=-1, keepdims=True) + jnp.sum(p2, axis=-1, keepdims=True)
                 + jnp.exp(sink - m))
        o_all = (lax.dot_general(p1.astype(BF16), vc, nt_dims, preferred_element_type=F32)
                 + jnp.dot(p2.astype(BF16), vn, preferred_element_type=F32)) * (1.0 / denom)
        for tt in range(nt):
            for g in range(GROUP):
                r0 = ((tt * GROUP + g) * N_KV_HEADS) * SUB
                o = jnp.where(low, o_all[r0:r0 + SUB], o_all[r0 + SUB:r0 + 2 * SUB])
                ocat[pl.ds(tt * gb + b0, SUB), 128 * g:128 * (g + 1)] = o
        return carry

    lax.fori_loop(0, gb // SUB, sub_block, 0, unroll=2)
    o_nat = _ungroup_heads([ocat[:, 128 * g:128 * (g + 1)] for g in range(GROUP)])
    yc = jnp.concatenate(o_nat, axis=-1) * _silu(proj(O_ZA, D_IN))

    mixcat = jnp.concatenate([ya, yb, yc], axis=-1).astype(BF16)
    mix = jnp.dot(mixcat, wout_ref[0], preferred_element_type=F32).reshape(nt, gb, D_MODEL)
    y = ALPHA * x3 + (1.0 + gate)[None] * mix
    xo_ref[...] = _ln(y) * lng_ref[0][None] + lnb_ref[0][None]


def _decode_layer(lidx, x, mod_s, win, wout, conv_w, conv_b, cnorm_g, cnorm_b, wpool, pool_scale,
                  ln_g, ln_b, cos_s, sin_s, sink_rows, cache_conv, cache_pool, cache_k, cache_v,
                  rolled):
    nt, nbatch, _ = x.shape
    gb = DEC_GROUP
    carried = list(rolled)
    per_layer = lambda shape: pl.BlockSpec((1,) + shape, lambda g, l: (l[0],) + (0,) * len(shape))
    tok = lambda c: pl.BlockSpec((nt, gb, c), lambda g, l: (0, g, 0))
    time_major = lambda r, c: pl.BlockSpec((1, r, gb, c), lambda g, l: (l[0], 0, g, 0))
    kv_blk = pl.BlockSpec((1, gb, N_KV_HEADS, HEAD_DIM, WINDOW), lambda g, l: (l[0], g, 0, 0, 0))
    nrow = nt * GROUP * N_KV_HEADS * SUB
    grid_spec = pltpu.PrefetchScalarGridSpec(
        num_scalar_prefetch=1,
        grid=(nbatch // gb,),
        in_specs=[
            tok(D_MODEL),
            pl.BlockSpec((1, gb, 3 * D_MODEL), lambda g, l: (l[0], g, 0)),
            per_layer((D_MODEL, D_IN)),
            per_layer((D_MODEL, D_MODEL)),
            per_layer((CONV_WIDTH, D_CONV)),
            per_layer((1, D_CONV)),
            per_layer((1, D_CONV)),
            per_layer((1, D_CONV)),
            per_layer((D_POOL, D_POOL)),
            per_layer((1, D_POOL)),
            per_layer((1, D_MODEL)),
            per_layer((1, D_MODEL)),
            pl.BlockSpec((nt, 1, 128), lambda g, l: (0, 0, 0)),
            pl.BlockSpec((nt, 1, 128), lambda g, l: (0, 0, 0)),
            per_layer((nrow, 1)),
            time_major(CONV_BUF, D_CONV),
            time_major(POOL_BUF, D_POOL),
            kv_blk,
            kv_blk,
        ] + [pl.BlockSpec(memory_space=pl.ANY)] * len(carried),
        out_specs=[tok(D_MODEL), time_major(CONV_BUF, D_CONV), time_major(POOL_BUF, D_POOL),
                   kv_blk, kv_blk],
        scratch_shapes=[
            pltpu.VMEM((nt * gb, D_ATTN), F32),
            pltpu.VMEM((nt * gb, D_ATTN), F32),
            pltpu.VMEM((nt * gb, D_KV), F32),
            pltpu.VMEM((nt * gb, D_KV), F32),
            pltpu.VMEM((nt * gb, D_KV), F32),
            pltpu.VMEM((nt * gb, D_KV), F32),
            pltpu.VMEM((D_KV, nt * gb), F32),
            pltpu.VMEM((D_KV, nt * gb), F32),
        ],
    )
    out_shape = [jax.ShapeDtypeStruct((nt, nbatch, D_MODEL), F32)] + [
        jax.ShapeDtypeStruct(c.shape, F32) for c in (cache_conv, cache_pool, cache_k, cache_v)]
    operands = (lidx, x, mod_s, win, wout, conv_w, conv_b, cnorm_g, cnorm_b, wpool, pool_scale,
                ln_g, ln_b, cos_s, sin_s, sink_rows, cache_conv, cache_pool, cache_k, cache_v)
    n_in = len(operands)
    return pl.pallas_call(
        _decode_kernel,
        grid_spec=grid_spec,
        out_shape=out_shape,
        input_output_aliases={n_in + i: 1 + i for i in range(len(carried))},
        compiler_params=pltpu.CompilerParams(
            dimension_semantics=("arbitrary",), vmem_limit_bytes=VMEM_LIMIT),
    )(*operands, *carried)


def _rope_tables(pos):
    half = HEAD_DIM // 2
    inv_freq = ROPE_THETA ** (-jnp.arange(half, dtype=F32) * (2.0 / HEAD_DIM))
    ang = pos.astype(F32)[:, None] * inv_freq[None, :]
    cos = jnp.cos(ang)
    sin = jnp.sin(ang)
    cos_t = jnp.concatenate([cos, cos, cos, cos], axis=-1)
    sin_t = jnp.concatenate([-sin, sin, -sin, sin], axis=-1)
    return cos_t, sin_t


def kernel(x_prompt, x_sample, cache_conv, cache_pool, cache_k, cache_v, c_prompt, c_sample,
           w_in, w_out, conv_w, conv_b, cnorm_g, cnorm_b, pool_w, pool_scale, sinks,
           w_mod, b_mod, ln_g, ln_b):
    B, T, _ = x_prompt.shape
    nbatch, nt, _ = x_sample.shape

    win = w_in.astype(BF16)
    wout = w_out.astype(BF16)
    wpool = jnp.zeros((DEPTH, D_POOL, D_POOL), F32)
    for gi in range(len(POOL_WINDOWS)):
        sl = slice(gi * POOL_GROUP_DIM, (gi + 1) * POOL_GROUP_DIM)
        wpool = wpool.at[:, sl, sl].set(pool_w[:, gi])
    wpool = wpool.astype(BF16)
    sinks_gh = sinks.reshape(DEPTH, N_KV_HEADS, GROUP).transpose(0, 2, 1)
    nq = D_ATTN // 128
    order = ([2 * c + (2 * c) // GROUP for c in range(nq)]
             + [2 * c + 1 - (2 * c) // GROUP for c in range(nq)])
    sinks_ord = jnp.stack([sinks[:, hd] for hd in order], axis=1)
    sinks_flat = jnp.concatenate(
        [sinks_ord[:, :, None, None] * LOG2E,
         jnp.full((DEPTH, N_HEADS, 1, 2 * BLOCK - 1), MASK_VALUE, F32)], axis=-1)
    sink_rows = jnp.broadcast_to(
        sinks_gh[:, None, :, :, None], (DEPTH, nt, GROUP, N_KV_HEADS, SUB)
    ).reshape(DEPTH, nt * GROUP * N_KV_HEADS * SUB, 1)

    r3 = lambda p: p.reshape(DEPTH, 1, -1)
    conv_b3, cng3, cnb3, pscale3, lng3, lnb3 = map(r3, (conv_b, cnorm_g, cnorm_b, pool_scale, ln_g, ln_b))

    cos_p, sin_p = _rope_tables(jnp.arange(T, dtype=jnp.int32))
    cos_s, sin_s = _rope_tables(PAST_LEN + jnp.arange(nt, dtype=jnp.int32))
    cos_s = cos_s.reshape(nt, 1, 128)
    sin_s = sin_s.reshape(nt, 1, 128)

    mod = _modulation(jnp.concatenate([c_sample, c_prompt], axis=0), w_mod, b_mod)
    mod_p = mod[:, nbatch:].reshape(DEPTH, B, 3, D_MODEL)

    cconv_t = cache_conv.transpose(0, 2, 1, 3)
    cpool_t = cache_pool.transpose(0, 2, 1, 3)
    ck_t = cache_k.transpose(0, 1, 3, 4, 2)
    cv_t = cache_v.transpose(0, 1, 3, 4, 2)

    xp = x_prompt
    xs = x_sample.transpose(1, 0, 2)
    conv_p, pool_p, k_p, v_p = ([] for _ in range(4))
    rolled = ()
    for l in range(DEPTH):
        lidx = jnp.full((1,), l, jnp.int32)
        xp, cp, pp, kp, vp = _prompt_layer(
            lidx, sinks_flat, xp, mod_p, win, wout, conv_w, conv_b3, cng3, cnb3, wpool, pscale3,
            lng3, lnb3, cos_p, sin_p)
        xs, *rolled = _decode_layer(
            lidx, xs, mod, win, wout, conv_w, conv_b3, cng3, cnb3, wpool, pscale3,
            lng3, lnb3, cos_s, sin_s, sink_rows, cconv_t, cpool_t, ck_t, cv_t, rolled)
        conv_p.append(cp); pool_p.append(pp); k_p.append(kp); v_p.append(vp)

    kv5 = lambda z: z.reshape(z.shape[:-1] + (N_KV_HEADS, HEAD_DIM))
    conv_s, pool_s, k_s, v_s = rolled
    return (xp, xs.transpose(1, 0, 2),
            jnp.stack(conv_p), jnp.stack(pool_p), kv5(jnp.stack(k_p)), kv5(jnp.stack(v_p)),
            conv_s.transpose(0, 2, 1, 3), pool_s.transpose(0, 2, 1, 3),
            k_s.transpose(0, 1, 4, 2, 3), v_s.transpose(0, 1, 4, 2, 3))
```

```python
import functools

import jax
import jax.numpy as jnp
from jax import lax
from jax.experimental import pallas as pl
from jax.experimental.pallas import tpu as pltpu

F32 = jnp.float32
BF16 = jnp.bfloat16

D_MODEL = 1024
DEPTH = 4
D_CONV = 256
D_POOL = 256
D_ATTN = 512
HEAD_DIM = 64
N_HEADS = 8
N_KV_HEADS = 2
GROUP = 4
D_KV = 128
WINDOW = 128
BLOCK = 128
CONV_WIDTH = 31
CONV_BUF = 30
POOL_WINDOWS = (2, 4, 8, 16)
POOL_GROUP_DIM = 64
POOL_BUF = 15
ROPE_THETA = 10000.0
LN_EPS = 1e-5
ALPHA = (2.0 * DEPTH) ** 0.25
LOG2E = 1.4426950408889634
MASK_VALUE = -1e30
PAST_LEN = 8192

O_U, O_G, O_ZC, O_PV, O_ZP, O_Q, O_K, O_V, O_ZA, D_IN = (
    0, 256, 512, 768, 1024, 1280, 1792, 1920, 2048, 2560)

SUBLANES = 8
CARRY = 32
TM = 1024
ROW_PIECES = 4
OUT_PIECES = 4
DEC_GROUP = 32
SUB = 8
VMEM_LIMIT = 48 * 1024 * 1024


def _sigmoid(x):
    return 1.0 / (1.0 + jnp.exp(-x))


def _silu(x):
    return x * _sigmoid(x)


def _ln(x, eps=LN_EPS):
    mu = jnp.mean(x, axis=-1, keepdims=True)
    xc = x - mu
    var = jnp.mean(xc * xc, axis=-1, keepdims=True)
    return xc * lax.rsqrt(var + eps)


def _rope(x, cos, sin_signed):
    lane = lax.broadcasted_iota(jnp.int32, x.shape, 1)
    first_half = (lane & 63) < 32
    rot = jnp.where(first_half, pltpu.roll(x, 96, axis=1), pltpu.roll(x, 32, axis=1))
    return x * cos + rot * sin_signed


def _swap_halves(x):
    return pltpu.roll(x, HEAD_DIM, axis=1)


def _group_heads(nat):
    low = lax.broadcasted_iota(jnp.int32, nat[0].shape, 1) < HEAD_DIM
    out = []
    for g in range(GROUP):
        a, b = nat[g // 2], nat[GROUP // 2 + g // 2]
        out.append(jnp.where(low, a, _swap_halves(b)) if g % 2 == 0
                   else jnp.where(low, _swap_halves(a), b))
    return out


def _ungroup_heads(grouped):
    low = lax.broadcasted_iota(jnp.int32, grouped[0].shape, 1) < HEAD_DIM
    nat = [None] * GROUP
    for c in range(GROUP // 2):
        even, odd = grouped[2 * c], grouped[2 * c + 1]
        nat[c] = jnp.where(low, even, _swap_halves(odd))
        nat[GROUP // 2 + c] = jnp.where(low, _swap_halves(even), odd)
    return nat


def _mod_kernel(c_ref, w_ref, b_ref, o_ref):
    c = c_ref[...]
    sc = _silu(c).astype(BF16)
    o_ref[0] = jnp.dot(sc, w_ref[0].astype(BF16), preferred_element_type=F32) + b_ref[0]


def _modulation(c_all, w_mod, b_mod):
    n = c_all.shape[0]
    tn = 1536
    return pl.pallas_call(
        _mod_kernel,
        grid=(DEPTH, 3 * D_MODEL // tn),
        in_specs=[
            pl.BlockSpec((n, D_MODEL), lambda l, j: (0, 0)),
            pl.BlockSpec((1, D_MODEL, tn), lambda l, j: (l, 0, j)),
            pl.BlockSpec((1, 1, tn), lambda l, j: (l, 0, j)),
        ],
        out_specs=pl.BlockSpec((1, n, tn), lambda l, j: (l, 0, j)),
        out_shape=jax.ShapeDtypeStruct((DEPTH, n, 3 * D_MODEL), F32),
        compiler_params=pltpu.CompilerParams(
            dimension_semantics=("arbitrary", "arbitrary"), vmem_limit_bytes=VMEM_LIMIT),
    )(c_all, w_mod, b_mod.reshape(DEPTH, 1, 3 * D_MODEL))


def _prompt_kernel(l_ref, sink_ref,
                   x_ref, mod_ref, win_ref, wout_ref, convw_ref, convb_ref, cng_ref, cnb_ref,
                   wpool_ref, pscale_ref, lng_ref, lnb_ref, cos_ref, sin_ref,
                   xo_ref, convo_ref, poolo_ref, ko_ref, vo_ref,
                   acat, ashift, pcat, ps_a, ps_b, kcat0, kcat1, vcat0, vcat1, mixcat):
    del l_ref
    kcat, vcat = (kcat0, kcat1), (vcat0, vcat1)
    t = pl.program_id(1)
    last = pl.num_programs(1) - 1
    nb = TM // BLOCK

    @pl.when(t == 0)
    def _():
        acat[0:CARRY, :] = jnp.zeros((CARRY, D_CONV), F32)
        pcat[0:CARRY, :] = jnp.zeros((CARRY, D_POOL), F32)
        for kc, vc in zip(kcat, vcat):
            kc[0:BLOCK, :] = jnp.zeros((BLOCK, D_KV), BF16)
            vc[0:BLOCK, 0:D_KV] = jnp.zeros((BLOCK, D_KV), BF16)
            vc[:, D_KV:2 * D_KV] = jnp.ones((BLOCK + TM, D_KV), BF16)

    kprev = [kc[0:BLOCK, :] for kc in kcat]
    vprev = [vc[0:BLOCK, :] for vc in vcat]
    x = x_ref[0]
    shift = mod_ref[0, 0, 0:1, :]
    scale = mod_ref[0, 0, 1:2, :]
    gate = mod_ref[0, 0, 2:3, :]
    piece = TM // ROW_PIECES
    nq = D_ATTN // 128
    cos = cos_ref[...]
    sin = sin_ref[...]
    hbs, ks, vs, zas, qrs = [], [], [], [], []
    for r0 in range(0, TM, piece):
        hbs.append((_ln(x[r0:r0 + piece]) * (1.0 + scale) + shift).astype(BF16))
        pr = jnp.dot(hbs[-1], win_ref[0, :, O_Q:D_IN], preferred_element_type=F32)
        cos_p, sin_p = cos[r0:r0 + piece], sin[r0:r0 + piece]
        k_p = _rope(pr[:, O_K - O_Q:O_V - O_Q], cos_p, sin_p)
        v_p = pr[:, O_V - O_Q:O_ZA - O_Q]
        kcat[0][BLOCK + r0:BLOCK + r0 + piece, :] = k_p.astype(BF16)
        kcat[1][BLOCK + r0:BLOCK + r0 + piece, :] = _swap_halves(k_p).astype(BF16)
        vcat[0][BLOCK + r0:BLOCK + r0 + piece, 0:D_KV] = v_p.astype(BF16)
        vcat[1][BLOCK + r0:BLOCK + r0 + piece, 0:D_KV] = _swap_halves(v_p).astype(BF16)
        q_p = pr[:, 0:D_ATTN] * (HEAD_DIM ** -0.5 * LOG2E)
        qrs.append([_rope(q_p[:, 128 * c:128 * (c + 1)], cos_p, sin_p) for c in range(nq)])
        ks.append(k_p)
        vs.append(v_p)
        zas.append(pr[:, O_ZA - O_Q:D_IN - O_Q])
    hb = jnp.concatenate(hbs, axis=0)
    k = jnp.concatenate(ks, axis=0)
    v = jnp.concatenate(vs, axis=0)
    za = jnp.concatenate(zas, axis=0)
    qr = [jnp.concatenate([qp[c] for qp in qrs], axis=0) for c in range(nq)]

    def proj(lo, hi):
        return jnp.dot(hb, win_ref[0, :, lo:hi], preferred_element_type=F32)


    @pl.when(t == last)
    def _():
        ko_ref[0] = k[TM - WINDOW:TM, :]
        vo_ref[0] = v[TM - WINDOW:TM, :]

    row = lax.broadcasted_iota(jnp.int32, (BLOCK, 2 * BLOCK), 0)
    col = lax.broadcasted_iota(jnp.int32, (BLOCK, 2 * BLOCK), 1)
    rel = col - BLOCK - row
    band = (rel <= 0) & (rel > -WINDOW)
    band_first = band & (col + jnp.minimum(t, 1) * BLOCK >= BLOCK)
    lane_q = lax.broadcasted_iota(jnp.int32, (BLOCK, 2 * HEAD_DIM), 1)
    low = lane_q < HEAD_DIM

    same = [(c, (2 * c) // GROUP) for c in range(nq)]
    other = [(c, 1 - (2 * c) // GROUP) for c in range(nq)]
    sink_fill = sink_ref[0]
    vrow = lax.broadcasted_iota(jnp.int32, (2 * BLOCK, 2 * D_KV), 0)
    vcol = lax.broadcasted_iota(jnp.int32, (2 * BLOCK, 2 * D_KV), 1)
    sink_row = (vrow == 0) & (vcol < D_KV)
    nt_dims = (((1,), (1,)), ((), ()))
    def attn_block(i):
        mask = band if i > 0 else band_first
        scores = []
        for copy, heads in ((0, same), (1, other)):
            lhs = jnp.concatenate(
                [jnp.where(low == (half == 0), qr[c][i * BLOCK:(i + 1) * BLOCK], 0.0)
                 for c, half in heads], axis=0).astype(BF16)
            kk = (kcat[copy][i * BLOCK:(i + 2) * BLOCK, :] if i > 0 else
                  jnp.concatenate([kprev[copy], kcat[copy][BLOCK:2 * BLOCK, :]], axis=0))
            scores.append(lax.dot_general(lhs, kk, nt_dims, preferred_element_type=F32))
        s = jnp.concatenate(scores, axis=0).reshape(N_HEADS, BLOCK, 2 * BLOCK)
        s = jnp.where(mask[None], s, sink_fill)
        m = jnp.max(s, axis=-1, keepdims=True)
        p = jnp.exp2(s - m).astype(BF16)
        half_rows = (N_HEADS // 2) * BLOCK
        outs = []
        for copy in range(2):
            vv = (vcat[copy][i * BLOCK:(i + 2) * BLOCK, :] if i > 0 else
                  jnp.concatenate([vprev[copy], vcat[copy][BLOCK:2 * BLOCK, :]], axis=0))
            vv = jnp.where(sink_row, jnp.zeros((), BF16), vv)
            pv = jnp.dot(p[copy * (N_HEADS // 2):(copy + 1) * (N_HEADS // 2)].reshape(
                half_rows, 2 * BLOCK), vv,
                preferred_element_type=F32).reshape(N_HEADS // 2, BLOCK, 2 * D_KV)
            outs.append(pv[:, :, 0:D_KV] / pv[:, :, D_KV:])
        for c in range(nq):
            o = jnp.where(low == (same[c][1] == 0), outs[0][c], outs[1][c])
            zg = za[i * BLOCK:(i + 1) * BLOCK, 128 * c:128 * (c + 1)]
            c0 = D_CONV + D_POOL + 128 * c
            mixcat[i * BLOCK:(i + 1) * BLOCK, c0:c0 + 128] = (o * _silu(zg)).astype(BF16)

    a = proj(O_U, O_G) * _sigmoid(proj(O_G, O_ZC))
    acat[CARRY:CARRY + TM, :] = a
    zc = proj(O_ZC, O_PV)
    for r in range(1, SUBLANES):
        ashift[r - 1] = acat[r:r + TM + CARRY - SUBLANES, :]
    rows = 128

    def conv_chunk(c):
        acc = jnp.zeros((rows, D_CONV), F32)
        for j in range(CONV_WIDTH):
            off = CARRY - CONV_BUF + j
            r, lo = off % SUBLANES, c * rows + off - off % SUBLANES
            tap = acat[lo:lo + rows, :] if r == 0 else ashift[r - 1, lo:lo + rows, :]
            acc = acc + tap * convw_ref[0, j:j + 1, :]
        conv = acc + convb_ref[0]
        ya = _silu(_ln(conv) * cng_ref[0] + cnb_ref[0])
        mixcat[c * rows:(c + 1) * rows, 0:D_CONV] = (
            ya * _silu(zc[c * rows:(c + 1) * rows])).astype(BF16)

    pool_v = proj(O_PV, O_ZP)
    pcat[CARRY:CARRY + TM, :] = pool_v
    zp = proj(O_ZP, O_Q)

    for i in range(nb):
        attn_block(i)
    for c in range(TM // rows):
        conv_chunk(c)

    kcat[0][0:BLOCK, :] = k[TM - BLOCK:TM].astype(BF16)
    kcat[1][0:BLOCK, :] = _swap_halves(k[TM - BLOCK:TM]).astype(BF16)
    vcat[0][0:BLOCK, 0:D_KV] = v[TM - BLOCK:TM].astype(BF16)
    vcat[1][0:BLOCK, 0:D_KV] = _swap_halves(v[TM - BLOCK:TM]).astype(BF16)

    @pl.when(t == last)
    def _():
        convo_ref[0] = acat[CARRY + TM - CONV_BUF:CARRY + TM, :]

    acat[0:CARRY, :] = acat[TM:TM + CARRY, :]

    n = CARRY + TM
    ps_a[8:n, :] = pcat[8:n, :] + pcat[7:n - 1, :]
    ps_b[16:n, :] = ps_a[16:n, :] + ps_a[14:n - 2, :]
    s2 = ps_a[CARRY:n, :]
    s4 = ps_b[CARRY:n, :]
    ps_a[24:n, :] = ps_b[24:n, :] + ps_b[20:n - 4, :]
    s8 = ps_a[CARRY:n, :]
    s16 = s8 + ps_a[CARRY - 8:n - 8, :]
    lane = lax.broadcasted_iota(jnp.int32, (TM, D_POOL), 1)
    sums = jnp.where(lane < 64, s2, jnp.where(lane < 128, s4, jnp.where(lane < 192, s8, s16)))
    wlen = jnp.where(lane < 64, 2.0, jnp.where(lane < 128, 4.0, jnp.where(lane < 192, 8.0, 16.0)))
    pos1 = (lax.broadcasted_iota(jnp.int32, (TM, D_POOL), 0) + (t * TM + 1)).astype(F32)
    cnt = jnp.minimum(wlen, pos1)
    pooled = (sums / cnt - pool_v).astype(BF16)
    yb = jnp.dot(pooled, wpool_ref[0], preferred_element_type=F32) * pscale_ref[0]
    mixcat[:, D_CONV:D_CONV + D_POOL] = (yb * _silu(zp)).astype(BF16)

    @pl.when(t == last)
    def _():
        poolo_ref[0] = pcat[CARRY + TM - POOL_BUF:CARRY + TM, :]

    pcat[0:CARRY, :] = pcat[TM:TM + CARRY, :]

    g_res = (1.0 + gate) * (1.0 / ALPHA)
    piece = TM // OUT_PIECES
    for r0 in range(0, TM, piece):
        mix = jnp.dot(mixcat[r0:r0 + piece, :], wout_ref[0], preferred_element_type=F32)
        y = x[r0:r0 + piece] + g_res * mix
        xo_ref[0, r0:r0 + piece, :] = _ln(y, LN_EPS / ALPHA ** 2) * lng_ref[0] + lnb_ref[0]


def _prompt_layer(lidx, sinks_perm, x, mod_p, win, wout, conv_w, conv_b, cnorm_g, cnorm_b,
                  wpool, pool_scale, ln_g, ln_b, cos_t, sin_t):
    B, T, _ = x.shape
    nt = T // TM
    per_layer = lambda shape: pl.BlockSpec((1,) + shape, lambda b, t, l: (l[0],) + (0,) * len(shape))
    grid_spec = pltpu.PrefetchScalarGridSpec(
        num_scalar_prefetch=1,
        grid=(B, nt),
        in_specs=[
            per_layer((N_HEADS, 1, 2 * BLOCK)),
            pl.BlockSpec((1, TM, D_MODEL), lambda b, t, l: (b, t, 0)),
            pl.BlockSpec((1, 1, 3, D_MODEL), lambda b, t, l: (l[0], b, 0, 0)),
            per_layer((D_MODEL, D_IN)),
            per_layer((D_MODEL, D_MODEL)),
            per_layer((CONV_WIDTH, D_CONV)),
            per_layer((1, D_CONV)),
            per_layer((1, D_CONV)),
            per_layer((1, D_CONV)),
            per_layer((D_POOL, D_POOL)),
            per_layer((1, D_POOL)),
            per_layer((1, D_MODEL)),
            per_layer((1, D_MODEL)),
            pl.BlockSpec((TM, 128), lambda b, t, l: (t, 0)),
            pl.BlockSpec((TM, 128), lambda b, t, l: (t, 0)),
        ],
        out_specs=[
            pl.BlockSpec((1, TM, D_MODEL), lambda b, t, l: (b, t, 0)),
            pl.BlockSpec((1, CONV_BUF, D_CONV), lambda b, t, l: (b, 0, 0)),
            pl.BlockSpec((1, POOL_BUF, D_POOL), lambda b, t, l: (b, 0, 0)),
            pl.BlockSpec((1, WINDOW, D_KV), lambda b, t, l: (b, 0, 0)),
            pl.BlockSpec((1, WINDOW, D_KV), lambda b, t, l: (b, 0, 0)),
        ],
        scratch_shapes=[
            pltpu.VMEM((CARRY + TM, D_CONV), F32),
            pltpu.VMEM((SUBLANES - 1, CARRY + TM - SUBLANES, D_CONV), F32),
            pltpu.VMEM((CARRY + TM, D_POOL), F32),
            pltpu.VMEM((CARRY + TM, D_POOL), F32),
            pltpu.VMEM((CARRY + TM, D_POOL), F32),
            pltpu.VMEM((BLOCK + TM, D_KV), BF16),
            pltpu.VMEM((BLOCK + TM, D_KV), BF16),
            pltpu.VMEM((BLOCK + TM, 2 * D_KV), BF16),
            pltpu.VMEM((BLOCK + TM, 2 * D_KV), BF16),
            pltpu.VMEM((TM, D_MODEL), BF16),
        ],
    )
    out_shape = [
        jax.ShapeDtypeStruct((B, T, D_MODEL), F32),
        jax.ShapeDtypeStruct((B, CONV_BUF, D_CONV), F32),
        jax.ShapeDtypeStruct((B, POOL_BUF, D_POOL), F32),
        jax.ShapeDtypeStruct((B, WINDOW, D_KV), F32),
        jax.ShapeDtypeStruct((B, WINDOW, D_KV), F32),
    ]
    return pl.pallas_call(
        _prompt_kernel,
        grid_spec=grid_spec,
        out_shape=out_shape,
        compiler_params=pltpu.CompilerParams(
            dimension_semantics=("arbitrary", "arbitrary"), vmem_limit_bytes=VMEM_LIMIT),
    )(lidx, sinks_perm, x, mod_p, win, wout, conv_w, conv_b, cnorm_g, cnorm_b,
      wpool, pool_scale, ln_g, ln_b, cos_t, sin_t)


def _decode_kernel(l_ref,
                   x_ref, mod_ref, win_ref, wout_ref, convw_ref, convb_ref, cng_ref, cnb_ref,
                   wpool_ref, pscale_ref, lng_ref, lnb_ref, cos_ref, sin_ref, sinkrow_ref,
                   cconv_ref, cpool_ref, ck_ref, cv_ref, *rest):
    xo_ref, convo_ref, poolo_ref, ko_ref, vo_ref, qs, ocat, kn_s, vn_s, kbt, vbt, knt, vnt = rest[-13:]
    gb = DEC_GROUP
    nt = x_ref.shape[0]
    x3 = x_ref[...]
    shift = mod_ref[0, :, 0:D_MODEL]
    scale = mod_ref[0, :, D_MODEL:2 * D_MODEL]
    gate = mod_ref[0, :, 2 * D_MODEL:3 * D_MODEL]
    h3 = _ln(x3) * (1.0 + scale)[None] + shift[None]
    hb = h3.reshape(nt * gb, D_MODEL).astype(BF16)

    def proj(lo, hi):
        return jnp.dot(hb, win_ref[0, :, lo:hi], preferred_element_type=F32)

    a = proj(O_U, O_G) * _sigmoid(proj(O_G, O_ZC))
    convo_ref[0, 0:CONV_BUF - nt] = cconv_ref[0, nt:CONV_BUF]
    convo_ref[0, CONV_BUF - nt:CONV_BUF] = a.reshape(nt, gb, D_CONV)
    zc = proj(O_ZC, O_PV)
    ya_rows = []
    for tt in range(nt):
        acc = jnp.zeros((gb, D_CONV), F32)
        for j in range(CONV_WIDTH):
            i = tt + j
            xi = (cconv_ref[0, i] if i < CONV_BUF
                  else a[(i - CONV_BUF) * gb:(i - CONV_BUF + 1) * gb])
            acc = acc + xi * convw_ref[0, j:j + 1, :]
        conv = acc + convb_ref[0]
        ya_rows.append(_silu(_ln(conv) * cng_ref[0] + cnb_ref[0]))
    ya = jnp.concatenate(ya_rows, axis=0) * _silu(zc)

    pv = proj(O_PV, O_ZP)
    poolo_ref[0, 0:POOL_BUF - nt] = cpool_ref[0, nt:POOL_BUF]
    poolo_ref[0, POOL_BUF - nt:POOL_BUF] = pv.reshape(nt, gb, D_POOL)
    lane = lax.broadcasted_iota(jnp.int32, (gb, D_POOL), 1)
    wlen = jnp.where(lane < 64, 2, jnp.where(lane < 128, 4, jnp.where(lane < 192, 8, 16)))
    pooled_rows = []
    for tt in range(nt):
        acc = jnp.zeros((gb, D_POOL), F32)
        for d in range(max(POOL_WINDOWS)):
            i = POOL_BUF + tt - d
            xi = (cpool_ref[0, i] if i < POOL_BUF
                  else pv[(i - POOL_BUF) * gb:(i - POOL_BUF + 1) * gb])
            acc = acc + (xi if d < min(POOL_WINDOWS) else jnp.where(wlen > d, xi, 0.0))
        cnt = jnp.minimum(wlen, PAST_LEN + tt + 1).astype(F32)
        pooled_rows.append(acc / cnt - pv[tt * gb:(tt + 1) * gb])
    pooled = jnp.concatenate(pooled_rows, axis=0).astype(BF16)
    yb = jnp.dot(pooled, wpool_ref[0], preferred_element_type=F32) * pscale_ref[0]
    yb = yb * _silu(proj(O_ZP, O_Q))

    cos3 = cos_ref[...]
    sin3 = sin_ref[...]
    cos = jnp.broadcast_to(cos3, (nt, gb, 128)).reshape(nt * gb, 128)
    sin = jnp.broadcast_to(sin3, (nt, gb, 128)).reshape(nt * gb, 128)
    k = _rope(proj(O_K, O_V), cos, sin)
    v = proj(O_V, O_ZA)
    kn_s[...] = k
    vn_s[...] = v
    q = proj(O_Q, O_K) * (HEAD_DIM ** -0.5)
    for g, qg in enumerate(_group_heads([q[:, 128 * c:128 * (c + 1)] for c in range(GROUP)])):
        qs[:, 128 * g:128 * (g + 1)] = _rope(qg, cos, sin)

    for new_rows, by_batch, new_t in ((k, kbt, knt), (v, vbt, vnt)):
        for tt in range(nt):
            by_batch[pl.ds(tt, gb, stride=nt), :] = new_rows[tt * gb:(tt + 1) * gb]
        new_t[...] = by_batch[...].T
    per_blk = 128 // nt
    lane_kv = lax.broadcasted_iota(jnp.int32, (D_KV, WINDOW), 1)

    for blk in range(gb // per_blk):
        def roll_cache(j, carry, blk=blk):
            b = blk * per_blk + j
            shift = (WINDOW - nt - nt * j) & (WINDOW - 1)
            for new_t, cache_ref, out_ref in ((knt, ck_ref, ko_ref), (vnt, cv_ref, vo_ref)):
                old = pltpu.roll(cache_ref[0, b].reshape(D_KV, WINDOW), WINDOW - nt, axis=1)
                new = pltpu.roll(new_t[:, 128 * blk:128 * (blk + 1)], shift, axis=1)
                out_ref[0, b] = jnp.where(lane_kv >= WINDOW - nt, new, old).reshape(
                    N_KV_HEADS, HEAD_DIM, WINDOW)
            return carry
        lax.fori_loop(0, per_blk, roll_cache, 0, unroll=4)

    nrow = nt * GROUP * N_KV_HEADS * SUB
    r1 = lax.broadcasted_iota(jnp.int32, (nrow, SUB * WINDOW), 0)
    c1 = lax.broadcasted_iota(jnp.int32, (nrow, SUB * WINDOW), 1)
    mask1 = ((c1 >> 7) == (r1 & (SUB - 1))) & ((c1 & (WINDOW - 1)) > (r1 >> 6))
    r2 = lax.broadcasted_iota(jnp.int32, (nrow, nt * SUB), 0)
    c2 = lax.broadcasted_iota(jnp.int32, (nrow, nt * SUB), 1)
    mask2 = ((c2 & (SUB - 1)) == (r2 & (SUB - 1))) & ((c2 >> 3) <= (r2 >> 6))
    lane_q = lax.broadcasted_iota(jnp.int32, (SUB, 128), 1)
    low = lane_q < HEAD_DIM
    sink = sinkrow_ref[0]

    def sub_block(sb, carry):
        b0 = pl.multiple_of(sb * SUB, SUB)
        pieces = []
        for tt in range(nt):
            for g in range(GROUP):
                qp = qs[pl.ds(tt * gb + b0, SUB), 128 * g:128 * (g + 1)]
                pieces.append(jnp.where(low, qp, 0.0))
                pieces.append(jnp.where(low, 0.0, qp))
        lhs = jnp.concatenate(pieces, axis=0).astype(BF16)
        kblk = ck_ref[0, pl.ds(b0, SUB)]
        vblk = cv_ref[0, pl.ds(b0, SUB)]
        kc = jnp.concatenate(
            [kblk[bb].reshape(D_KV, WINDOW) for bb in range(SUB)], axis=1).astype(BF16)
        vc = jnp.concatenate(
            [vblk[bb].reshape(D_KV, WINDOW) for bb in range(SUB)], axis=1).astype(BF16)
        kn = jnp.concatenate(
            [kn_s[pl.ds(tt * gb + b0, SUB), :] for tt in range(nt)], axis=0).astype(BF16)
        vn = jnp.concatenate(
            [vn_s[pl.ds(tt * gb + b0, SUB), :] for tt in range(nt)], axis=0).astype(BF16)
        nt_dims = (((1,), (1,)), ((), ()))
        s1 = jnp.where(mask1, jnp.dot(lhs, kc, preferred_element_type=F32), MASK_VALUE)
        s2 = jnp.where(mask2, lax.dot_general(lhs, kn, nt_dims, preferred_element_type=F32),
                       MASK_VALUE)
        m = jnp.maximum(jnp.maximum(jnp.max(s1, axis=-1, keepdims=True),
                                    jnp.max(s2, axis=-1, keepdims=True)), sink)
        p1 = jnp.exp(s1 - m)
        p2 = jnp.exp(s2 - m)
        denom = (jnp.sum(p1, axis=-1, keepdims=True) + jnp.sum(p2, axis=-1, keepdims=True)
                 + jnp.exp(sink - m))
        o_all = (lax.dot_general(p1.astype(BF16), vc, nt_dims, preferred_element_type=F32)
                 + jnp.dot(p2.astype(BF16), vn, preferred_element_type=F32)) * (1.0 / denom)
        for tt in range(nt):
            for g in range(GROUP):
                r0 = ((tt * GROUP + g) * N_KV_HEADS) * SUB
                o = jnp.where(low, o_all[r0:r0 + SUB], o_all[r0 + SUB:r0 + 2 * SUB])
                ocat[pl.ds(tt * gb + b0, SUB), 128 * g:128 * (g + 1)] = o
        return carry

    lax.fori_loop(0, gb // SUB, sub_block, 0, unroll=2)
    o_nat = _ungroup_heads([ocat[:, 128 * g:128 * (g + 1)] for g in range(GROUP)])
    yc = jnp.concatenate(o_nat, axis=-1) * _silu(proj(O_ZA, D_IN))

    mixcat = jnp.concatenate([ya, yb, yc], axis=-1).astype(BF16)
    mix = jnp.dot(mixcat, wout_ref[0], preferred_element_type=F32).reshape(nt, gb, D_MODEL)
    y = ALPHA * x3 + (1.0 + gate)[None] * mix
    xo_ref[...] = _ln(y) * lng_ref[0][None] + lnb_ref[0][None]


def _decode_layer(lidx, x, mod_s, win, wout, conv_w, conv_b, cnorm_g, cnorm_b, wpool, pool_scale,
                  ln_g, ln_b, cos_s, sin_s, sink_rows, cache_conv, cache_pool, cache_k, cache_v,
                  rolled):
    nt, nbatch, _ = x.shape
    gb = DEC_GROUP
    carried = list(rolled)
    per_layer = lambda shape: pl.BlockSpec((1,) + shape, lambda g, l: (l[0],) + (0,) * len(shape))
    tok = lambda c: pl.BlockSpec((nt, gb, c), lambda g, l: (0, g, 0))
    time_major = lambda r, c: pl.BlockSpec((1, r, gb, c), lambda g, l: (l[0], 0, g, 0))
    kv_blk = pl.BlockSpec((1, gb, N_KV_HEADS, HEAD_DIM, WINDOW), lambda g, l: (l[0], g, 0, 0, 0))
    nrow = nt * GROUP * N_KV_HEADS * SUB
    grid_spec = pltpu.PrefetchScalarGridSpec(
        num_scalar_prefetch=1,
        grid=(nbatch // gb,),
        in_specs=[
            tok(D_MODEL),
            pl.BlockSpec((1, gb, 3 * D_MODEL), lambda g, l: (l[0], g, 0)),
            per_layer((D_MODEL, D_IN)),
            per_layer((D_MODEL, D_MODEL)),
            per_layer((CONV_WIDTH, D_CONV)),
            per_layer((1, D_CONV)),
            per_layer((1, D_CONV)),
            per_layer((1, D_CONV)),
            per_layer((D_POOL, D_POOL)),
            per_layer((1, D_POOL)),
            per_layer((1, D_MODEL)),
            per_layer((1, D_MODEL)),
            pl.BlockSpec((nt, 1, 128), lambda g, l: (0, 0, 0)),
            pl.BlockSpec((nt, 1, 128), lambda g, l: (0, 0, 0)),
            per_layer((nrow, 1)),
            time_major(CONV_BUF, D_CONV),
            time_major(POOL_BUF, D_POOL),
            kv_blk,
            kv_blk,
        ] + [pl.BlockSpec(memory_space=pl.ANY)] * len(carried),
        out_specs=[tok(D_MODEL), time_major(CONV_BUF, D_CONV), time_major(POOL_BUF, D_POOL),
                   kv_blk, kv_blk],
        scratch_shapes=[
            pltpu.VMEM((nt * gb, D_ATTN), F32),
            pltpu.VMEM((nt * gb, D_ATTN), F32),
            pltpu.VMEM((nt * gb, D_KV), F32),
            pltpu.VMEM((nt * gb, D_KV), F32),
            pltpu.VMEM((nt * gb, D_KV), F32),
            pltpu.VMEM((nt * gb, D_KV), F32),
            pltpu.VMEM((D_KV, nt * gb), F32),
            pltpu.VMEM((D_KV, nt * gb), F32),
        ],
    )
    out_shape = [jax.ShapeDtypeStruct((nt, nbatch, D_MODEL), F32)] + [
        jax.ShapeDtypeStruct(c.shape, F32) for c in (cache_conv, cache_pool, cache_k, cache_v)]
    operands = (lidx, x, mod_s, win, wout, conv_w, conv_b, cnorm_g, cnorm_b, wpool, pool_scale,
                ln_g, ln_b, cos_s, sin_s, sink_rows, cache_conv, cache_pool, cache_k, cache_v)
    n_in = len(operands)
    return pl.pallas_call(
        _decode_kernel,
        grid_spec=grid_spec,
        out_shape=out_shape,
        input_output_aliases={n_in + i: 1 + i for i in range(len(carried))},
        compiler_params=pltpu.CompilerParams(
            dimension_semantics=("arbitrary",), vmem_limit_bytes=VMEM_LIMIT),
    )(*operands, *carried)


def _rope_tables(pos):
    half = HEAD_DIM // 2
    inv_freq = ROPE_THETA ** (-jnp.arange(half, dtype=F32) * (2.0 / HEAD_DIM))
    ang = pos.astype(F32)[:, None] * inv_freq[None, :]
    cos = jnp.cos(ang)
    sin = jnp.sin(ang)
    cos_t = jnp.concatenate([cos, cos, cos, cos], axis=-1)
    sin_t = jnp.concatenate([-sin, sin, -sin, sin], axis=-1)
    return cos_t, sin_t


def kernel(x_prompt, x_sample, cache_conv, cache_pool, cache_k, cache_v, c_prompt, c_sample,
           w_in, w_out, conv_w, conv_b, cnorm_g, cnorm_b, pool_w, pool_scale, sinks,
           w_mod, b_mod, ln_g, ln_b):
    B, T, _ = x_prompt.shape
    nbatch, nt, _ = x_sample.shape

    win = w_in.astype(BF16)
    wout = w_out.astype(BF16)
    wpool = jnp.zeros((DEPTH, D_POOL, D_POOL), F32)
    for gi in range(len(POOL_WINDOWS)):
        sl = slice(gi * POOL_GROUP_DIM, (gi + 1) * POOL_GROUP_DIM)
        wpool = wpool.at[:, sl, sl].set(pool_w[:, gi])
    wpool = wpool.astype(BF16)
    sinks_gh = sinks.reshape(DEPTH, N_KV_HEADS, GROUP).transpose(0, 2, 1)
    nq = D_ATTN // 128
    order = ([2 * c + (2 * c) // GROUP for c in range(nq)]
             + [2 * c + 1 - (2 * c) // GROUP for c in range(nq)])
    sinks_ord = jnp.stack([sinks[:, hd] for hd in order], axis=1)
    sinks_flat = jnp.concatenate(
        [sinks_ord[:, :, None, None] * LOG2E,
         jnp.full((DEPTH, N_HEADS, 1, 2 * BLOCK - 1), MASK_VALUE, F32)], axis=-1)
    sink_rows = jnp.broadcast_to(
        sinks_gh[:, None, :, :, None], (DEPTH, nt, GROUP, N_KV_HEADS, SUB)
    ).reshape(DEPTH, nt * GROUP * N_KV_HEADS * SUB, 1)

    r3 = lambda p: p.reshape(DEPTH, 1, -1)
    conv_b3, cng3, cnb3, pscale3, lng3, lnb3 = map(r3, (conv_b, cnorm_g, cnorm_b, pool_scale, ln_g, ln_b))

    cos_p, sin_p = _rope_tables(jnp.arange(T, dtype=jnp.int32))
    cos_s, sin_s = _rope_tables(PAST_LEN + jnp.arange(nt, dtype=jnp.int32))
    cos_s = cos_s.reshape(nt, 1, 128)
    sin_s = sin_s.reshape(nt, 1, 128)

    mod = _modulation(jnp.concatenate([c_sample, c_prompt], axis=0), w_mod, b_mod)
    mod_p = mod[:, nbatch:].reshape(DEPTH, B, 3, D_MODEL)

    cconv_t = cache_conv.transpose(0, 2, 1, 3)
    cpool_t = cache_pool.transpose(0, 2, 1, 3)
    ck_t = cache_k.transpose(0, 1, 3, 4, 2)
    cv_t = cache_v.transpose(0, 1, 3, 4, 2)

    xp = x_prompt
    xs = x_sample.transpose(1, 0, 2)
    conv_p, pool_p, k_p, v_p = ([] for _ in range(4))
    rolled = ()
    for l in range(DEPTH):
        lidx = jnp.full((1,), l, jnp.int32)
        xp, cp, pp, kp, vp = _prompt_layer(
            lidx, sinks_flat, xp, mod_p, win, wout, conv_w, conv_b3, cng3, cnb3, wpool, pscale3,
            lng3, lnb3, cos_p, sin_p)
        xs, *rolled = _decode_layer(
            lidx, xs, mod, win, wout, conv_w, conv_b3, cng3, cnb3, wpool, pscale3,
            lng3, lnb3, cos_s, sin_s, sink_rows, cconv_t, cpool_t, ck_t, cv_t, rolled)
        conv_p.append(cp); pool_p.append(pp); k_p.append(kp); v_p.append(vp)

    kv5 = lambda z: z.reshape(z.shape[:-1] + (N_KV_HEADS, HEAD_DIM))
    conv_s, pool_s, k_s, v_s = rolled
    return (xp, xs.transpose(1, 0, 2),
            jnp.stack(conv_p), jnp.stack(pool_p), kv5(jnp.stack(k_p)), kv5(jnp.stack(v_p)),
            conv_s.transpose(0, 2, 1, 3), pool_s.transpose(0, 2, 1, 3),
            k_s.transpose(0, 1, 4, 2, 3), v_s.transpose(0, 1, 4, 2, 3))
```

```python
import functools

import jax
import jax.numpy as jnp
from jax import lax
from jax.experimental import pallas as pl
from jax.experimental.pallas import tpu as pltpu

F32 = jnp.float32
BF16 = jnp.bfloat16

D_MODEL = 1024
DEPTH = 4
D_CONV = 256
D_POOL = 256
D_ATTN = 512
HEAD_DIM = 64
N_HEADS = 8
N_KV_HEADS = 2
GROUP = 4
D_KV = 128
WINDOW = 128
BLOCK = 128
CONV_WIDTH = 31
CONV_BUF = 30
POOL_WINDOWS = (2, 4, 8, 16)
POOL_GROUP_DIM = 64
POOL_BUF = 15
ROPE_THETA = 10000.0
LN_EPS = 1e-5
ALPHA = (2.0 * DEPTH) ** 0.25
LOG2E = 1.4426950408889634
MASK_VALUE = -1e30
PAST_LEN = 8192

O_U, O_G, O_ZC, O_PV, O_ZP, O_Q, O_K, O_V, O_ZA, D_IN = (
    0, 256, 512, 768, 1024, 1280, 1792, 1920, 2048, 2560)

SUBLANES = 8
CARRY = 32
TM = 1024
ROW_PIECES = 4
OUT_PIECES = 4
DEC_GROUP = 32
SUB = 8
VMEM_LIMIT = 48 * 1024 * 1024


def _sigmoid(x):
    return 1.0 / (1.0 + jnp.exp(-x))


def _silu(x):
    return x * _sigmoid(x)


def _ln(x, eps=LN_EPS):
    mu = jnp.mean(x, axis=-1, keepdims=True)
    xc = x - mu
    var = jnp.mean(xc * xc, axis=-1, keepdims=True)
    return xc * lax.rsqrt(var + eps)


def _rope(x, cos, sin_signed):
    lane = lax.broadcasted_iota(jnp.int32, x.shape, 1)
    first_half = (lane & 63) < 32
    rot = jnp.where(first_half, pltpu.roll(x, 96, axis=1), pltpu.roll(x, 32, axis=1))
    return x * cos + rot * sin_signed


def _swap_halves(x):
    return pltpu.roll(x, HEAD_DIM, axis=1)


def _group_heads(nat):
    low = lax.broadcasted_iota(jnp.int32, nat[0].shape, 1) < HEAD_DIM
    out = []
    for g in range(GROUP):
        a, b = nat[g // 2], nat[GROUP // 2 + g // 2]
        out.append(jnp.where(low, a, _swap_halves(b)) if g % 2 == 0
                   else jnp.where(low, _swap_halves(a), b))
    return out


def _ungroup_heads(grouped):
    low = lax.broadcasted_iota(jnp.int32, grouped[0].shape, 1) < HEAD_DIM
    nat = [None] * GROUP
    for c in range(GROUP // 2):
        even, odd = grouped[2 * c], grouped[2 * c + 1]
        nat[c] = jnp.where(low, even, _swap_halves(odd))
        nat[GROUP // 2 + c] = jnp.where(low, _swap_halves(even), odd)
    return nat


def _mod_kernel(c_ref, w_ref, b_ref, o_ref):
    c = c_ref[...]
    sc = _silu(c).astype(BF16)
    o_ref[0] = jnp.dot(sc, w_ref[0].astype(BF16), preferred_element_type=F32) + b_ref[0]


def _modulation(c_all, w_mod, b_mod):
    n = c_all.shape[0]
    tn = 1536
    return pl.pallas_call(
        _mod_kernel,
        grid=(DEPTH, 3 * D_MODEL // tn),
        in_specs=[
            pl.BlockSpec((n, D_MODEL), lambda l, j: (0, 0)),
            pl.BlockSpec((1, D_MODEL, tn), lambda l, j: (l, 0, j)),
            pl.BlockSpec((1, 1, tn), lambda l, j: (l, 0, j)),
        ],
        out_specs=pl.BlockSpec((1, n, tn), lambda l, j: (l, 0, j)),
        out_shape=jax.ShapeDtypeStruct((DEPTH, n, 3 * D_MODEL), F32),
        compiler_params=pltpu.CompilerParams(
            dimension_semantics=("arbitrary", "arbitrary"), vmem_limit_bytes=VMEM_LIMIT),
    )(c_all, w_mod, b_mod.reshape(DEPTH, 1, 3 * D_MODEL))


def _prompt_kernel(l_ref, sink_ref,
                   x_ref, mod_ref, win_ref, wout_ref, convw_ref, convb_ref, cng_ref, cnb_ref,
                   wpool_ref, pscale_ref, lng_ref, lnb_ref, cos_ref, sin_ref,
                   xo_ref, convo_ref, poolo_ref, ko_ref, vo_ref,
                   acat, ashift, pcat, ps_a, ps_b, kcat0, kcat1, vcat0, vcat1, mixcat):
    del l_ref
    kcat, vcat = (kcat0, kcat1), (vcat0, vcat1)
    t = pl.program_id(1)
    last = pl.num_programs(1) - 1
    nb = TM // BLOCK

    @pl.when(t == 0)
    def _():
        acat[0:CARRY, :] = jnp.zeros((CARRY, D_CONV), F32)
        pcat[0:CARRY, :] = jnp.zeros((CARRY, D_POOL), F32)
        for kc, vc in zip(kcat, vcat):
            kc[0:BLOCK, :] = jnp.zeros((BLOCK, D_KV), BF16)
            vc[0:BLOCK, 0:D_KV] = jnp.zeros((BLOCK, D_KV), BF16)
            vc[:, D_KV:2 * D_KV] = jnp.ones((BLOCK + TM, D_KV), BF16)

    kprev = [kc[0:BLOCK, :] for kc in kcat]
    vprev = [vc[0:BLOCK, :] for vc in vcat]
    x = x_ref[0]
    shift = mod_ref[0, 0, 0:1, :]
    scale = mod_ref[0, 0, 1:2, :]
    gate = mod_ref[0, 0, 2:3, :]
    piece = TM // ROW_PIECES
    nq = D_ATTN // 128
    cos = cos_ref[...]
    sin = sin_ref[...]
    hbs, ks, vs, zas, qrs = [], [], [], [], []
    for r0 in range(0, TM, piece):
        hbs.append((_ln(x[r0:r0 + piece]) * (1.0 + scale) + shift).astype(BF16))
        pr = jnp.dot(hbs[-1], win_ref[0, :, O_Q:D_IN], preferred_element_type=F32)
        cos_p, sin_p = cos[r0:r0 + piece], sin[r0:r0 + piece]
        k_p = _rope(pr[:, O_K - O_Q:O_V - O_Q], cos_p, sin_p)
        v_p = pr[:, O_V - O_Q:O_ZA - O_Q]
        kcat[0][BLOCK + r0:BLOCK + r0 + piece, :] = k_p.astype(BF16)
        kcat[1][BLOCK + r0:BLOCK + r0 + piece, :] = _swap_halves(k_p).astype(BF16)
        vcat[0][BLOCK + r0:BLOCK + r0 + piece, 0:D_KV] = v_p.astype(BF16)
        vcat[1][BLOCK + r0:BLOCK + r0 + piece, 0:D_KV] = _swap_halves(v_p).astype(BF16)
        q_p = pr[:, 0:D_ATTN] * (HEAD_DIM ** -0.5 * LOG2E)
        qrs.append([_rope(q_p[:, 128 * c:128 * (c + 1)], cos_p, sin_p) for c in range(nq)])
        ks.append(k_p)
        vs.append(v_p)
        zas.append(pr[:, O_ZA - O_Q:D_IN - O_Q])
    hb = jnp.concatenate(hbs, axis=0)
    k = jnp.concatenate(ks, axis=0)
    v = jnp.concatenate(vs, axis=0)
    za = jnp.concatenate(zas, axis=0)
    qr = [jnp.concatenate([qp[c] for qp in qrs], axis=0) for c in range(nq)]

    def proj(lo, hi):
        return jnp.dot(hb, win_ref[0, :, lo:hi], preferred_element_type=F32)


    @pl.when(t == last)
    def _():
        ko_ref[0] = k[TM - WINDOW:TM, :]
        vo_ref[0] = v[TM - WINDOW:TM, :]

    row = lax.broadcasted_iota(jnp.int32, (BLOCK, 2 * BLOCK), 0)
    col = lax.broadcasted_iota(jnp.int32, (BLOCK, 2 * BLOCK), 1)
    rel = col - BLOCK - row
    band = (rel <= 0) & (rel > -WINDOW)
    band_first = band & (col + jnp.minimum(t, 1) * BLOCK >= BLOCK)
    lane_q = lax.broadcasted_iota(jnp.int32, (BLOCK, 2 * HEAD_DIM), 1)
    low = lane_q < HEAD_DIM

    same = [(c, (2 * c) // GROUP) for c in range(nq)]
    other = [(c, 1 - (2 * c) // GROUP) for c in range(nq)]
    sink_fill = sink_ref[0]
    vrow = lax.broadcasted_iota(jnp.int32, (2 * BLOCK, 2 * D_KV), 0)
    vcol = lax.broadcasted_iota(jnp.int32, (2 * BLOCK, 2 * D_KV), 1)
    sink_row = (vrow == 0) & (vcol < D_KV)
    nt_dims = (((1,), (1,)), ((), ()))
    def attn_block(i):
        mask = band if i > 0 else band_first
        scores = []
        for copy, heads in ((0, same), (1, other)):
            lhs = jnp.concatenate(
                [jnp.where(low == (half == 0), qr[c][i * BLOCK:(i + 1) * BLOCK], 0.0)
                 for c, half in heads], axis=0).astype(BF16)
            kk = (kcat[copy][i * BLOCK:(i + 2) * BLOCK, :] if i > 0 else
                  jnp.concatenate([kprev[copy], kcat[copy][BLOCK:2 * BLOCK, :]], axis=0))
            scores.append(lax.dot_general(lhs, kk, nt_dims, preferred_element_type=F32))
        s = jnp.concatenate(scores, axis=0).reshape(N_HEADS, BLOCK, 2 * BLOCK)
        s = jnp.where(mask[None], s, sink_fill)
        m = jnp.max(s, axis=-1, keepdims=True)
        p = jnp.exp2(s - m).astype(BF16)
        half_rows = (N_HEADS // 2) * BLOCK
        outs = []
        for copy in range(2):
            vv = (vcat[copy][i * BLOCK:(i + 2) * BLOCK, :] if i > 0 else
                  jnp.concatenate([vprev[copy], vcat[copy][BLOCK:2 * BLOCK, :]], axis=0))
            vv = jnp.where(sink_row, jnp.zeros((), BF16), vv)
            pv = jnp.dot(p[copy * (N_HEADS // 2):(copy + 1) * (N_HEADS // 2)].reshape(
                half_rows, 2 * BLOCK), vv,
                preferred_element_type=F32).reshape(N_HEADS // 2, BLOCK, 2 * D_KV)
            outs.append(pv[:, :, 0:D_KV] / pv[:, :, D_KV:])
        for c in range(nq):
            o = jnp.where(low == (same[c][1] == 0), outs[0][c], outs[1][c])
            zg = za[i * BLOCK:(i + 1) * BLOCK, 128 * c:128 * (c + 1)]
            c0 = D_CONV + D_POOL + 128 * c
            mixcat[i * BLOCK:(i + 1) * BLOCK, c0:c0 + 128] = (o * _silu(zg)).astype(BF16)

    a = proj(O_U, O_G) * _sigmoid(proj(O_G, O_ZC))
    acat[CARRY:CARRY + TM, :] = a
    zc = proj(O_ZC, O_PV)
    for r in range(1, SUBLANES):
        ashift[r - 1] = acat[r:r + TM + CARRY - SUBLANES, :]
    rows = 128

    def conv_chunk(c):
        acc = jnp.zeros((rows, D_CONV), F32)
        for j in range(CONV_WIDTH):
            off = CARRY - CONV_BUF + j
            r, lo = off % SUBLANES, c * rows + off - off % SUBLANES
            tap = acat[lo:lo + rows, :] if r == 0 else ashift[r - 1, lo:lo + rows, :]
            acc = acc + tap * convw_ref[0, j:j + 1, :]
        conv = acc + convb_ref[0]
        ya = _silu(_ln(conv) * cng_ref[0] + cnb_ref[0])
        mixcat[c * rows:(c + 1) * rows, 0:D_CONV] = (
            ya * _silu(zc[c * rows:(c + 1) * rows])).astype(BF16)

    pool_v = proj(O_PV, O_ZP)
    pcat[CARRY:CARRY + TM, :] = pool_v
    zp = proj(O_ZP, O_Q)

    for i in range(nb):
        attn_block(i)
    for c in range(TM // rows):
        conv_chunk(c)

    kcat[0][0:BLOCK, :] = k[TM - BLOCK:TM].astype(BF16)
    kcat[1][0:BLOCK, :] = _swap_halves(k[TM - BLOCK:TM]).astype(BF16)
    vcat[0][0:BLOCK, 0:D_KV] = v[TM - BLOCK:TM].astype(BF16)
    vcat[1][0:BLOCK, 0:D_KV] = _swap_halves(v[TM - BLOCK:TM]).astype(BF16)

    @pl.when(t == last)
    def _():
        convo_ref[0] = acat[CARRY + TM - CONV_BUF:CARRY + TM, :]

    acat[0:CARRY, :] = acat[TM:TM + CARRY, :]

    n = CARRY + TM
    ps_a[8:n, :] = pcat[8:n, :] + pcat[7:n - 1, :]
    ps_b[16:n, :] = ps_a[16:n, :] + ps_a[14:n - 2, :]
    s2 = ps_a[CARRY:n, :]
    s4 = ps_b[CARRY:n, :]
    ps_a[24:n, :] = ps_b[24:n, :] + ps_b[20:n - 4, :]
    s8 = ps_a[CARRY:n, :]
    s16 = s8 + ps_a[CARRY - 8:n - 8, :]
    lane = lax.broadcasted_iota(jnp.int32, (TM, D_POOL), 1)
    sums = jnp.where(lane < 64, s2, jnp.where(lane < 128, s4, jnp.where(lane < 192, s8, s16)))
    wlen = jnp.where(lane < 64, 2.0, jnp.where(lane < 128, 4.0, jnp.where(lane < 192, 8.0, 16.0)))
    pos1 = (lax.broadcasted_iota(jnp.int32, (TM, D_POOL), 0) + (t * TM + 1)).astype(F32)
    cnt = jnp.minimum(wlen, pos1)
    pooled = (sums / cnt - pool_v).astype(BF16)
    yb = jnp.dot(pooled, wpool_ref[0], preferred_element_type=F32) * pscale_ref[0]
    mixcat[:, D_CONV:D_CONV + D_POOL] = (yb * _silu(zp)).astype(BF16)

    @pl.when(t == last)
    def _():
        poolo_ref[0] = pcat[CARRY + TM - POOL_BUF:CARRY + TM, :]

    pcat[0:CARRY, :] = pcat[TM:TM + CARRY, :]

    g_res = (1.0 + gate) * (1.0 / ALPHA)
    piece = TM // OUT_PIECES
    for r0 in range(0, TM, piece):
        mix = jnp.dot(mixcat[r0:r0 + piece, :], wout_ref[0], preferred_element_type=F32)
        y = x[r0:r0 + piece] + g_res * mix
        xo_ref[0, r0:r0 + piece, :] = _ln(y, LN_EPS / ALPHA ** 2) * lng_ref[0] + lnb_ref[0]


def _prompt_layer(lidx, sinks_perm, x, mod_p, win, wout, conv_w, conv_b, cnorm_g, cnorm_b,
                  wpool, pool_scale, ln_g, ln_b, cos_t, sin_t):
    B, T, _ = x.shape
    nt = T // TM
    per_layer = lambda shape: pl.BlockSpec((1,) + shape, lambda b, t, l: (l[0],) + (0,) * len(shape))
    grid_spec = pltpu.PrefetchScalarGridSpec(
        num_scalar_prefetch=1,
        grid=(B, nt),
        in_specs=[
            per_layer((N_HEADS, 1, 2 * BLOCK)),
            pl.BlockSpec((1, TM, D_MODEL), lambda b, t, l: (b, t, 0)),
            pl.BlockSpec((1, 1, 3, D_MODEL), lambda b, t, l: (l[0], b, 0, 0)),
            per_layer((D_MODEL, D_IN)),
            per_layer((D_MODEL, D_MODEL)),
            per_layer((CONV_WIDTH, D_CONV)),
            per_layer((1, D_CONV)),
            per_layer((1, D_CONV)),
            per_layer((1, D_CONV)),
            per_layer((D_POOL, D_POOL)),
            per_layer((1, D_POOL)),
            per_layer((1, D_MODEL)),
            per_layer((1, D_MODEL)),
            pl.BlockSpec((TM, 128), lambda b, t, l: (t, 0)),
            pl.BlockSpec((TM, 128), lambda b, t, l: (t, 0)),
        ],
        out_specs=[
            pl.BlockSpec((1, TM, D_MODEL), lambda b, t, l: (b, t, 0)),
            pl.BlockSpec((1, CONV_BUF, D_CONV), lambda b, t, l: (b, 0, 0)),
            pl.BlockSpec((1, POOL_BUF, D_POOL), lambda b, t, l: (b, 0, 0)),
            pl.BlockSpec((1, WINDOW, D_KV), lambda b, t, l: (b, 0, 0)),
            pl.BlockSpec((1, WINDOW, D_KV), lambda b, t, l: (b, 0, 0)),
        ],
        scratch_shapes=[
            pltpu.VMEM((CARRY + TM, D_CONV), F32),
            pltpu.VMEM((SUBLANES - 1, CARRY + TM - SUBLANES, D_CONV), F32),
            pltpu.VMEM((CARRY + TM, D_POOL), F32),
            pltpu.VMEM((CARRY + TM, D_POOL), F32),
            pltpu.VMEM((CARRY + TM, D_POOL), F32),
            pltpu.VMEM((BLOCK + TM, D_KV), BF16),
            pltpu.VMEM((BLOCK + TM, D_KV), BF16),
            pltpu.VMEM((BLOCK + TM, 2 * D_KV), BF16),
            pltpu.VMEM((BLOCK + TM, 2 * D_KV), BF16),
            pltpu.VMEM((TM, D_MODEL), BF16),
        ],
    )
    out_shape = [
        jax.ShapeDtypeStruct((B, T, D_MODEL), F32),
        jax.ShapeDtypeStruct((B, CONV_BUF, D_CONV), F32),
        jax.ShapeDtypeStruct((B, POOL_BUF, D_POOL), F32),
        jax.ShapeDtypeStruct((B, WINDOW, D_KV), F32),
        jax.ShapeDtypeStruct((B, WINDOW, D_KV), F32),
    ]
    return pl.pallas_call(
        _prompt_kernel,
        grid_spec=grid_spec,
        out_shape=out_shape,
        compiler_params=pltpu.CompilerParams(
            dimension_semantics=("arbitrary", "arbitrary"), vmem_limit_bytes=VMEM_LIMIT),
    )(lidx, sinks_perm, x, mod_p, win, wout, conv_w, conv_b, cnorm_g, cnorm_b,
      wpool, pool_scale, ln_g, ln_b, cos_t, sin_t)


def _decode_kernel(l_ref,
                   x_ref, mod_ref, win_ref, wout_ref, convw_ref, convb_ref, cng_ref, cnb_ref,
                   wpool_ref, pscale_ref, lng_ref, lnb_ref, cos_ref, sin_ref, sinkrow_ref,
                   cconv_ref, cpool_ref, ck_ref, cv_ref, *rest):
    xo_ref, convo_ref, poolo_ref, ko_ref, vo_ref, qs, ocat, kn_s, vn_s, kbt, vbt, knt, vnt = rest[-13:]
    gb = DEC_GROUP
    nt = x_ref.shape[0]
    x3 = x_ref[...]
    shift = mod_ref[0, :, 0:D_MODEL]
    scale = mod_ref[0, :, D_MODEL:2 * D_MODEL]
    gate = mod_ref[0, :, 2 * D_MODEL:3 * D_MODEL]
    h3 = _ln(x3) * (1.0 + scale)[None] + shift[None]
    hb = h3.reshape(nt * gb, D_MODEL).astype(BF16)

    def proj(lo, hi):
        return jnp.dot(hb, win_ref[0, :, lo:hi], preferred_element_type=F32)

    a = proj(O_U, O_G) * _sigmoid(proj(O_G, O_ZC))
    convo_ref[0, 0:CONV_BUF - nt] = cconv_ref[0, nt:CONV_BUF]
    convo_ref[0, CONV_BUF - nt:CONV_BUF] = a.reshape(nt, gb, D_CONV)
    zc = proj(O_ZC, O_PV)
    ya_rows = []
    for tt in range(nt):
        acc = jnp.zeros((gb, D_CONV), F32)
        for j in range(CONV_WIDTH):
            i = tt + j
            xi = (cconv_ref[0, i] if i < CONV_BUF
                  else a[(i - CONV_BUF) * gb:(i - CONV_BUF + 1) * gb])
            acc = acc + xi * convw_ref[0, j:j + 1, :]
        conv = acc + convb_ref[0]
        ya_rows.append(_silu(_ln(conv) * cng_ref[0] + cnb_ref[0]))
    ya = jnp.concatenate(ya_rows, axis=0) * _silu(zc)

    pv = proj(O_PV, O_ZP)
    poolo_ref[0, 0:POOL_BUF - nt] = cpool_ref[0, nt:POOL_BUF]
    poolo_ref[0, POOL_BUF - nt:POOL_BUF] = pv.reshape(nt, gb, D_POOL)
    lane = lax.broadcasted_iota(jnp.int32, (gb, D_POOL), 1)
    wlen = jnp.where(lane < 64, 2, jnp.where(lane < 128, 4, jnp.where(lane < 192, 8, 16)))
    pooled_rows = []
    for tt in range(nt):
        acc = jnp.zeros((gb, D_POOL), F32)
        for d in range(max(POOL_WINDOWS)):
            i = POOL_BUF + tt - d
            xi = (cpool_ref[0, i] if i < POOL_BUF
                  else pv[(i - POOL_BUF) * gb:(i - POOL_BUF + 1) * gb])
            acc = acc + (xi if d < min(POOL_WINDOWS) else jnp.where(wlen > d, xi, 0.0))
        cnt = jnp.minimum(wlen, PAST_LEN + tt + 1).astype(F32)
        pooled_rows.append(acc / cnt - pv[tt * gb:(tt + 1) * gb])
    pooled = jnp.concatenate(pooled_rows, axis=0).astype(BF16)
    yb = jnp.dot(pooled, wpool_ref[0], preferred_element_type=F32) * pscale_ref[0]
    yb = yb * _silu(proj(O_ZP, O_Q))

    cos3 = cos_ref[...]
    sin3 = sin_ref[...]
    cos = jnp.broadcast_to(cos3, (nt, gb, 128)).reshape(nt * gb, 128)
    sin = jnp.broadcast_to(sin3, (nt, gb, 128)).reshape(nt * gb, 128)
    k = _rope(proj(O_K, O_V), cos, sin)
    v = proj(O_V, O_ZA)
    kn_s[...] = k
    vn_s[...] = v
    q = proj(O_Q, O_K) * (HEAD_DIM ** -0.5)
    for g, qg in enumerate(_group_heads([q[:, 128 * c:128 * (c + 1)] for c in range(GROUP)])):
        qs[:, 128 * g:128 * (g + 1)] = _rope(qg, cos, sin)

    for new_rows, by_batch, new_t in ((k, kbt, knt), (v, vbt, vnt)):
        for tt in range(nt):
            by_batch[pl.ds(tt, gb, stride=nt), :] = new_rows[tt * gb:(tt + 1) * gb]
        new_t[...] = by_batch[...].T
    per_blk = 128 // nt
    lane_kv = lax.broadcasted_iota(jnp.int32, (D_KV, WINDOW), 1)

    for blk in range(gb // per_blk):
        def roll_cache(j, carry, blk=blk):
            b = blk * per_blk + j
            shift = (WINDOW - nt - nt * j) & (WINDOW - 1)
            for new_t, cache_ref, out_ref in ((knt, ck_ref, ko_ref), (vnt, cv_ref, vo_ref)):
                old = pltpu.roll(cache_ref[0, b].reshape(D_KV, WINDOW), WINDOW - nt, axis=1)
                new = pltpu.roll(new_t[:, 128 * blk:128 * (blk + 1)], shift, axis=1)
                out_ref[0, b] = jnp.where(lane_kv >= WINDOW - nt, new, old).reshape(
                    N_KV_HEADS, HEAD_DIM, WINDOW)
            return carry
        lax.fori_loop(0, per_blk, roll_cache, 0, unroll=8)

    nrow = nt * GROUP * N_KV_HEADS * SUB
    r1 = lax.broadcasted_iota(jnp.int32, (nrow, SUB * WINDOW), 0)
    c1 = lax.broadcasted_iota(jnp.int32, (nrow, SUB * WINDOW), 1)
    mask1 = ((c1 >> 7) == (r1 & (SUB - 1))) & ((c1 & (WINDOW - 1)) > (r1 >> 6))
    r2 = lax.broadcasted_iota(jnp.int32, (nrow, 128), 0)
    c2 = lax.broadcasted_iota(jnp.int32, (nrow, 128), 1)
    mask2 = ((c2 < nt * SUB) & ((c2 & (SUB - 1)) == (r2 & (SUB - 1)))
             & ((c2 >> 3) <= (r2 >> 6)))
    lane_q = lax.broadcasted_iota(jnp.int32, (SUB, 128), 1)
    low = lane_q < HEAD_DIM
    fill1 = jnp.where(c1 == 0, sinkrow_ref[0], MASK_VALUE)
    lane_c = lax.broadcasted_iota(jnp.int32, (D_KV, WINDOW), 1)
    ones_c = jnp.ones((D_KV, SUB * WINDOW), BF16)
    ones_n = jnp.ones((128, D_KV), BF16)
    zeros_n = jnp.zeros((128 - nt * SUB, D_KV), F32)

    def sub_block(sb, carry):
        b0 = pl.multiple_of(sb * SUB, SUB)
        pieces = []
        for tt in range(nt):
            for g in range(GROUP):
                qp = qs[pl.ds(tt * gb + b0, SUB), 128 * g:128 * (g + 1)]
                pieces.append(jnp.where(low, qp, 0.0))
                pieces.append(jnp.where(low, 0.0, qp))
        lhs = jnp.concatenate(pieces, axis=0).astype(BF16)
        kblk = ck_ref[0, pl.ds(b0, SUB)]
        vblk = cv_ref[0, pl.ds(b0, SUB)]
        kc = jnp.concatenate(
            [kblk[bb].reshape(D_KV, WINDOW) for bb in range(SUB)], axis=1).astype(BF16)
        vc = jnp.concatenate(
            [jnp.where(lane_c == 0, 0.0, vblk[0].reshape(D_KV, WINDOW))]
            + [vblk[bb].reshape(D_KV, WINDOW) for bb in range(1, SUB)], axis=1).astype(BF16)
        kn = jnp.concatenate(
            [kn_s[pl.ds(tt * gb + b0, SUB), :] for tt in range(nt)] + [zeros_n],
            axis=0).astype(BF16)
        vn = jnp.concatenate(
            [vn_s[pl.ds(tt * gb + b0, SUB), :] for tt in range(nt)] + [zeros_n],
            axis=0).astype(BF16)
        nt_dims = (((1,), (1,)), ((), ()))
        s1 = jnp.where(mask1, jnp.dot(lhs, kc, preferred_element_type=F32), fill1)
        s2 = jnp.where(mask2, lax.dot_general(lhs, kn, nt_dims, preferred_element_type=F32),
                       MASK_VALUE)
        s = jnp.concatenate([s1, s2], axis=1)
        p = jnp.exp(s - jnp.max(s, axis=-1, keepdims=True)).astype(BF16)
        tot = (lax.dot_general(p[:, 0:SUB * WINDOW], jnp.concatenate([vc, ones_c], axis=0),
                               nt_dims, preferred_element_type=F32)
               + jnp.dot(p[:, SUB * WINDOW:], jnp.concatenate([vn, ones_n], axis=1),
                         preferred_element_type=F32))
        o_all = tot[:, 0:D_KV] / tot[:, D_KV:]
        for tt in range(nt):
            for g in range(GROUP):
                r0 = ((tt * GROUP + g) * N_KV_HEADS) * SUB
                o = jnp.where(low, o_all[r0:r0 + SUB], o_all[r0 + SUB:r0 + 2 * SUB])
                ocat[pl.ds(tt * gb + b0, SUB), 128 * g:128 * (g + 1)] = o
        return carry

    lax.fori_loop(0, gb // SUB, sub_block, 0, unroll=4)
    o_nat = _ungroup_heads([ocat[:, 128 * g:128 * (g + 1)] for g in range(GROUP)])
    yc = jnp.concatenate(o_nat, axis=-1) * _silu(proj(O_ZA, D_IN))

    mixcat = jnp.concatenate([ya, yb, yc], axis=-1).astype(BF16)
    mix = jnp.dot(mixcat, wout_ref[0], preferred_element_type=F32).reshape(nt, gb, D_MODEL)
    y = ALPHA * x3 + (1.0 + gate)[None] * mix
    xo_ref[...] = _ln(y) * lng_ref[0][None] + lnb_ref[0][None]


def _decode_layer(lidx, x, mod_s, win, wout, conv_w, conv_b, cnorm_g, cnorm_b, wpool, pool_scale,
                  ln_g, ln_b, cos_s, sin_s, sink_rows, cache_conv, cache_pool, cache_k, cache_v,
                  rolled):
    nt, nbatch, _ = x.shape
    gb = DEC_GROUP
    carried = list(rolled)
    per_layer = lambda shape: pl.BlockSpec((1,) + shape, lambda g, l: (l[0],) + (0,) * len(shape))
    tok = lambda c: pl.BlockSpec((nt, gb, c), lambda g, l: (0, g, 0))
    time_major = lambda r, c: pl.BlockSpec((1, r, gb, c), lambda g, l: (l[0], 0, g, 0))
    kv_blk = pl.BlockSpec((1, gb, N_KV_HEADS, HEAD_DIM, WINDOW), lambda g, l: (l[0], g, 0, 0, 0))
    nrow = nt * GROUP * N_KV_HEADS * SUB
    grid_spec = pltpu.PrefetchScalarGridSpec(
        num_scalar_prefetch=1,
        grid=(nbatch // gb,),
        in_specs=[
            tok(D_MODEL),
            pl.BlockSpec((1, gb, 3 * D_MODEL), lambda g, l: (l[0], g, 0)),
            per_layer((D_MODEL, D_IN)),
            per_layer((D_MODEL, D_MODEL)),
            per_layer((CONV_WIDTH, D_CONV)),
            per_layer((1, D_CONV)),
            per_layer((1, D_CONV)),
            per_layer((1, D_CONV)),
            per_layer((D_POOL, D_POOL)),
            per_layer((1, D_POOL)),
            per_layer((1, D_MODEL)),
            per_layer((1, D_MODEL)),
            pl.BlockSpec((nt, 1, 128), lambda g, l: (0, 0, 0)),
            pl.BlockSpec((nt, 1, 128), lambda g, l: (0, 0, 0)),
            per_layer((nrow, 1)),
            time_major(CONV_BUF, D_CONV),
            time_major(POOL_BUF, D_POOL),
            kv_blk,
            kv_blk,
        ] + [pl.BlockSpec(memory_space=pl.ANY)] * len(carried),
        out_specs=[tok(D_MODEL), time_major(CONV_BUF, D_CONV), time_major(POOL_BUF, D_POOL),
                   kv_blk, kv_blk],
        scratch_shapes=[
            pltpu.VMEM((nt * gb, D_ATTN), F32),
            pltpu.VMEM((nt * gb, D_ATTN), F32),
            pltpu.VMEM((nt * gb, D_KV), F32),
            pltpu.VMEM((nt * gb, D_KV), F32),
            pltpu.VMEM((nt * gb, D_KV), F32),
            pltpu.VMEM((nt * gb, D_KV), F32),
            pltpu.VMEM((D_KV, nt * gb), F32),
            pltpu.VMEM((D_KV, nt * gb), F32),
        ],
    )
    out_shape = [jax.ShapeDtypeStruct((nt, nbatch, D_MODEL), F32)] + [
        jax.ShapeDtypeStruct(c.shape, F32) for c in (cache_conv, cache_pool, cache_k, cache_v)]
    operands = (lidx, x, mod_s, win, wout, conv_w, conv_b, cnorm_g, cnorm_b, wpool, pool_scale,
                ln_g, ln_b, cos_s, sin_s, sink_rows, cache_conv, cache_pool, cache_k, cache_v)
    n_in = len(operands)
    return pl.pallas_call(
        _decode_kernel,
        grid_spec=grid_spec,
        out_shape=out_shape,
        input_output_aliases={n_in + i: 1 + i for i in range(len(carried))},
        compiler_params=pltpu.CompilerParams(
            dimension_semantics=("arbitrary",), vmem_limit_bytes=VMEM_LIMIT),
    )(*operands, *carried)


def _rope_tables(pos):
    half = HEAD_DIM // 2
    inv_freq = ROPE_THETA ** (-jnp.arange(half, dtype=F32) * (2.0 / HEAD_DIM))
    ang = pos.astype(F32)[:, None] * inv_freq[None, :]
    cos = jnp.cos(ang)
    sin = jnp.sin(ang)
    cos_t = jnp.concatenate([cos, cos, cos, cos], axis=-1)
    sin_t = jnp.concatenate([-sin, sin, -sin, sin], axis=-1)
    return cos_t, sin_t


def kernel(x_prompt, x_sample, cache_conv, cache_pool, cache_k, cache_v, c_prompt, c_sample,
           w_in, w_out, conv_w, conv_b, cnorm_g, cnorm_b, pool_w, pool_scale, sinks,
           w_mod, b_mod, ln_g, ln_b):
    B, T, _ = x_prompt.shape
    nbatch, nt, _ = x_sample.shape

    win = w_in.astype(BF16)
    wout = w_out.astype(BF16)
    wpool = jnp.zeros((DEPTH, D_POOL, D_POOL), F32)
    for gi in range(len(POOL_WINDOWS)):
        sl = slice(gi * POOL_GROUP_DIM, (gi + 1) * POOL_GROUP_DIM)
        wpool = wpool.at[:, sl, sl].set(pool_w[:, gi])
    wpool = wpool.astype(BF16)
    sinks_gh = sinks.reshape(DEPTH, N_KV_HEADS, GROUP).transpose(0, 2, 1)
    nq = D_ATTN // 128
    order = ([2 * c + (2 * c) // GROUP for c in range(nq)]
             + [2 * c + 1 - (2 * c) // GROUP for c in range(nq)])
    sinks_ord = jnp.stack([sinks[:, hd] for hd in order], axis=1)
    sinks_flat = jnp.concatenate(
        [sinks_ord[:, :, None, None] * LOG2E,
         jnp.full((DEPTH, N_HEADS, 1, 2 * BLOCK - 1), MASK_VALUE, F32)], axis=-1)
    sink_rows = jnp.broadcast_to(
        sinks_gh[:, None, :, :, None], (DEPTH, nt, GROUP, N_KV_HEADS, SUB)
    ).reshape(DEPTH, nt * GROUP * N_KV_HEADS * SUB, 1)

    r3 = lambda p: p.reshape(DEPTH, 1, -1)
    conv_b3, cng3, cnb3, pscale3, lng3, lnb3 = map(r3, (conv_b, cnorm_g, cnorm_b, pool_scale, ln_g, ln_b))

    cos_p, sin_p = _rope_tables(jnp.arange(T, dtype=jnp.int32))
    cos_s, sin_s = _rope_tables(PAST_LEN + jnp.arange(nt, dtype=jnp.int32))
    cos_s = cos_s.reshape(nt, 1, 128)
    sin_s = sin_s.reshape(nt, 1, 128)

    mod = _modulation(jnp.concatenate([c_sample, c_prompt], axis=0), w_mod, b_mod)
    mod_p = mod[:, nbatch:].reshape(DEPTH, B, 3, D_MODEL)

    cconv_t = cache_conv.transpose(0, 2, 1, 3)
    cpool_t = cache_pool.transpose(0, 2, 1, 3)
    ck_t = cache_k.transpose(0, 1, 3, 4, 2)
    cv_t = cache_v.transpose(0, 1, 3, 4, 2)

    xp = x_prompt
    xs = x_sample.transpose(1, 0, 2)
    conv_p, pool_p, k_p, v_p = ([] for _ in range(4))
    rolled = ()
    for l in range(DEPTH):
        lidx = jnp.full((1,), l, jnp.int32)
        xp, cp, pp, kp, vp = _prompt_layer(
            lidx, sinks_flat, xp, mod_p, win, wout, conv_w, conv_b3, cng3, cnb3, wpool, pscale3,
            lng3, lnb3, cos_p, sin_p)
        xs, *rolled = _decode_layer(
            lidx, xs, mod, win, wout, conv_w, conv_b3, cng3, cnb3, wpool, pscale3,
            lng3, lnb3, cos_s, sin_s, sink_rows, cconv_t, cpool_t, ck_t, cv_t, rolled)
        conv_p.append(cp); pool_p.append(pp); k_p.append(kp); v_p.append(vp)

    kv5 = lambda z: z.reshape(z.shape[:-1] + (N_KV_HEADS, HEAD_DIM))
    conv_s, pool_s, k_s, v_s = rolled
    return (xp, xs.transpose(1, 0, 2),
            jnp.stack(conv_p), jnp.stack(pool_p), kv5(jnp.stack(k_p)), kv5(jnp.stack(v_p)),
            conv_s.transpose(0, 2, 1, 3), pool_s.transpose(0, 2, 1, 3),
            k_s.transpose(0, 1, 4, 2, 3), v_s.transpose(0, 1, 4, 2, 3))
```

```python
import functools

import jax
import jax.numpy as jnp
from jax import lax
from jax.experimental import pallas as pl
from jax.experimental.pallas import tpu as pltpu

F32 = jnp.float32
BF16 = jnp.bfloat16

D_MODEL = 1024
DEPTH = 4
D_CONV = 256
D_POOL = 256
D_ATTN = 512
HEAD_DIM = 64
N_HEADS = 8
N_KV_HEADS = 2
GROUP = 4
D_KV = 128
WINDOW = 128
BLOCK = 128
CONV_WIDTH = 31
CONV_BUF = 30
POOL_WINDOWS = (2, 4, 8, 16)
POOL_GROUP_DIM = 64
POOL_BUF = 15
ROPE_THETA = 10000.0
LN_EPS = 1e-5
ALPHA = (2.0 * DEPTH) ** 0.25
LOG2E = 1.4426950408889634
MASK_VALUE = -1e30
PAST_LEN = 8192

O_U, O_G, O_ZC, O_PV, O_ZP, O_Q, O_K, O_V, O_ZA, D_IN = (
    0, 256, 512, 768, 1024, 1280, 1792, 1920, 2048, 2560)

SUBLANES = 8
CARRY = 32
TM = 1024
ROW_PIECES = 4
OUT_PIECES = 4
DEC_GROUP = 32
SUB = 8
VMEM_LIMIT = 48 * 1024 * 1024


def _sigmoid(x):
    return 1.0 / (1.0 + jnp.exp(-x))


def _silu(x):
    return x * _sigmoid(x)


def _ln(x, eps=LN_EPS):
    mu = jnp.mean(x, axis=-1, keepdims=True)
    xc = x - mu
    var = jnp.mean(xc * xc, axis=-1, keepdims=True)
    return xc * lax.rsqrt(var + eps)


def _rope(x, cos, sin_signed):
    lane = lax.broadcasted_iota(jnp.int32, x.shape, 1)
    first_half = (lane & 63) < 32
    rot = jnp.where(first_half, pltpu.roll(x, 96, axis=1), pltpu.roll(x, 32, axis=1))
    return x * cos + rot * sin_signed


def _swap_halves(x):
    return pltpu.roll(x, HEAD_DIM, axis=1)


def _group_heads(nat):
    low = lax.broadcasted_iota(jnp.int32, nat[0].shape, 1) < HEAD_DIM
    out = []
    for g in range(GROUP):
        a, b = nat[g // 2], nat[GROUP // 2 + g // 2]
        out.append(jnp.where(low, a, _swap_halves(b)) if g % 2 == 0
                   else jnp.where(low, _swap_halves(a), b))
    return out


def _ungroup_heads(grouped):
    low = lax.broadcasted_iota(jnp.int32, grouped[0].shape, 1) < HEAD_DIM
    nat = [None] * GROUP
    for c in range(GROUP // 2):
        even, odd = grouped[2 * c], grouped[2 * c + 1]
        nat[c] = jnp.where(low, even, _swap_halves(odd))
        nat[GROUP // 2 + c] = jnp.where(low, _swap_halves(even), odd)
    return nat


def _mod_kernel(c_ref, w_ref, b_ref, o_ref):
    c = c_ref[...]
    sc = _silu(c).astype(BF16)
    o_ref[0] = jnp.dot(sc, w_ref[0].astype(BF16), preferred_element_type=F32) + b_ref[0]


def _modulation(c_all, w_mod, b_mod):
    n = c_all.shape[0]
    tn = 1536
    return pl.pallas_call(
        _mod_kernel,
        grid=(DEPTH, 3 * D_MODEL // tn),
        in_specs=[
            pl.BlockSpec((n, D_MODEL), lambda l, j: (0, 0)),
            pl.BlockSpec((1, D_MODEL, tn), lambda l, j: (l, 0, j)),
            pl.BlockSpec((1, 1, tn), lambda l, j: (l, 0, j)),
        ],
        out_specs=pl.BlockSpec((1, n, tn), lambda l, j: (l, 0, j)),
        out_shape=jax.ShapeDtypeStruct((DEPTH, n, 3 * D_MODEL), F32),
        compiler_params=pltpu.CompilerParams(
            dimension_semantics=("arbitrary", "arbitrary"), vmem_limit_bytes=VMEM_LIMIT),
    )(c_all, w_mod, b_mod.reshape(DEPTH, 1, 3 * D_MODEL))


def _prompt_kernel(l_ref, sink_ref,
                   x_ref, mod_ref, win_ref, wout_ref, convw_ref, convb_ref, cng_ref, cnb_ref,
                   wpool_ref, pscale_ref, lng_ref, lnb_ref, cos_ref, sin_ref,
                   xo_ref, convo_ref, poolo_ref, ko_ref, vo_ref,
                   acat, ashift, pcat, ps_a, ps_b, kcat0, kcat1, vcat0, vcat1, mixcat):
    del l_ref
    kcat, vcat = (kcat0, kcat1), (vcat0, vcat1)
    t = pl.program_id(1)
    last = pl.num_programs(1) - 1
    nb = TM // BLOCK

    @pl.when(t == 0)
    def _():
        acat[0:CARRY, :] = jnp.zeros((CARRY, D_CONV), F32)
        pcat[0:CARRY, :] = jnp.zeros((CARRY, D_POOL), F32)
        for kc, vc in zip(kcat, vcat):
            kc[0:BLOCK, :] = jnp.zeros((BLOCK, D_KV), BF16)
            vc[0:BLOCK, 0:D_KV] = jnp.zeros((BLOCK, D_KV), BF16)
            vc[:, D_KV:2 * D_KV] = jnp.ones((BLOCK + TM, D_KV), BF16)

    kprev = [kc[0:BLOCK, :] for kc in kcat]
    vprev = [vc[0:BLOCK, :] for vc in vcat]
    x = x_ref[0]
    shift = mod_ref[0, 0, 0:1, :]
    scale = mod_ref[0, 0, 1:2, :]
    gate = mod_ref[0, 0, 2:3, :]
    piece = TM // ROW_PIECES
    nq = D_ATTN // 128
    cos = cos_ref[...]
    sin = sin_ref[...]
    hbs, ks, vs, zas, qrs = [], [], [], [], []
    for r0 in range(0, TM, piece):
        hbs.append((_ln(x[r0:r0 + piece]) * (1.0 + scale) + shift).astype(BF16))
        pr = jnp.dot(hbs[-1], win_ref[0, :, O_Q:D_IN], preferred_element_type=F32)
        cos_p, sin_p = cos[r0:r0 + piece], sin[r0:r0 + piece]
        k_p = _rope(pr[:, O_K - O_Q:O_V - O_Q], cos_p, sin_p)
        v_p = pr[:, O_V - O_Q:O_ZA - O_Q]
        kcat[0][BLOCK + r0:BLOCK + r0 + piece, :] = k_p.astype(BF16)
        kcat[1][BLOCK + r0:BLOCK + r0 + piece, :] = _swap_halves(k_p).astype(BF16)
        vcat[0][BLOCK + r0:BLOCK + r0 + piece, 0:D_KV] = v_p.astype(BF16)
        vcat[1][BLOCK + r0:BLOCK + r0 + piece, 0:D_KV] = _swap_halves(v_p).astype(BF16)
        q_p = pr[:, 0:D_ATTN] * (HEAD_DIM ** -0.5 * LOG2E)
        qrs.append([_rope(q_p[:, 128 * c:128 * (c + 1)], cos_p, sin_p) for c in range(nq)])
        ks.append(k_p)
        vs.append(v_p)
        zas.append(pr[:, O_ZA - O_Q:D_IN - O_Q])
    hb = jnp.concatenate(hbs, axis=0)
    k = jnp.concatenate(ks, axis=0)
    v = jnp.concatenate(vs, axis=0)
    za = jnp.concatenate(zas, axis=0)
    qr = [jnp.concatenate([qp[c] for qp in qrs], axis=0) for c in range(nq)]

    def proj(lo, hi):
        return jnp.dot(hb, win_ref[0, :, lo:hi], preferred_element_type=F32)


    @pl.when(t == last)
    def _():
        ko_ref[0] = k[TM - WINDOW:TM, :]
        vo_ref[0] = v[TM - WINDOW:TM, :]

    row = lax.broadcasted_iota(jnp.int32, (BLOCK, 2 * BLOCK), 0)
    col = lax.broadcasted_iota(jnp.int32, (BLOCK, 2 * BLOCK), 1)
    rel = col - BLOCK - row
    band = (rel <= 0) & (rel > -WINDOW)
    band_first = band & (col + jnp.minimum(t, 1) * BLOCK >= BLOCK)
    lane_q = lax.broadcasted_iota(jnp.int32, (BLOCK, 2 * HEAD_DIM), 1)
    low = lane_q < HEAD_DIM

    same = [(c, (2 * c) // GROUP) for c in range(nq)]
    other = [(c, 1 - (2 * c) // GROUP) for c in range(nq)]
    sink_fill = sink_ref[0]
    vrow = lax.broadcasted_iota(jnp.int32, (2 * BLOCK, 2 * D_KV), 0)
    vcol = lax.broadcasted_iota(jnp.int32, (2 * BLOCK, 2 * D_KV), 1)
    sink_row = (vrow == 0) & (vcol < D_KV)
    nt_dims = (((1,), (1,)), ((), ()))
    def attn_block(i):
        mask = band if i > 0 else band_first
        scores = []
        for copy, heads in ((0, same), (1, other)):
            lhs = jnp.concatenate(
                [jnp.where(low == (half == 0), qr[c][i * BLOCK:(i + 1) * BLOCK], 0.0)
                 for c, half in heads], axis=0).astype(BF16)
            kk = (kcat[copy][i * BLOCK:(i + 2) * BLOCK, :] if i > 0 else
                  jnp.concatenate([kprev[copy], kcat[copy][BLOCK:2 * BLOCK, :]], axis=0))
            scores.append(lax.dot_general(lhs, kk, nt_dims, preferred_element_type=F32))
        s = jnp.concatenate(scores, axis=0).reshape(N_HEADS, BLOCK, 2 * BLOCK)
        s = jnp.where(mask[None], s, sink_fill)
        m = jnp.max(s, axis=-1, keepdims=True)
        p = jnp.exp2(s - m).astype(BF16)
        half_rows = (N_HEADS // 2) * BLOCK
        outs = []
        for copy in range(2):
            vv = (vcat[copy][i * BLOCK:(i + 2) * BLOCK, :] if i > 0 else
                  jnp.concatenate([vprev[copy], vcat[copy][BLOCK:2 * BLOCK, :]], axis=0))
            vv = jnp.where(sink_row, jnp.zeros((), BF16), vv)
            pv = jnp.dot(p[copy * (N_HEADS // 2):(copy + 1) * (N_HEADS // 2)].reshape(
                half_rows, 2 * BLOCK), vv,
                preferred_element_type=F32).reshape(N_HEADS // 2, BLOCK, 2 * D_KV)
            outs.append(pv[:, :, 0:D_KV] / pv[:, :, D_KV:])
        for c in range(nq):
            o = jnp.where(low == (same[c][1] == 0), outs[0][c], outs[1][c])
            zg = za[i * BLOCK:(i + 1) * BLOCK, 128 * c:128 * (c + 1)]
            c0 = D_CONV + D_POOL + 128 * c
            mixcat[i * BLOCK:(i + 1) * BLOCK, c0:c0 + 128] = (o * _silu(zg)).astype(BF16)

    a = proj(O_U, O_G) * _sigmoid(proj(O_G, O_ZC))
    acat[CARRY:CARRY + TM, :] = a
    zc = proj(O_ZC, O_PV)
    for r in range(1, SUBLANES):
        ashift[r - 1] = acat[r:r + TM + CARRY - SUBLANES, :]
    rows = 128

    def conv_chunk(c):
        acc = jnp.zeros((rows, D_CONV), F32)
        for j in range(CONV_WIDTH):
            off = CARRY - CONV_BUF + j
            r, lo = off % SUBLANES, c * rows + off - off % SUBLANES
            tap = acat[lo:lo + rows, :] if r == 0 else ashift[r - 1, lo:lo + rows, :]
            acc = acc + tap * convw_ref[0, j:j + 1, :]
        conv = acc + convb_ref[0]
        ya = _silu(_ln(conv) * cng_ref[0] + cnb_ref[0])
        mixcat[c * rows:(c + 1) * rows, 0:D_CONV] = (
            ya * _silu(zc[c * rows:(c + 1) * rows])).astype(BF16)

    pool_v = proj(O_PV, O_ZP)
    pcat[CARRY:CARRY + TM, :] = pool_v
    zp = proj(O_ZP, O_Q)

    for i in range(nb):
        attn_block(i)
    for c in range(TM // rows):
        conv_chunk(c)

    kcat[0][0:BLOCK, :] = k[TM - BLOCK:TM].astype(BF16)
    kcat[1][0:BLOCK, :] = _swap_halves(k[TM - BLOCK:TM]).astype(BF16)
    vcat[0][0:BLOCK, 0:D_KV] = v[TM - BLOCK:TM].astype(BF16)
    vcat[1][0:BLOCK, 0:D_KV] = _swap_halves(v[TM - BLOCK:TM]).astype(BF16)

    @pl.when(t == last)
    def _():
        convo_ref[0] = acat[CARRY + TM - CONV_BUF:CARRY + TM, :]

    acat[0:CARRY, :] = acat[TM:TM + CARRY, :]

    n = CARRY + TM
    ps_a[8:n, :] = pcat[8:n, :] + pcat[7:n - 1, :]
    ps_b[16:n, :] = ps_a[16:n, :] + ps_a[14:n - 2, :]
    s2 = ps_a[CARRY:n, :]
    s4 = ps_b[CARRY:n, :]
    ps_a[24:n, :] = ps_b[24:n, :] + ps_b[20:n - 4, :]
    s8 = ps_a[CARRY:n, :]
    s16 = s8 + ps_a[CARRY - 8:n - 8, :]
    lane = lax.broadcasted_iota(jnp.int32, (TM, D_POOL), 1)
    sums = jnp.where(lane < 64, s2, jnp.where(lane < 128, s4, jnp.where(lane < 192, s8, s16)))
    wlen = jnp.where(lane < 64, 2.0, jnp.where(lane < 128, 4.0, jnp.where(lane < 192, 8.0, 16.0)))
    pos1 = (lax.broadcasted_iota(jnp.int32, (TM, D_POOL), 0) + (t * TM + 1)).astype(F32)
    cnt = jnp.minimum(wlen, pos1)
    pooled = (sums / cnt - pool_v).astype(BF16)
    yb = jnp.dot(pooled, wpool_ref[0], preferred_element_type=F32) * pscale_ref[0]
    mixcat[:, D_CONV:D_CONV + D_POOL] = (yb * _silu(zp)).astype(BF16)

    @pl.when(t == last)
    def _():
        poolo_ref[0] = pcat[CARRY + TM - POOL_BUF:CARRY + TM, :]

    pcat[0:CARRY, :] = pcat[TM:TM + CARRY, :]

    g_res = (1.0 + gate) * (1.0 / ALPHA)
    piece = TM // OUT_PIECES
    for r0 in range(0, TM, piece):
        mix = jnp.dot(mixcat[r0:r0 + piece, :], wout_ref[0], preferred_element_type=F32)
        y = x[r0:r0 + piece] + g_res * mix
        xo_ref[0, r0:r0 + piece, :] = _ln(y, LN_EPS / ALPHA ** 2) * lng_ref[0] + lnb_ref[0]


def _prompt_layer(lidx, sinks_perm, x, mod_p, win, wout, conv_w, conv_b, cnorm_g, cnorm_b,
                  wpool, pool_scale, ln_g, ln_b, cos_t, sin_t):
    B, T, _ = x.shape
    nt = T // TM
    per_layer = lambda shape: pl.BlockSpec((1,) + shape, lambda b, t, l: (l[0],) + (0,) * len(shape))
    grid_spec = pltpu.PrefetchScalarGridSpec(
        num_scalar_prefetch=1,
        grid=(B, nt),
        in_specs=[
            per_layer((N_HEADS, 1, 2 * BLOCK)),
            pl.BlockSpec((1, TM, D_MODEL), lambda b, t, l: (b, t, 0)),
            pl.BlockSpec((1, 1, 3, D_MODEL), lambda b, t, l: (l[0], b, 0, 0)),
            per_layer((D_MODEL, D_IN)),
            per_layer((D_MODEL, D_MODEL)),
            per_layer((CONV_WIDTH, D_CONV)),
            per_layer((1, D_CONV)),
            per_layer((1, D_CONV)),
            per_layer((1, D_CONV)),
            per_layer((D_POOL, D_POOL)),
            per_layer((1, D_POOL)),
            per_layer((1, D_MODEL)),
            per_layer((1, D_MODEL)),
            pl.BlockSpec((TM, 128), lambda b, t, l: (t, 0)),
            pl.BlockSpec((TM, 128), lambda b, t, l: (t, 0)),
        ],
        out_specs=[
            pl.BlockSpec((1, TM, D_MODEL), lambda b, t, l: (b, t, 0)),
            pl.BlockSpec((1, CONV_BUF, D_CONV), lambda b, t, l: (b, 0, 0)),
            pl.BlockSpec((1, POOL_BUF, D_POOL), lambda b, t, l: (b, 0, 0)),
            pl.BlockSpec((1, WINDOW, D_KV), lambda b, t, l: (b, 0, 0)),
            pl.BlockSpec((1, WINDOW, D_KV), lambda b, t, l: (b, 0, 0)),
        ],
        scratch_shapes=[
            pltpu.VMEM((CARRY + TM, D_CONV), F32),
            pltpu.VMEM((SUBLANES - 1, CARRY + TM - SUBLANES, D_CONV), F32),
            pltpu.VMEM((CARRY + TM, D_POOL), F32),
            pltpu.VMEM((CARRY + TM, D_POOL), F32),
            pltpu.VMEM((CARRY + TM, D_POOL), F32),
            pltpu.VMEM((BLOCK + TM, D_KV), BF16),
            pltpu.VMEM((BLOCK + TM, D_KV), BF16),
            pltpu.VMEM((BLOCK + TM, 2 * D_KV), BF16),
            pltpu.VMEM((BLOCK + TM, 2 * D_KV), BF16),
            pltpu.VMEM((TM, D_MODEL), BF16),
        ],
    )
    out_shape = [
        jax.ShapeDtypeStruct((B, T, D_MODEL), F32),
        jax.ShapeDtypeStruct((B, CONV_BUF, D_CONV), F32),
        jax.ShapeDtypeStruct((B, POOL_BUF, D_POOL), F32),
        jax.ShapeDtypeStruct((B, WINDOW, D_KV), F32),
        jax.ShapeDtypeStruct((B, WINDOW, D_KV), F32),
    ]
    return pl.pallas_call(
        _prompt_kernel,
        grid_spec=grid_spec,
        out_shape=out_shape,
        compiler_params=pltpu.CompilerParams(
            dimension_semantics=("arbitrary", "arbitrary"), vmem_limit_bytes=VMEM_LIMIT),
    )(lidx, sinks_perm, x, mod_p, win, wout, conv_w, conv_b, cnorm_g, cnorm_b,
      wpool, pool_scale, ln_g, ln_b, cos_t, sin_t)


def _decode_kernel(l_ref,
                   x_ref, mod_ref, win_ref, wout_ref, convw_ref, convb_ref, cng_ref, cnb_ref,
                   wpool_ref, pscale_ref, lng_ref, lnb_ref, cos_ref, sin_ref, sinkrow_ref,
                   cconv_ref, cpool_ref, ck_ref, cv_ref, *rest):
    xo_ref, convo_ref, poolo_ref, ko_ref, vo_ref, qs, ocat, kn_s, vn_s, kbt, vbt, knt, vnt = rest[-13:]
    gb = DEC_GROUP
    nt = x_ref.shape[0]
    x3 = x_ref[...]
    shift = mod_ref[0, :, 0:D_MODEL]
    scale = mod_ref[0, :, D_MODEL:2 * D_MODEL]
    gate = mod_ref[0, :, 2 * D_MODEL:3 * D_MODEL]
    h3 = _ln(x3) * (1.0 + scale)[None] + shift[None]
    hb = h3.reshape(nt * gb, D_MODEL).astype(BF16)

    def proj(lo, hi):
        return jnp.dot(hb, win_ref[0, :, lo:hi], preferred_element_type=F32)

    a = proj(O_U, O_G) * _sigmoid(proj(O_G, O_ZC))
    convo_ref[0, 0:CONV_BUF - nt] = cconv_ref[0, nt:CONV_BUF]
    convo_ref[0, CONV_BUF - nt:CONV_BUF] = a.reshape(nt, gb, D_CONV)
    zc = proj(O_ZC, O_PV)
    ya_rows = []
    for tt in range(nt):
        acc = jnp.zeros((gb, D_CONV), F32)
        for j in range(CONV_WIDTH):
            i = tt + j
            xi = (cconv_ref[0, i] if i < CONV_BUF
                  else a[(i - CONV_BUF) * gb:(i - CONV_BUF + 1) * gb])
            acc = acc + xi * convw_ref[0, j:j + 1, :]
        conv = acc + convb_ref[0]
        ya_rows.append(_silu(_ln(conv) * cng_ref[0] + cnb_ref[0]))
    ya = jnp.concatenate(ya_rows, axis=0) * _silu(zc)

    pv = proj(O_PV, O_ZP)
    poolo_ref[0, 0:POOL_BUF - nt] = cpool_ref[0, nt:POOL_BUF]
    poolo_ref[0, POOL_BUF - nt:POOL_BUF] = pv.reshape(nt, gb, D_POOL)
    lane = lax.broadcasted_iota(jnp.int32, (gb, D_POOL), 1)
    wlen = jnp.where(lane < 64, 2, jnp.where(lane < 128, 4, jnp.where(lane < 192, 8, 16)))
    pooled_rows = []
    for tt in range(nt):
        acc = jnp.zeros((gb, D_POOL), F32)
        for d in range(max(POOL_WINDOWS)):
            i = POOL_BUF + tt - d
            xi = (cpool_ref[0, i] if i < POOL_BUF
                  else pv[(i - POOL_BUF) * gb:(i - POOL_BUF + 1) * gb])
            acc = acc + (xi if d < min(POOL_WINDOWS) else jnp.where(wlen > d, xi, 0.0))
        cnt = jnp.minimum(wlen, PAST_LEN + tt + 1).astype(F32)
        pooled_rows.append(acc / cnt - pv[tt * gb:(tt + 1) * gb])
    pooled = jnp.concatenate(pooled_rows, axis=0).astype(BF16)
    yb = jnp.dot(pooled, wpool_ref[0], preferred_element_type=F32) * pscale_ref[0]
    yb = yb * _silu(proj(O_ZP, O_Q))

    cos3 = cos_ref[...]
    sin3 = sin_ref[...]
    cos = jnp.broadcast_to(cos3, (nt, gb, 128)).reshape(nt * gb, 128)
    sin = jnp.broadcast_to(sin3, (nt, gb, 128)).reshape(nt * gb, 128)
    k = _rope(proj(O_K, O_V), cos, sin)
    v = proj(O_V, O_ZA)
    kn_s[...] = k
    vn_s[...] = v
    q = proj(O_Q, O_K) * (HEAD_DIM ** -0.5)
    for g, qg in enumerate(_group_heads([q[:, 128 * c:128 * (c + 1)] for c in range(GROUP)])):
        qs[:, 128 * g:128 * (g + 1)] = _rope(qg, cos, sin)

    for new_rows, by_batch, new_t in ((k, kbt, knt), (v, vbt, vnt)):
        for tt in range(nt):
            by_batch[pl.ds(tt, gb, stride=nt), :] = new_rows[tt * gb:(tt + 1) * gb]
        new_t[...] = by_batch[...].T
    per_blk = 128 // nt
    lane_kv = lax.broadcasted_iota(jnp.int32, (D_KV, WINDOW), 1)

    def roll_cache(b):
        blk, j = divmod(b, per_blk)
        shift = (WINDOW - nt - nt * j) % WINDOW
        for new_t, cache_ref, out_ref in ((knt, ck_ref, ko_ref), (vnt, cv_ref, vo_ref)):
            old = pltpu.roll(cache_ref[0, b].reshape(D_KV, WINDOW), WINDOW - nt, axis=1)
            new = new_t[:, 128 * blk:128 * (blk + 1)]
            if shift:
                new = pltpu.roll(new, shift, axis=1)
            out_ref[0, b] = jnp.where(lane_kv >= WINDOW - nt, new, old).reshape(
                N_KV_HEADS, HEAD_DIM, WINDOW)

    nrow = nt * GROUP * N_KV_HEADS * SUB
    r1 = lax.broadcasted_iota(jnp.int32, (nrow, SUB * WINDOW), 0)
    c1 = lax.broadcasted_iota(jnp.int32, (nrow, SUB * WINDOW), 1)
    mask1 = ((c1 >> 7) == (r1 & (SUB - 1))) & ((c1 & (WINDOW - 1)) > (r1 >> 6))
    r2 = lax.broadcasted_iota(jnp.int32, (nrow, 128), 0)
    c2 = lax.broadcasted_iota(jnp.int32, (nrow, 128), 1)
    mask2 = ((c2 < nt * SUB) & ((c2 & (SUB - 1)) == (r2 & (SUB - 1)))
             & ((c2 >> 3) <= (r2 >> 6)))
    lane_q = lax.broadcasted_iota(jnp.int32, (SUB, 128), 1)
    low = lane_q < HEAD_DIM
    fill1 = jnp.where(c1 == 0, sinkrow_ref[0], MASK_VALUE)
    lane_c = lax.broadcasted_iota(jnp.int32, (D_KV, WINDOW), 1)
    ones_c = jnp.ones((D_KV, SUB * WINDOW), BF16)
    ones_n = jnp.ones((128, D_KV), BF16)
    zeros_n = jnp.zeros((128 - nt * SUB, D_KV), F32)

    def sub_block(sb):
        b0 = sb * SUB
        pieces = []
        for tt in range(nt):
            for g in range(GROUP):
                qp = qs[pl.ds(tt * gb + b0, SUB), 128 * g:128 * (g + 1)]
                pieces.append(jnp.where(low, qp, 0.0))
                pieces.append(jnp.where(low, 0.0, qp))
        lhs = jnp.concatenate(pieces, axis=0).astype(BF16)
        kblk = ck_ref[0, pl.ds(b0, SUB)]
        vblk = cv_ref[0, pl.ds(b0, SUB)]
        kc = jnp.concatenate(
            [kblk[bb].reshape(D_KV, WINDOW) for bb in range(SUB)], axis=1).astype(BF16)
        vc = jnp.concatenate(
            [jnp.where(lane_c == 0, 0.0, vblk[0].reshape(D_KV, WINDOW))]
            + [vblk[bb].reshape(D_KV, WINDOW) for bb in range(1, SUB)], axis=1).astype(BF16)
        kn = jnp.concatenate(
            [kn_s[pl.ds(tt * gb + b0, SUB), :] for tt in range(nt)] + [zeros_n],
            axis=0).astype(BF16)
        vn = jnp.concatenate(
            [vn_s[pl.ds(tt * gb + b0, SUB), :] for tt in range(nt)] + [zeros_n],
            axis=0).astype(BF16)
        nt_dims = (((1,), (1,)), ((), ()))
        s1 = jnp.where(mask1, jnp.dot(lhs, kc, preferred_element_type=F32), fill1)
        s2 = jnp.where(mask2, lax.dot_general(lhs, kn, nt_dims, preferred_element_type=F32),
                       MASK_VALUE)
        s = jnp.concatenate([s1, s2], axis=1)
        p = jnp.exp(s - jnp.max(s, axis=-1, keepdims=True)).astype(BF16)
        tot = (lax.dot_general(p[:, 0:SUB * WINDOW], jnp.concatenate([vc, ones_c], axis=0),
                               nt_dims, preferred_element_type=F32)
               + jnp.dot(p[:, SUB * WINDOW:], jnp.concatenate([vn, ones_n], axis=1),
                         preferred_element_type=F32))
        o_all = tot[:, 0:D_KV] / tot[:, D_KV:]
        for tt in range(nt):
            for g in range(GROUP):
                r0 = ((tt * GROUP + g) * N_KV_HEADS) * SUB
                o = jnp.where(low, o_all[r0:r0 + SUB], o_all[r0 + SUB:r0 + 2 * SUB])
                ocat[pl.ds(tt * gb + b0, SUB), 128 * g:128 * (g + 1)] = o

    for sb in range(gb // SUB):
        sub_block(sb)
        for b in range(sb * SUB, (sb + 1) * SUB):
            roll_cache(b)
    o_nat = _ungroup_heads([ocat[:, 128 * g:128 * (g + 1)] for g in range(GROUP)])
    yc = jnp.concatenate(o_nat, axis=-1) * _silu(proj(O_ZA, D_IN))

    mixcat = jnp.concatenate([ya, yb, yc], axis=-1).astype(BF16)
    mix = jnp.dot(mixcat, wout_ref[0], preferred_element_type=F32).reshape(nt, gb, D_MODEL)
    y = ALPHA * x3 + (1.0 + gate)[None] * mix
    xo_ref[...] = _ln(y) * lng_ref[0][None] + lnb_ref[0][None]


def _decode_layer(lidx, x, mod_s, win, wout, conv_w, conv_b, cnorm_g, cnorm_b, wpool, pool_scale,
                  ln_g, ln_b, cos_s, sin_s, sink_rows, cache_conv, cache_pool, cache_k, cache_v,
                  rolled):
    nt, nbatch, _ = x.shape
    gb = DEC_GROUP
    carried = list(rolled)
    per_layer = lambda shape: pl.BlockSpec((1,) + shape, lambda g, l: (l[0],) + (0,) * len(shape))
    tok = lambda c: pl.BlockSpec((nt, gb, c), lambda g, l: (0, g, 0))
    time_major = lambda r, c: pl.BlockSpec((1, r, gb, c), lambda g, l: (l[0], 0, g, 0))
    kv_blk = pl.BlockSpec((1, gb, N_KV_HEADS, HEAD_DIM, WINDOW), lambda g, l: (l[0], g, 0, 0, 0))
    nrow = nt * GROUP * N_KV_HEADS * SUB
    grid_spec = pltpu.PrefetchScalarGridSpec(
        num_scalar_prefetch=1,
        grid=(nbatch // gb,),
        in_specs=[
            tok(D_MODEL),
            pl.BlockSpec((1, gb, 3 * D_MODEL), lambda g, l: (l[0], g, 0)),
            per_layer((D_MODEL, D_IN)),
            per_layer((D_MODEL, D_MODEL)),
            per_layer((CONV_WIDTH, D_CONV)),
            per_layer((1, D_CONV)),
            per_layer((1, D_CONV)),
            per_layer((1, D_CONV)),
            per_layer((D_POOL, D_POOL)),
            per_layer((1, D_POOL)),
            per_layer((1, D_MODEL)),
            per_layer((1, D_MODEL)),
            pl.BlockSpec((nt, 1, 128), lambda g, l: (0, 0, 0)),
            pl.BlockSpec((nt, 1, 128), lambda g, l: (0, 0, 0)),
            per_layer((nrow, 1)),
            time_major(CONV_BUF, D_CONV),
            time_major(POOL_BUF, D_POOL),
            kv_blk,
            kv_blk,
        ] + [pl.BlockSpec(memory_space=pl.ANY)] * len(carried),
        out_specs=[tok(D_MODEL), time_major(CONV_BUF, D_CONV), time_major(POOL_BUF, D_POOL),
                   kv_blk, kv_blk],
        scratch_shapes=[
            pltpu.VMEM((nt * gb, D_ATTN), F32),
            pltpu.VMEM((nt * gb, D_ATTN), F32),
            pltpu.VMEM((nt * gb, D_KV), F32),
            pltpu.VMEM((nt * gb, D_KV), F32),
            pltpu.VMEM((nt * gb, D_KV), F32),
            pltpu.VMEM((nt * gb, D_KV), F32),
            pltpu.VMEM((D_KV, nt * gb), F32),
            pltpu.VMEM((D_KV, nt * gb), F32),
        ],
    )
    out_shape = [jax.ShapeDtypeStruct((nt, nbatch, D_MODEL), F32)] + [
        jax.ShapeDtypeStruct(c.shape, F32) for c in (cache_conv, cache_pool, cache_k, cache_v)]
    operands = (lidx, x, mod_s, win, wout, conv_w, conv_b, cnorm_g, cnorm_b, wpool, pool_scale,
                ln_g, ln_b, cos_s, sin_s, sink_rows, cache_conv, cache_pool, cache_k, cache_v)
    n_in = len(operands)
    return pl.pallas_call(
        _decode_kernel,
        grid_spec=grid_spec,
        out_shape=out_shape,
        input_output_aliases={n_in + i: 1 + i for i in range(len(carried))},
        compiler_params=pltpu.CompilerParams(
            dimension_semantics=("arbitrary",), vmem_limit_bytes=VMEM_LIMIT),
    )(*operands, *carried)


def _rope_tables(pos):
    half = HEAD_DIM // 2
    inv_freq = ROPE_THETA ** (-jnp.arange(half, dtype=F32) * (2.0 / HEAD_DIM))
    ang = pos.astype(F32)[:, None] * inv_freq[None, :]
    cos = jnp.cos(ang)
    sin = jnp.sin(ang)
    cos_t = jnp.concatenate([cos, cos, cos, cos], axis=-1)
    sin_t = jnp.concatenate([-sin, sin, -sin, sin], axis=-1)
    return cos_t, sin_t


def kernel(x_prompt, x_sample, cache_conv, cache_pool, cache_k, cache_v, c_prompt, c_sample,
           w_in, w_out, conv_w, conv_b, cnorm_g, cnorm_b, pool_w, pool_scale, sinks,
           w_mod, b_mod, ln_g, ln_b):
    B, T, _ = x_prompt.shape
    nbatch, nt, _ = x_sample.shape

    win = w_in.astype(BF16)
    wout = w_out.astype(BF16)
    wpool = jnp.zeros((DEPTH, D_POOL, D_POOL), F32)
    for gi in range(len(POOL_WINDOWS)):
        sl = slice(gi * POOL_GROUP_DIM, (gi + 1) * POOL_GROUP_DIM)
        wpool = wpool.at[:, sl, sl].set(pool_w[:, gi])
    wpool = wpool.astype(BF16)
    sinks_gh = sinks.reshape(DEPTH, N_KV_HEADS, GROUP).transpose(0, 2, 1)
    nq = D_ATTN // 128
    order = ([2 * c + (2 * c) // GROUP for c in range(nq)]
             + [2 * c + 1 - (2 * c) // GROUP for c in range(nq)])
    sinks_ord = jnp.stack([sinks[:, hd] for hd in order], axis=1)
    sinks_flat = jnp.concatenate(
        [sinks_ord[:, :, None, None] * LOG2E,
         jnp.full((DEPTH, N_HEADS, 1, 2 * BLOCK - 1), MASK_VALUE, F32)], axis=-1)
    sink_rows = jnp.broadcast_to(
        sinks_gh[:, None, :, :, None], (DEPTH, nt, GROUP, N_KV_HEADS, SUB)
    ).reshape(DEPTH, nt * GROUP * N_KV_HEADS * SUB, 1)

    r3 = lambda p: p.reshape(DEPTH, 1, -1)
    conv_b3, cng3, cnb3, pscale3, lng3, lnb3 = map(r3, (conv_b, cnorm_g, cnorm_b, pool_scale, ln_g, ln_b))

    cos_p, sin_p = _rope_tables(jnp.arange(T, dtype=jnp.int32))
    cos_s, sin_s = _rope_tables(PAST_LEN + jnp.arange(nt, dtype=jnp.int32))
    cos_s = cos_s.reshape(nt, 1, 128)
    sin_s = sin_s.reshape(nt, 1, 128)

    mod = _modulation(jnp.concatenate([c_sample, c_prompt], axis=0), w_mod, b_mod)
    mod_p = mod[:, nbatch:].reshape(DEPTH, B, 3, D_MODEL)

    cconv_t = cache_conv.transpose(0, 2, 1, 3)
    cpool_t = cache_pool.transpose(0, 2, 1, 3)
    ck_t = cache_k.transpose(0, 1, 3, 4, 2)
    cv_t = cache_v.transpose(0, 1, 3, 4, 2)

    xp = x_prompt
    xs = x_sample.transpose(1, 0, 2)
    conv_p, pool_p, k_p, v_p = ([] for _ in range(4))
    rolled = ()
    for l in range(DEPTH):
        lidx = jnp.full((1,), l, jnp.int32)
        xp, cp, pp, kp, vp = _prompt_layer(
            lidx, sinks_flat, xp, mod_p, win, wout, conv_w, conv_b3, cng3, cnb3, wpool, pscale3,
            lng3, lnb3, cos_p, sin_p)
        xs, *rolled = _decode_layer(
            lidx, xs, mod, win, wout, conv_w, conv_b3, cng3, cnb3, wpool, pscale3,
            lng3, lnb3, cos_s, sin_s, sink_rows, cconv_t, cpool_t, ck_t, cv_t, rolled)
        conv_p.append(cp); pool_p.append(pp); k_p.append(kp); v_p.append(vp)

    kv5 = lambda z: z.reshape(z.shape[:-1] + (N_KV_HEADS, HEAD_DIM))
    conv_s, pool_s, k_s, v_s = rolled
    return (xp, xs.transpose(1, 0, 2),
            jnp.stack(conv_p), jnp.stack(pool_p), kv5(jnp.stack(k_p)), kv5(jnp.stack(v_p)),
            conv_s.transpose(0, 2, 1, 3), pool_s.transpose(0, 2, 1, 3),
            k_s.transpose(0, 1, 4, 2, 3), v_s.transpose(0, 1, 4, 2, 3))
```

```python
import jax
import jax.numpy as jnp
from jax import lax
from jax.experimental import pallas as pl
from jax.experimental.pallas import tpu as pltpu

F32 = jnp.float32
BF16 = jnp.bfloat16

D_MODEL = 1024
DEPTH = 4
D_CONV = 256
D_POOL = 256
D_ATTN = 512
HEAD_DIM = 64
N_HEADS = 8
N_KV_HEADS = 2
GROUP = 4
D_KV = 128
WINDOW = 128
BLOCK = 128
CONV_WIDTH = 31
CONV_BUF = 30
POOL_WINDOWS = (2, 4, 8, 16)
POOL_GROUP_DIM = 64
POOL_BUF = 15
ROPE_THETA = 10000.0
LN_EPS = 1e-5
ALPHA = (2.0 * DEPTH) ** 0.25
LOG2E = 1.4426950408889634
MASK_VALUE = -1e30
PAST_LEN = 8192

O_U, O_G, O_ZC, O_PV, O_ZP, O_Q, O_K, O_V, O_ZA, D_IN = (
    0, 256, 512, 768, 1024, 1280, 1792, 1920, 2048, 2560)

SUBLANES = 8
CARRY = 32
TM = 1024
ROW_PIECES = 4
OUT_PIECES = 4
DEC_GROUP = 32
SUB = 8
VMEM_LIMIT = 48 * 1024 * 1024


def _sigmoid(x):
    return 1.0 / (1.0 + jnp.exp(-x))


def _silu(x):
    return x * _sigmoid(x)


def _ln(x, eps=LN_EPS):
    mu = jnp.mean(x, axis=-1, keepdims=True)
    xc = x - mu
    var = jnp.mean(xc * xc, axis=-1, keepdims=True)
    return xc * lax.rsqrt(var + eps)


def _rope(x, cos, sin_signed):
    lane = lax.broadcasted_iota(jnp.int32, x.shape, 1)
    first_half = (lane & 63) < 32
    rot = jnp.where(first_half, pltpu.roll(x, 96, axis=1), pltpu.roll(x, 32, axis=1))
    return x * cos + rot * sin_signed


def _swap_halves(x):
    return pltpu.roll(x, HEAD_DIM, axis=1)


def _group_heads(nat):
    low = lax.broadcasted_iota(jnp.int32, nat[0].shape, 1) < HEAD_DIM
    out = []
    for g in range(GROUP):
        a, b = nat[g // 2], nat[GROUP // 2 + g // 2]
        out.append(jnp.where(low, a, _swap_halves(b)) if g % 2 == 0
                   else jnp.where(low, _swap_halves(a), b))
    return out


def _ungroup_heads(grouped):
    low = lax.broadcasted_iota(jnp.int32, grouped[0].shape, 1) < HEAD_DIM
    nat = [None] * GROUP
    for c in range(GROUP // 2):
        even, odd = grouped[2 * c], grouped[2 * c + 1]
        nat[c] = jnp.where(low, even, _swap_halves(odd))
        nat[GROUP // 2 + c] = jnp.where(low, _swap_halves(even), odd)
    return nat


def _mod_kernel(c_ref, w_ref, b_ref, o_ref):
    c = c_ref[...]
    sc = _silu(c).astype(BF16)
    o_ref[0] = jnp.dot(sc, w_ref[0].astype(BF16), preferred_element_type=F32) + b_ref[0]


def _modulation(c_all, w_mod, b_mod):
    n = c_all.shape[0]
    tn = 3 * D_MODEL
    return pl.pallas_call(
        _mod_kernel,
        grid=(DEPTH, 3 * D_MODEL // tn),
        in_specs=[
            pl.BlockSpec((n, D_MODEL), lambda l, j: (0, 0)),
            pl.BlockSpec((1, D_MODEL, tn), lambda l, j: (l, 0, j)),
            pl.BlockSpec((1, 1, tn), lambda l, j: (l, 0, j)),
        ],
        out_specs=pl.BlockSpec((1, n, tn), lambda l, j: (l, 0, j)),
        out_shape=jax.ShapeDtypeStruct((DEPTH, n, 3 * D_MODEL), F32),
        compiler_params=pltpu.CompilerParams(
            dimension_semantics=("arbitrary", "arbitrary"), vmem_limit_bytes=VMEM_LIMIT),
    )(c_all, w_mod, b_mod.reshape(DEPTH, 1, 3 * D_MODEL))


def _prompt_kernel(l_ref, sink_ref,
                   x_ref, mod_ref, win_ref, wout_ref, convw_ref, convb_ref, cng_ref, cnb_ref,
                   wpool_ref, pscale_ref, lng_ref, lnb_ref, cos_ref, sin_ref,
                   xo_ref, convo_ref, poolo_ref, ko_ref, vo_ref,
                   acat, ashift, pcat, ps_a, ps_b, kcat0, kcat1, vcat0, vcat1, mixcat):
    del l_ref
    kcat, vcat = (kcat0, kcat1), (vcat0, vcat1)
    t = pl.program_id(1)
    last = pl.num_programs(1) - 1
    nb = TM // BLOCK

    @pl.when(t == 0)
    def _():
        acat[0:CARRY, :] = jnp.zeros((CARRY, D_CONV), F32)
        pcat[0:CARRY, :] = jnp.zeros((CARRY, D_POOL), F32)
        for kc, vc in zip(kcat, vcat):
            kc[0:BLOCK, :] = jnp.zeros((BLOCK, D_KV), BF16)
            vc[0:BLOCK, 0:D_KV] = jnp.zeros((BLOCK, D_KV), BF16)
            vc[:, D_KV:2 * D_KV] = jnp.ones((BLOCK + TM, D_KV), BF16)

    kprev = [kc[0:BLOCK, :] for kc in kcat]
    vprev = [vc[0:BLOCK, :] for vc in vcat]
    x = x_ref[0]
    shift = mod_ref[0, 0, 0:1, :]
    scale = mod_ref[0, 0, 1:2, :]
    gate = mod_ref[0, 0, 2:3, :]
    piece = TM // ROW_PIECES
    nq = D_ATTN // 128
    cos = cos_ref[...]
    sin = sin_ref[...]
    hbs, ks, vs, zas, qrs = [], [], [], [], []
    for r0 in range(0, TM, piece):
        hbs.append((_ln(x[r0:r0 + piece]) * (1.0 + scale) + shift).astype(BF16))
        pr = jnp.dot(hbs[-1], win_ref[0, :, O_Q:D_IN], preferred_element_type=F32)
        cos_p, sin_p = cos[r0:r0 + piece], sin[r0:r0 + piece]
        k_p = _rope(pr[:, O_K - O_Q:O_V - O_Q], cos_p, sin_p)
        v_p = pr[:, O_V - O_Q:O_ZA - O_Q]
        kcat[0][BLOCK + r0:BLOCK + r0 + piece, :] = k_p.astype(BF16)
        kcat[1][BLOCK + r0:BLOCK + r0 + piece, :] = _swap_halves(k_p).astype(BF16)
        vcat[0][BLOCK + r0:BLOCK + r0 + piece, 0:D_KV] = v_p.astype(BF16)
        vcat[1][BLOCK + r0:BLOCK + r0 + piece, 0:D_KV] = _swap_halves(v_p).astype(BF16)
        q_p = pr[:, 0:D_ATTN] * (HEAD_DIM ** -0.5 * LOG2E)
        qrs.append([_rope(q_p[:, 128 * c:128 * (c + 1)], cos_p, sin_p) for c in range(nq)])
        ks.append(k_p)
        vs.append(v_p)
        zas.append(pr[:, O_ZA - O_Q:D_IN - O_Q])
    hb = jnp.concatenate(hbs, axis=0)
    k = jnp.concatenate(ks, axis=0)
    v = jnp.concatenate(vs, axis=0)
    za = jnp.concatenate(zas, axis=0)
    qr = [jnp.concatenate([qp[c] for qp in qrs], axis=0) for c in range(nq)]

    def proj(lo, hi):
        return jnp.dot(hb, win_ref[0, :, lo:hi], preferred_element_type=F32)


    @pl.when(t == last)
    def _():
        ko_ref[0] = k[TM - WINDOW:TM, :]
        vo_ref[0] = v[TM - WINDOW:TM, :]

    row = lax.broadcasted_iota(jnp.int32, (BLOCK, 2 * BLOCK), 0)
    col = lax.broadcasted_iota(jnp.int32, (BLOCK, 2 * BLOCK), 1)
    rel = col - BLOCK - row
    band = (rel <= 0) & (rel > -WINDOW)
    band_first = band & (col + jnp.minimum(t, 1) * BLOCK >= BLOCK)
    lane_q = lax.broadcasted_iota(jnp.int32, (BLOCK, 2 * HEAD_DIM), 1)
    low = lane_q < HEAD_DIM

    same = [(c, (2 * c) // GROUP) for c in range(nq)]
    other = [(c, 1 - (2 * c) // GROUP) for c in range(nq)]
    sink_fill = sink_ref[0]
    vrow = lax.broadcasted_iota(jnp.int32, (2 * BLOCK, 2 * D_KV), 0)
    vcol = lax.broadcasted_iota(jnp.int32, (2 * BLOCK, 2 * D_KV), 1)
    sink_row = (vrow == 0) & (vcol < D_KV)
    nt_dims = (((1,), (1,)), ((), ()))
    def attn_block(i):
        mask = band if i > 0 else band_first
        scores = []
        for copy, heads in ((0, same), (1, other)):
            lhs = jnp.concatenate(
                [jnp.where(low == (half == 0), qr[c][i * BLOCK:(i + 1) * BLOCK], 0.0)
                 for c, half in heads], axis=0).astype(BF16)
            kk = (kcat[copy][i * BLOCK:(i + 2) * BLOCK, :] if i > 0 else
                  jnp.concatenate([kprev[copy], kcat[copy][BLOCK:2 * BLOCK, :]], axis=0))
            scores.append(lax.dot_general(lhs, kk, nt_dims, preferred_element_type=F32))
        s = jnp.concatenate(scores, axis=0).reshape(N_HEADS, BLOCK, 2 * BLOCK)
        s = jnp.where(mask[None], s, sink_fill)
        m = jnp.max(s, axis=-1, keepdims=True)
        p = jnp.exp2(s - m).astype(BF16)
        half_rows = (N_HEADS // 2) * BLOCK
        outs = []
        for copy in range(2):
            vv = (vcat[copy][i * BLOCK:(i + 2) * BLOCK, :] if i > 0 else
                  jnp.concatenate([vprev[copy], vcat[copy][BLOCK:2 * BLOCK, :]], axis=0))
            vv = jnp.where(sink_row, jnp.zeros((), BF16), vv)
            pv = jnp.dot(p[copy * (N_HEADS // 2):(copy + 1) * (N_HEADS // 2)].reshape(
                half_rows, 2 * BLOCK), vv,
                preferred_element_type=F32).reshape(N_HEADS // 2, BLOCK, 2 * D_KV)
            outs.append(pv[:, :, 0:D_KV] / pv[:, :, D_KV:])
        for c in range(nq):
            o = jnp.where(low == (same[c][1] == 0), outs[0][c], outs[1][c])
            zg = za[i * BLOCK:(i + 1) * BLOCK, 128 * c:128 * (c + 1)]
            c0 = D_CONV + D_POOL + 128 * c
            mixcat[i * BLOCK:(i + 1) * BLOCK, c0:c0 + 128] = (o * _silu(zg)).astype(BF16)

    a = proj(O_U, O_G) * _sigmoid(proj(O_G, O_ZC))
    acat[CARRY:CARRY + TM, :] = a
    zc = proj(O_ZC, O_PV)
    for r in range(1, SUBLANES):
        ashift[r - 1] = acat[r:r + TM + CARRY - SUBLANES, :]
    rows = 128

    def conv_chunk(c):
        acc = jnp.zeros((rows, D_CONV), F32)
        for j in range(CONV_WIDTH):
            off = CARRY - CONV_BUF + j
            r, lo = off % SUBLANES, c * rows + off - off % SUBLANES
            tap = acat[lo:lo + rows, :] if r == 0 else ashift[r - 1, lo:lo + rows, :]
            acc = acc + tap * convw_ref[0, j:j + 1, :]
        conv = acc + convb_ref[0]
        ya = _silu(_ln(conv) * cng_ref[0] + cnb_ref[0])
        mixcat[c * rows:(c + 1) * rows, 0:D_CONV] = (
            ya * _silu(zc[c * rows:(c + 1) * rows])).astype(BF16)

    pool_v = proj(O_PV, O_ZP)
    pcat[CARRY:CARRY + TM, :] = pool_v
    zp = proj(O_ZP, O_Q)

    for i in range(nb):
        attn_block(i)
    for c in range(TM // rows):
        conv_chunk(c)

    kcat[0][0:BLOCK, :] = k[TM - BLOCK:TM].astype(BF16)
    kcat[1][0:BLOCK, :] = _swap_halves(k[TM - BLOCK:TM]).astype(BF16)
    vcat[0][0:BLOCK, 0:D_KV] = v[TM - BLOCK:TM].astype(BF16)
    vcat[1][0:BLOCK, 0:D_KV] = _swap_halves(v[TM - BLOCK:TM]).astype(BF16)

    @pl.when(t == last)
    def _():
        convo_ref[0] = acat[CARRY + TM - CONV_BUF:CARRY + TM, :]

    acat[0:CARRY, :] = acat[TM:TM + CARRY, :]

    n = CARRY + TM
    ps_a[8:n, :] = pcat[8:n, :] + pcat[7:n - 1, :]
    ps_b[16:n, :] = ps_a[16:n, :] + ps_a[14:n - 2, :]
    s2 = ps_a[CARRY:n, :]
    s4 = ps_b[CARRY:n, :]
    ps_a[24:n, :] = ps_b[24:n, :] + ps_b[20:n - 4, :]
    s8 = ps_a[CARRY:n, :]
    s16 = s8 + ps_a[CARRY - 8:n - 8, :]
    lane = lax.broadcasted_iota(jnp.int32, (TM, D_POOL), 1)
    sums = jnp.where(lane < 64, s2, jnp.where(lane < 128, s4, jnp.where(lane < 192, s8, s16)))
    wlen = jnp.where(lane < 64, 2.0, jnp.where(lane < 128, 4.0, jnp.where(lane < 192, 8.0, 16.0)))
    pos1 = (lax.broadcasted_iota(jnp.int32, (TM, D_POOL), 0) + (t * TM + 1)).astype(F32)
    cnt = jnp.minimum(wlen, pos1)
    pooled = (sums / cnt - pool_v).astype(BF16)
    yb = jnp.dot(pooled, wpool_ref[0], preferred_element_type=F32) * pscale_ref[0]
    mixcat[:, D_CONV:D_CONV + D_POOL] = (yb * _silu(zp)).astype(BF16)

    @pl.when(t == last)
    def _():
        poolo_ref[0] = pcat[CARRY + TM - POOL_BUF:CARRY + TM, :]

    pcat[0:CARRY, :] = pcat[TM:TM + CARRY, :]

    g_res = (1.0 + gate) * (1.0 / ALPHA)
    piece = TM // OUT_PIECES
    for r0 in range(0, TM, piece):
        mix = jnp.dot(mixcat[r0:r0 + piece, :], wout_ref[0], preferred_element_type=F32)
        y = x[r0:r0 + piece] + g_res * mix
        xo_ref[0, r0:r0 + piece, :] = _ln(y, LN_EPS / ALPHA ** 2) * lng_ref[0] + lnb_ref[0]


def _prompt_layer(lidx, sink_fill, x, mod_p, win, wout, conv_w, conv_b, cnorm_g, cnorm_b,
                  wpool, pool_scale, ln_g, ln_b, cos_t, sin_t):
    B, T, _ = x.shape
    nt = T // TM
    per_layer = lambda shape: pl.BlockSpec((1,) + shape, lambda b, t, l: (l[0],) + (0,) * len(shape))
    grid_spec = pltpu.PrefetchScalarGridSpec(
        num_scalar_prefetch=1,
        grid=(B, nt),
        in_specs=[
            per_layer((N_HEADS, 1, 2 * BLOCK)),
            pl.BlockSpec((1, TM, D_MODEL), lambda b, t, l: (b, t, 0)),
            pl.BlockSpec((1, 1, 3, D_MODEL), lambda b, t, l: (l[0], b, 0, 0)),
            per_layer((D_MODEL, D_IN)),
            per_layer((D_MODEL, D_MODEL)),
            per_layer((CONV_WIDTH, D_CONV)),
            per_layer((1, D_CONV)),
            per_layer((1, D_CONV)),
            per_layer((1, D_CONV)),
            per_layer((D_POOL, D_POOL)),
            per_layer((1, D_POOL)),
            per_layer((1, D_MODEL)),
            per_layer((1, D_MODEL)),
            pl.BlockSpec((TM, 128), lambda b, t, l: (t, 0)),
            pl.BlockSpec((TM, 128), lambda b, t, l: (t, 0)),
        ],
        out_specs=[
            pl.BlockSpec((1, TM, D_MODEL), lambda b, t, l: (b, t, 0)),
            pl.BlockSpec((1, CONV_BUF, D_CONV), lambda b, t, l: (b, 0, 0)),
            pl.BlockSpec((1, POOL_BUF, D_POOL), lambda b, t, l: (b, 0, 0)),
            pl.BlockSpec((1, WINDOW, D_KV), lambda b, t, l: (b, 0, 0)),
            pl.BlockSpec((1, WINDOW, D_KV), lambda b, t, l: (b, 0, 0)),
        ],
        scratch_shapes=[
            pltpu.VMEM((CARRY + TM, D_CONV), F32),
            pltpu.VMEM((SUBLANES - 1, CARRY + TM - SUBLANES, D_CONV), F32),
            pltpu.VMEM((CARRY + TM, D_POOL), F32),
            pltpu.VMEM((CARRY + TM, D_POOL), F32),
            pltpu.VMEM((CARRY + TM, D_POOL), F32),
            pltpu.VMEM((BLOCK + TM, D_KV), BF16),
            pltpu.VMEM((BLOCK + TM, D_KV), BF16),
            pltpu.VMEM((BLOCK + TM, 2 * D_KV), BF16),
            pltpu.VMEM((BLOCK + TM, 2 * D_KV), BF16),
            pltpu.VMEM((TM, D_MODEL), BF16),
        ],
    )
    out_shape = [
        jax.ShapeDtypeStruct((B, T, D_MODEL), F32),
        jax.ShapeDtypeStruct((B, CONV_BUF, D_CONV), F32),
        jax.ShapeDtypeStruct((B, POOL_BUF, D_POOL), F32),
        jax.ShapeDtypeStruct((B, WINDOW, D_KV), F32),
        jax.ShapeDtypeStruct((B, WINDOW, D_KV), F32),
    ]
    return pl.pallas_call(
        _prompt_kernel,
        grid_spec=grid_spec,
        out_shape=out_shape,
        compiler_params=pltpu.CompilerParams(
            dimension_semantics=("arbitrary", "arbitrary"), vmem_limit_bytes=VMEM_LIMIT),
    )(lidx, sink_fill, x, mod_p, win, wout, conv_w, conv_b, cnorm_g, cnorm_b,
      wpool, pool_scale, ln_g, ln_b, cos_t, sin_t)


def _decode_kernel(l_ref,
                   x_ref, mod_ref, win_ref, wout_ref, convw_ref, convb_ref, cng_ref, cnb_ref,
                   wpool_ref, pscale_ref, lng_ref, lnb_ref, cos_ref, sin_ref, sinkrow_ref,
                   cconv_ref, cpool_ref, ck_ref, cv_ref, *rest):
    xo_ref, convo_ref, poolo_ref, ko_ref, vo_ref, qs, ocat, kn_s, vn_s, kbt, vbt, knt, vnt = rest[-13:]
    gb = DEC_GROUP
    nt = x_ref.shape[0]
    x3 = x_ref[...]
    shift = mod_ref[0, :, 0:D_MODEL]
    scale = mod_ref[0, :, D_MODEL:2 * D_MODEL]
    gate = mod_ref[0, :, 2 * D_MODEL:3 * D_MODEL]
    h3 = _ln(x3) * (1.0 + scale)[None] + shift[None]
    hb = h3.reshape(nt * gb, D_MODEL).astype(BF16)

    def proj(lo, hi):
        return jnp.dot(hb, win_ref[0, :, lo:hi], preferred_element_type=F32)

    a = proj(O_U, O_G) * _sigmoid(proj(O_G, O_ZC))
    convo_ref[0, 0:CONV_BUF - nt] = cconv_ref[0, nt:CONV_BUF]
    convo_ref[0, CONV_BUF - nt:CONV_BUF] = a.reshape(nt, gb, D_CONV)
    zc = proj(O_ZC, O_PV)
    ya_rows = []
    for tt in range(nt):
        acc = jnp.zeros((gb, D_CONV), F32)
        for j in range(CONV_WIDTH):
            i = tt + j
            xi = (cconv_ref[0, i] if i < CONV_BUF
                  else a[(i - CONV_BUF) * gb:(i - CONV_BUF + 1) * gb])
            acc = acc + xi * convw_ref[0, j:j + 1, :]
        conv = acc + convb_ref[0]
        ya_rows.append(_silu(_ln(conv) * cng_ref[0] + cnb_ref[0]))
    ya = jnp.concatenate(ya_rows, axis=0) * _silu(zc)

    pv = proj(O_PV, O_ZP)
    poolo_ref[0, 0:POOL_BUF - nt] = cpool_ref[0, nt:POOL_BUF]
    poolo_ref[0, POOL_BUF - nt:POOL_BUF] = pv.reshape(nt, gb, D_POOL)
    lane = lax.broadcasted_iota(jnp.int32, (gb, D_POOL), 1)
    wlen = jnp.where(lane < 64, 2, jnp.where(lane < 128, 4, jnp.where(lane < 192, 8, 16)))
    pooled_rows = []
    for tt in range(nt):
        acc = jnp.zeros((gb, D_POOL), F32)
        for d in range(max(POOL_WINDOWS)):
            i = POOL_BUF + tt - d
            xi = (cpool_ref[0, i] if i < POOL_BUF
                  else pv[(i - POOL_BUF) * gb:(i - POOL_BUF + 1) * gb])
            acc = acc + (xi if d < min(POOL_WINDOWS) else jnp.where(wlen > d, xi, 0.0))
        cnt = jnp.minimum(wlen, PAST_LEN + tt + 1).astype(F32)
        pooled_rows.append(acc / cnt - pv[tt * gb:(tt + 1) * gb])
    pooled = jnp.concatenate(pooled_rows, axis=0).astype(BF16)
    yb = jnp.dot(pooled, wpool_ref[0], preferred_element_type=F32) * pscale_ref[0]
    yb = yb * _silu(proj(O_ZP, O_Q))

    cos3 = cos_ref[...]
    sin3 = sin_ref[...]
    cos = jnp.broadcast_to(cos3, (nt, gb, 128)).reshape(nt * gb, 128)
    sin = jnp.broadcast_to(sin3, (nt, gb, 128)).reshape(nt * gb, 128)
    k = _rope(proj(O_K, O_V), cos, sin)
    v = proj(O_V, O_ZA)
    kn_s[...] = k
    vn_s[...] = v
    q = proj(O_Q, O_K) * (HEAD_DIM ** -0.5)
    for g, qg in enumerate(_group_heads([q[:, 128 * c:128 * (c + 1)] for c in range(GROUP)])):
        qs[:, 128 * g:128 * (g + 1)] = _rope(qg, cos, sin)

    for new_rows, by_batch, new_t in ((k, kbt, knt), (v, vbt, vnt)):
        for tt in range(nt):
            by_batch[pl.ds(tt, gb, stride=nt), :] = new_rows[tt * gb:(tt + 1) * gb]
        new_t[...] = by_batch[...].T
    per_blk = 128 // nt
    lane_kv = lax.broadcasted_iota(jnp.int32, (D_KV, WINDOW), 1)

    def roll_cache(b):
        blk, j = divmod(b, per_blk)
        shift = (WINDOW - nt - nt * j) % WINDOW
        for new_t, cache_ref, out_ref in ((knt, ck_ref, ko_ref), (vnt, cv_ref, vo_ref)):
            old = pltpu.roll(cache_ref[0, b].reshape(D_KV, WINDOW), WINDOW - nt, axis=1)
            new = new_t[:, 128 * blk:128 * (blk + 1)]
            if shift:
                new = pltpu.roll(new, shift, axis=1)
            out_ref[0, b] = jnp.where(lane_kv >= WINDOW - nt, new, old).reshape(
                N_KV_HEADS, HEAD_DIM, WINDOW)

    nrow = nt * GROUP * N_KV_HEADS * SUB
    r1 = lax.broadcasted_iota(jnp.int32, (nrow, SUB * WINDOW), 0)
    c1 = lax.broadcasted_iota(jnp.int32, (nrow, SUB * WINDOW), 1)
    mask1 = ((c1 >> 7) == (r1 & (SUB - 1))) & ((c1 & (WINDOW - 1)) > (r1 >> 6))
    r2 = lax.broadcasted_iota(jnp.int32, (nrow, 128), 0)
    c2 = lax.broadcasted_iota(jnp.int32, (nrow, 128), 1)
    mask2 = ((c2 < nt * SUB) & ((c2 & (SUB - 1)) == (r2 & (SUB - 1)))
             & ((c2 >> 3) <= (r2 >> 6)))
    lane_q = lax.broadcasted_iota(jnp.int32, (SUB, 128), 1)
    low = lane_q < HEAD_DIM
    fill1 = jnp.where(c1 == 0, sinkrow_ref[0], MASK_VALUE)
    lane_c = lax.broadcasted_iota(jnp.int32, (D_KV, WINDOW), 1)
    ones_c = jnp.ones((D_KV, SUB * WINDOW), BF16)
    ones_n = jnp.ones((128, D_KV), BF16)
    zeros_n = jnp.zeros((128 - nt * SUB, D_KV), F32)

    def sub_block(sb):
        b0 = sb * SUB
        pieces = []
        for tt in range(nt):
            for g in range(GROUP):
                qp = qs[pl.ds(tt * gb + b0, SUB), 128 * g:128 * (g + 1)]
                pieces.append(jnp.where(low, qp, 0.0))
                pieces.append(jnp.where(low, 0.0, qp))
        lhs = jnp.concatenate(pieces, axis=0).astype(BF16)
        kblk = ck_ref[0, pl.ds(b0, SUB)]
        vblk = cv_ref[0, pl.ds(b0, SUB)]
        kc = jnp.concatenate(
            [kblk[bb].reshape(D_KV, WINDOW) for bb in range(SUB)], axis=1).astype(BF16)
        vc = jnp.concatenate(
            [jnp.where(lane_c == 0, 0.0, vblk[0].reshape(D_KV, WINDOW))]
            + [vblk[bb].reshape(D_KV, WINDOW) for bb in range(1, SUB)], axis=1).astype(BF16)
        kn = jnp.concatenate(
            [kn_s[pl.ds(tt * gb + b0, SUB), :] for tt in range(nt)] + [zeros_n],
            axis=0).astype(BF16)
        vn = jnp.concatenate(
            [vn_s[pl.ds(tt * gb + b0, SUB), :] for tt in range(nt)] + [zeros_n],
            axis=0).astype(BF16)
        nt_dims = (((1,), (1,)), ((), ()))
        s1 = jnp.where(mask1, jnp.dot(lhs, kc, preferred_element_type=F32), fill1)
        s2 = jnp.where(mask2, lax.dot_general(lhs, kn, nt_dims, preferred_element_type=F32),
                       MASK_VALUE)
        s = jnp.concatenate([s1, s2], axis=1)
        p = jnp.exp(s - jnp.max(s, axis=-1, keepdims=True)).astype(BF16)
        tot = (lax.dot_general(p[:, 0:SUB * WINDOW], jnp.concatenate([vc, ones_c], axis=0),
                               nt_dims, preferred_element_type=F32)
               + jnp.dot(p[:, SUB * WINDOW:], jnp.concatenate([vn, ones_n], axis=1),
                         preferred_element_type=F32))
        o_all = tot[:, 0:D_KV] / tot[:, D_KV:]
        for tt in range(nt):
            for g in range(GROUP):
                r0 = ((tt * GROUP + g) * N_KV_HEADS) * SUB
                o = jnp.where(low, o_all[r0:r0 + SUB], o_all[r0 + SUB:r0 + 2 * SUB])
                ocat[pl.ds(tt * gb + b0, SUB), 128 * g:128 * (g + 1)] = o

    for sb in range(gb // SUB):
        for b in range(sb * SUB, (sb + 1) * SUB):
            roll_cache(b)
        sub_block(sb)
    o_nat = _ungroup_heads([ocat[:, 128 * g:128 * (g + 1)] for g in range(GROUP)])
    yc = jnp.concatenate(o_nat, axis=-1) * _silu(proj(O_ZA, D_IN))

    mixcat = jnp.concatenate([ya, yb, yc], axis=-1).astype(BF16)
    mix = jnp.dot(mixcat, wout_ref[0], preferred_element_type=F32).reshape(nt, gb, D_MODEL)
    y = ALPHA * x3 + (1.0 + gate)[None] * mix
    xo_ref[...] = _ln(y) * lng_ref[0][None] + lnb_ref[0][None]


def _decode_layer(lidx, x, mod_s, win, wout, conv_w, conv_b, cnorm_g, cnorm_b, wpool, pool_scale,
                  ln_g, ln_b, cos_s, sin_s, sink_rows, cache_conv, cache_pool, cache_k, cache_v,
                  rolled):
    nt, nbatch, _ = x.shape
    gb = DEC_GROUP
    carried = list(rolled)
    per_layer = lambda shape: pl.BlockSpec((1,) + shape, lambda g, l: (l[0],) + (0,) * len(shape))
    tok = lambda c: pl.BlockSpec((nt, gb, c), lambda g, l: (0, g, 0))
    time_major = lambda r, c: pl.BlockSpec((1, r, gb, c), lambda g, l: (l[0], 0, g, 0))
    kv_blk = pl.BlockSpec((1, gb, N_KV_HEADS, HEAD_DIM, WINDOW), lambda g, l: (l[0], g, 0, 0, 0))
    nrow = nt * GROUP * N_KV_HEADS * SUB
    grid_spec = pltpu.PrefetchScalarGridSpec(
        num_scalar_prefetch=1,
        grid=(nbatch // gb,),
        in_specs=[
            tok(D_MODEL),
            pl.BlockSpec((1, gb, 3 * D_MODEL), lambda g, l: (l[0], g, 0)),
            per_layer((D_MODEL, D_IN)),
            per_layer((D_MODEL, D_MODEL)),
            per_layer((CONV_WIDTH, D_CONV)),
            per_layer((1, D_CONV)),
            per_layer((1, D_CONV)),
            per_layer((1, D_CONV)),
            per_layer((D_POOL, D_POOL)),
            per_layer((1, D_POOL)),
            per_layer((1, D_MODEL)),
            per_layer((1, D_MODEL)),
            pl.BlockSpec((nt, 1, 128), lambda g, l: (0, 0, 0)),
            pl.BlockSpec((nt, 1, 128), lambda g, l: (0, 0, 0)),
            per_layer((nrow, 1)),
            time_major(CONV_BUF, D_CONV),
            time_major(POOL_BUF, D_POOL),
            kv_blk,
            kv_blk,
        ] + [pl.BlockSpec(memory_space=pl.ANY)] * len(carried),
        out_specs=[tok(D_MODEL), time_major(CONV_BUF, D_CONV), time_major(POOL_BUF, D_POOL),
                   kv_blk, kv_blk],
        scratch_shapes=[
            pltpu.VMEM((nt * gb, D_ATTN), F32),
            pltpu.VMEM((nt * gb, D_ATTN), F32),
            pltpu.VMEM((nt * gb, D_KV), F32),
            pltpu.VMEM((nt * gb, D_KV), F32),
            pltpu.VMEM((nt * gb, D_KV), F32),
            pltpu.VMEM((nt * gb, D_KV), F32),
            pltpu.VMEM((D_KV, nt * gb), F32),
            pltpu.VMEM((D_KV, nt * gb), F32),
        ],
    )
    out_shape = [jax.ShapeDtypeStruct((nt, nbatch, D_MODEL), F32)] + [
        jax.ShapeDtypeStruct(c.shape, F32) for c in (cache_conv, cache_pool, cache_k, cache_v)]
    operands = (lidx, x, mod_s, win, wout, conv_w, conv_b, cnorm_g, cnorm_b, wpool, pool_scale,
                ln_g, ln_b, cos_s, sin_s, sink_rows, cache_conv, cache_pool, cache_k, cache_v)
    n_in = len(operands)
    return pl.pallas_call(
        _decode_kernel,
        grid_spec=grid_spec,
        out_shape=out_shape,
        input_output_aliases={n_in + i: 1 + i for i in range(len(carried))},
        compiler_params=pltpu.CompilerParams(
            dimension_semantics=("arbitrary",), vmem_limit_bytes=VMEM_LIMIT),
    )(*operands, *carried)


def _rope_tables(pos):
    half = HEAD_DIM // 2
    inv_freq = ROPE_THETA ** (-jnp.arange(half, dtype=F32) * (2.0 / HEAD_DIM))
    ang = pos.astype(F32)[:, None] * inv_freq[None, :]
    cos = jnp.cos(ang)
    sin = jnp.sin(ang)
    cos_t = jnp.concatenate([cos, cos, cos, cos], axis=-1)
    sin_t = jnp.concatenate([-sin, sin, -sin, sin], axis=-1)
    return cos_t, sin_t


def kernel(x_prompt, x_sample, cache_conv, cache_pool, cache_k, cache_v, c_prompt, c_sample,
           w_in, w_out, conv_w, conv_b, cnorm_g, cnorm_b, pool_w, pool_scale, sinks,
           w_mod, b_mod, ln_g, ln_b):
    B, T, _ = x_prompt.shape
    nbatch, nt, _ = x_sample.shape

    win = w_in.astype(BF16)
    wout = w_out.astype(BF16)
    wpool = jnp.zeros((DEPTH, D_POOL, D_POOL), F32)
    for gi in range(len(POOL_WINDOWS)):
        sl = slice(gi * POOL_GROUP_DIM, (gi + 1) * POOL_GROUP_DIM)
        wpool = wpool.at[:, sl, sl].set(pool_w[:, gi])
    wpool = wpool.astype(BF16)
    sinks_gh = sinks.reshape(DEPTH, N_KV_HEADS, GROUP).transpose(0, 2, 1)
    nq = D_ATTN // 128
    order = ([2 * c + (2 * c) // GROUP for c in range(nq)]
             + [2 * c + 1 - (2 * c) // GROUP for c in range(nq)])
    sinks_ord = jnp.stack([sinks[:, hd] for hd in order], axis=1)
    sink_fill = jnp.concatenate(
        [sinks_ord[:, :, None, None] * LOG2E,
         jnp.full((DEPTH, N_HEADS, 1, 2 * BLOCK - 1), MASK_VALUE, F32)], axis=-1)
    sink_rows = jnp.broadcast_to(
        sinks_gh[:, None, :, :, None], (DEPTH, nt, GROUP, N_KV_HEADS, SUB)
    ).reshape(DEPTH, nt * GROUP * N_KV_HEADS * SUB, 1)

    r3 = lambda p: p.reshape(DEPTH, 1, -1)
    conv_b3, cng3, cnb3, pscale3, lng3, lnb3 = map(r3, (conv_b, cnorm_g, cnorm_b, pool_scale, ln_g, ln_b))

    cos_p, sin_p = _rope_tables(jnp.arange(T, dtype=jnp.int32))
    cos_s, sin_s = _rope_tables(PAST_LEN + jnp.arange(nt, dtype=jnp.int32))
    cos_s = cos_s.reshape(nt, 1, 128)
    sin_s = sin_s.reshape(nt, 1, 128)

    mod = _modulation(jnp.concatenate([c_sample, c_prompt], axis=0), w_mod, b_mod)
    mod_p = mod[:, nbatch:].reshape(DEPTH, B, 3, D_MODEL)

    cconv_t = cache_conv.transpose(0, 2, 1, 3)
    cpool_t = cache_pool.transpose(0, 2, 1, 3)
    ck_t = cache_k.transpose(0, 1, 3, 4, 2)
    cv_t = cache_v.transpose(0, 1, 3, 4, 2)

    xp = x_prompt
    xs = x_sample.transpose(1, 0, 2)
    conv_p, pool_p, k_p, v_p = ([] for _ in range(4))
    rolled = ()
    for l in range(DEPTH):
        lidx = jnp.full((1,), l, jnp.int32)
        xp, cp, pp, kp, vp = _prompt_layer(
            lidx, sink_fill, xp, mod_p, win, wout, conv_w, conv_b3, cng3, cnb3, wpool, pscale3,
            lng3, lnb3, cos_p, sin_p)
        xs, *rolled = _decode_layer(
            lidx, xs, mod, win, wout, conv_w, conv_b3, cng3, cnb3, wpool, pscale3,
            lng3, lnb3, cos_s, sin_s, sink_rows, cconv_t, cpool_t, ck_t, cv_t, rolled)
        conv_p.append(cp); pool_p.append(pp); k_p.append(kp); v_p.append(vp)

    kv5 = lambda z: z.reshape(z.shape[:-1] + (N_KV_HEADS, HEAD_DIM))
    conv_s, pool_s, k_s, v_s = rolled
    return (xp, xs.transpose(1, 0, 2),
            jnp.stack(conv_p), jnp.stack(pool_p), kv5(jnp.stack(k_p)), kv5(jnp.stack(v_p)),
            conv_s.transpose(0, 2, 1, 3), pool_s.transpose(0, 2, 1, 3),
            k_s.transpose(0, 1, 4, 2, 3), v_s.transpose(0, 1, 4, 2, 3))
```

```python
import jax
import jax.numpy as jnp
from jax import lax
from jax.experimental import pallas as pl
from jax.experimental.pallas import tpu as pltpu

F32 = jnp.float32
BF16 = jnp.bfloat16

D_MODEL = 1024
DEPTH = 4
D_CONV = 256
D_POOL = 256
D_ATTN = 512
HEAD_DIM = 64
N_HEADS = 8
N_KV_HEADS = 2
GROUP = 4
D_KV = 128
WINDOW = 128
BLOCK = 128
CONV_WIDTH = 31
CONV_BUF = 30
POOL_WINDOWS = (2, 4, 8, 16)
POOL_GROUP_DIM = 64
POOL_BUF = 15
ROPE_THETA = 10000.0
LN_EPS = 1e-5
ALPHA = (2.0 * DEPTH) ** 0.25
LOG2E = 1.4426950408889634
MASK_VALUE = -1e30
PAST_LEN = 8192

O_U, O_G, O_ZC, O_PV, O_ZP, O_Q, O_K, O_V, O_ZA, D_IN = (
    0, 256, 512, 768, 1024, 1280, 1792, 1920, 2048, 2560)

SUBLANES = 8
CARRY = 32
TM = 1024
ROW_PIECES = 4
OUT_PIECES = 4
DEC_GROUP = 32
SUB = 8
VMEM_LIMIT = 48 * 1024 * 1024


def _sigmoid(x):
    return 1.0 / (1.0 + jnp.exp(-x))


def _silu(x):
    return x * _sigmoid(x)


def _ln(x, eps=LN_EPS):
    mu = jnp.mean(x, axis=-1, keepdims=True)
    xc = x - mu
    var = jnp.mean(xc * xc, axis=-1, keepdims=True)
    return xc * lax.rsqrt(var + eps)


def _rope(x, cos, sin_signed):
    lane = lax.broadcasted_iota(jnp.int32, x.shape, 1)
    first_half = (lane & 63) < 32
    rot = jnp.where(first_half, pltpu.roll(x, 96, axis=1), pltpu.roll(x, 32, axis=1))
    return x * cos + rot * sin_signed


def _swap_halves(x):
    return pltpu.roll(x, HEAD_DIM, axis=1)


def _group_heads(nat):
    low = lax.broadcasted_iota(jnp.int32, nat[0].shape, 1) < HEAD_DIM
    out = []
    for g in range(GROUP):
        a, b = nat[g // 2], nat[GROUP // 2 + g // 2]
        out.append(jnp.where(low, a, _swap_halves(b)) if g % 2 == 0
                   else jnp.where(low, _swap_halves(a), b))
    return out


def _ungroup_heads(grouped):
    low = lax.broadcasted_iota(jnp.int32, grouped[0].shape, 1) < HEAD_DIM
    nat = [None] * GROUP
    for c in range(GROUP // 2):
        even, odd = grouped[2 * c], grouped[2 * c + 1]
        nat[c] = jnp.where(low, even, _swap_halves(odd))
        nat[GROUP // 2 + c] = jnp.where(low, _swap_halves(even), odd)
    return nat


def _mod_kernel(c_ref, w_ref, b_ref, o_ref):
    c = c_ref[...]
    sc = _silu(c).astype(BF16)
    o_ref[0] = jnp.dot(sc, w_ref[0].astype(BF16), preferred_element_type=F32) + b_ref[0]


def _modulation(c_all, w_mod, b_mod):
    n = c_all.shape[0]
    tn = 3 * D_MODEL
    return pl.pallas_call(
        _mod_kernel,
        grid=(DEPTH, 3 * D_MODEL // tn),
        in_specs=[
            pl.BlockSpec((n, D_MODEL), lambda l, j: (0, 0)),
            pl.BlockSpec((1, D_MODEL, tn), lambda l, j: (l, 0, j)),
            pl.BlockSpec((1, 1, tn), lambda l, j: (l, 0, j)),
        ],
        out_specs=pl.BlockSpec((1, n, tn), lambda l, j: (l, 0, j)),
        out_shape=jax.ShapeDtypeStruct((DEPTH, n, 3 * D_MODEL), F32),
        compiler_params=pltpu.CompilerParams(
            dimension_semantics=("arbitrary", "arbitrary"), vmem_limit_bytes=VMEM_LIMIT),
    )(c_all, w_mod, b_mod.reshape(DEPTH, 1, 3 * D_MODEL))


def _prompt_kernel(l_ref, sink_ref,
                   x_ref, mod_ref, win_ref, wout_ref, convw_ref, convb_ref, cng_ref, cnb_ref,
                   wpool_ref, pscale_ref, lng_ref, lnb_ref, cos_ref, sin_ref,
                   xo_ref, convo_ref, poolo_ref, ko_ref, vo_ref,
                   acat, ashift, pcat, ps_a, ps_b, kcat0, kcat1, vcat0, vcat1, mixcat):
    del l_ref
    kcat, vcat = (kcat0, kcat1), (vcat0, vcat1)
    t = pl.program_id(1)
    last = pl.num_programs(1) - 1
    nb = TM // BLOCK

    @pl.when(t == 0)
    def _():
        acat[0:CARRY, :] = jnp.zeros((CARRY, D_CONV), F32)
        pcat[0:CARRY, :] = jnp.zeros((CARRY, D_POOL), F32)
        for kc, vc in zip(kcat, vcat):
            kc[0:BLOCK, :] = jnp.zeros((BLOCK, D_KV), BF16)
            vc[0:BLOCK, 0:D_KV] = jnp.zeros((BLOCK, D_KV), BF16)
            vc[:, D_KV:2 * D_KV] = jnp.ones((BLOCK + TM, D_KV), BF16)

    kprev = [kc[0:BLOCK, :] for kc in kcat]
    vprev = [vc[0:BLOCK, :] for vc in vcat]
    shift = mod_ref[0, 0, 0:1, :]
    scale = mod_ref[0, 0, 1:2, :]
    gate = mod_ref[0, 0, 2:3, :]
    piece = TM // ROW_PIECES
    nq = D_ATTN // 128
    cos = cos_ref[...]
    sin = sin_ref[...]
    hbs, ks, vs, zas, qrs = [], [], [], [], []
    for r0 in range(0, TM, piece):
        hbs.append((_ln(x_ref[0, r0:r0 + piece, :]) * (1.0 + scale) + shift).astype(BF16))
        pr = jnp.dot(hbs[-1], win_ref[0, :, O_Q:D_IN], preferred_element_type=F32)
        cos_p, sin_p = cos[r0:r0 + piece], sin[r0:r0 + piece]
        k_p = _rope(pr[:, O_K - O_Q:O_V - O_Q], cos_p, sin_p)
        v_p = pr[:, O_V - O_Q:O_ZA - O_Q]
        kcat[0][BLOCK + r0:BLOCK + r0 + piece, :] = k_p.astype(BF16)
        kcat[1][BLOCK + r0:BLOCK + r0 + piece, :] = _swap_halves(k_p).astype(BF16)
        vcat[0][BLOCK + r0:BLOCK + r0 + piece, 0:D_KV] = v_p.astype(BF16)
        vcat[1][BLOCK + r0:BLOCK + r0 + piece, 0:D_KV] = _swap_halves(v_p).astype(BF16)
        q_p = pr[:, 0:D_ATTN] * (HEAD_DIM ** -0.5 * LOG2E)
        qrs.append([_rope(q_p[:, 128 * c:128 * (c + 1)], cos_p, sin_p) for c in range(nq)])
        ks.append(k_p)
        vs.append(v_p)
        zas.append(pr[:, O_ZA - O_Q:D_IN - O_Q])
    hb = jnp.concatenate(hbs, axis=0)
    k = jnp.concatenate(ks, axis=0)
    v = jnp.concatenate(vs, axis=0)
    za = jnp.concatenate(zas, axis=0)
    qr = [jnp.concatenate([qp[c] for qp in qrs], axis=0) for c in range(nq)]

    def proj(lo, hi):
        return jnp.dot(hb, win_ref[0, :, lo:hi], preferred_element_type=F32)


    @pl.when(t == last)
    def _():
        ko_ref[0] = k[TM - WINDOW:TM, :]
        vo_ref[0] = v[TM - WINDOW:TM, :]

    row = lax.broadcasted_iota(jnp.int32, (BLOCK, 2 * BLOCK), 0)
    col = lax.broadcasted_iota(jnp.int32, (BLOCK, 2 * BLOCK), 1)
    rel = col - BLOCK - row
    band = (rel <= 0) & (rel > -WINDOW)
    band_first = band & (col + jnp.minimum(t, 1) * BLOCK >= BLOCK)
    lane_q = lax.broadcasted_iota(jnp.int32, (BLOCK, 2 * HEAD_DIM), 1)
    low = lane_q < HEAD_DIM

    same = [(c, (2 * c) // GROUP) for c in range(nq)]
    other = [(c, 1 - (2 * c) // GROUP) for c in range(nq)]
    sink_fill = sink_ref[0]
    vrow = lax.broadcasted_iota(jnp.int32, (2 * BLOCK, 2 * D_KV), 0)
    vcol = lax.broadcasted_iota(jnp.int32, (2 * BLOCK, 2 * D_KV), 1)
    sink_row = (vrow == 0) & (vcol < D_KV)
    nt_dims = (((1,), (1,)), ((), ()))
    def attn_block(i):
        mask = band if i > 0 else band_first
        scores = []
        for copy, heads in ((0, same), (1, other)):
            lhs = jnp.concatenate(
                [jnp.where(low == (half == 0), qr[c][i * BLOCK:(i + 1) * BLOCK], 0.0)
                 for c, half in heads], axis=0).astype(BF16)
            kk = (kcat[copy][i * BLOCK:(i + 2) * BLOCK, :] if i > 0 else
                  jnp.concatenate([kprev[copy], kcat[copy][BLOCK:2 * BLOCK, :]], axis=0))
            scores.append(lax.dot_general(lhs, kk, nt_dims, preferred_element_type=F32))
        s = jnp.concatenate(scores, axis=0).reshape(N_HEADS, BLOCK, 2 * BLOCK)
        s = jnp.where(mask[None], s, sink_fill)
        m = jnp.max(s, axis=-1, keepdims=True)
        p = jnp.exp2(s - m).astype(BF16)
        half_rows = (N_HEADS // 2) * BLOCK
        outs = []
        for copy in range(2):
            vv = (vcat[copy][i * BLOCK:(i + 2) * BLOCK, :] if i > 0 else
                  jnp.concatenate([vprev[copy], vcat[copy][BLOCK:2 * BLOCK, :]], axis=0))
            vv = jnp.where(sink_row, jnp.zeros((), BF16), vv)
            pv = jnp.dot(p[copy * (N_HEADS // 2):(copy + 1) * (N_HEADS // 2)].reshape(
                half_rows, 2 * BLOCK), vv,
                preferred_element_type=F32).reshape(N_HEADS // 2, BLOCK, 2 * D_KV)
            outs.append(pv[:, :, 0:D_KV] / pv[:, :, D_KV:])
        for c in range(nq):
            o = jnp.where(low == (same[c][1] == 0), outs[0][c], outs[1][c])
            zg = za[i * BLOCK:(i + 1) * BLOCK, 128 * c:128 * (c + 1)]
            c0 = D_CONV + D_POOL + 128 * c
            mixcat[i * BLOCK:(i + 1) * BLOCK, c0:c0 + 128] = (o * _silu(zg)).astype(BF16)

    a = proj(O_U, O_G) * _sigmoid(proj(O_G, O_ZC))
    acat[CARRY:CARRY + TM, :] = a
    zc = proj(O_ZC, O_PV)
    for r in range(1, SUBLANES):
        ashift[r - 1] = acat[r:r + TM + CARRY - SUBLANES, :]
    rows = 256

    def conv_chunk(c):
        acc = jnp.zeros((rows, D_CONV), F32)
        for j in range(CONV_WIDTH):
            off = CARRY - CONV_BUF + j
            r, lo = off % SUBLANES, c * rows + off - off % SUBLANES
            tap = acat[lo:lo + rows, :] if r == 0 else ashift[r - 1, lo:lo + rows, :]
            acc = acc + tap * convw_ref[0, j:j + 1, :]
        conv = acc + convb_ref[0]
        ya = _silu(_ln(conv) * cng_ref[0] + cnb_ref[0])
        mixcat[c * rows:(c + 1) * rows, 0:D_CONV] = (
            ya * _silu(zc[c * rows:(c + 1) * rows])).astype(BF16)

    pool_v = proj(O_PV, O_ZP)
    pcat[CARRY:CARRY + TM, :] = pool_v
    zp = proj(O_ZP, O_Q)

    for i in range(nb):
        attn_block(i)
    for c in range(TM // rows):
        conv_chunk(c)

    kcat[0][0:BLOCK, :] = k[TM - BLOCK:TM].astype(BF16)
    kcat[1][0:BLOCK, :] = _swap_halves(k[TM - BLOCK:TM]).astype(BF16)
    vcat[0][0:BLOCK, 0:D_KV] = v[TM - BLOCK:TM].astype(BF16)
    vcat[1][0:BLOCK, 0:D_KV] = _swap_halves(v[TM - BLOCK:TM]).astype(BF16)

    @pl.when(t == last)
    def _():
        convo_ref[0] = acat[CARRY + TM - CONV_BUF:CARRY + TM, :]

    acat[0:CARRY, :] = acat[TM:TM + CARRY, :]

    n = CARRY + TM
    ps_a[8:n, :] = pcat[8:n, :] + pcat[7:n - 1, :]
    ps_b[16:n, :] = ps_a[16:n, :] + ps_a[14:n - 2, :]
    s2 = ps_a[CARRY:n, :]
    s4 = ps_b[CARRY:n, :]
    ps_a[24:n, :] = ps_b[24:n, :] + ps_b[20:n - 4, :]
    s8 = ps_a[CARRY:n, :]
    s16 = s8 + ps_a[CARRY - 8:n - 8, :]
    lane = lax.broadcasted_iota(jnp.int32, (TM, D_POOL), 1)
    sums = jnp.where(lane < 64, s2, jnp.where(lane < 128, s4, jnp.where(lane < 192, s8, s16)))
    wlen = jnp.where(lane < 64, 2.0, jnp.where(lane < 128, 4.0, jnp.where(lane < 192, 8.0, 16.0)))
    pos1 = (lax.broadcasted_iota(jnp.int32, (TM, D_POOL), 0) + (t * TM + 1)).astype(F32)
    cnt = jnp.minimum(wlen, pos1)
    pooled = (sums / cnt - pool_v).astype(BF16)
    yb = jnp.dot(pooled, wpool_ref[0], preferred_element_type=F32) * pscale_ref[0]
    mixcat[:, D_CONV:D_CONV + D_POOL] = (yb * _silu(zp)).astype(BF16)

    @pl.when(t == last)
    def _():
        poolo_ref[0] = pcat[CARRY + TM - POOL_BUF:CARRY + TM, :]

    pcat[0:CARRY, :] = pcat[TM:TM + CARRY, :]

    g_res = (1.0 + gate) * (1.0 / ALPHA)
    piece = TM // OUT_PIECES
    for r0 in range(0, TM, piece):
        mix = jnp.dot(mixcat[r0:r0 + piece, :], wout_ref[0], preferred_element_type=F32)
        y = x_ref[0, r0:r0 + piece, :] + g_res * mix
        xo_ref[0, r0:r0 + piece, :] = _ln(y, LN_EPS / ALPHA ** 2) * lng_ref[0] + lnb_ref[0]


def _prompt_layer(lidx, sink_fill, x, mod_p, win, wout, conv_w, conv_b, cnorm_g, cnorm_b,
                  wpool, pool_scale, ln_g, ln_b, cos_t, sin_t):
    B, T, _ = x.shape
    nt = T // TM
    per_layer = lambda shape: pl.BlockSpec((1,) + shape, lambda b, t, l: (l[0],) + (0,) * len(shape))
    grid_spec = pltpu.PrefetchScalarGridSpec(
        num_scalar_prefetch=1,
        grid=(B, nt),
        in_specs=[
            per_layer((N_HEADS, 1, 2 * BLOCK)),
            pl.BlockSpec((1, TM, D_MODEL), lambda b, t, l: (b, t, 0)),
            pl.BlockSpec((1, 1, 3, D_MODEL), lambda b, t, l: (l[0], b, 0, 0)),
            per_layer((D_MODEL, D_IN)),
            per_layer((D_MODEL, D_MODEL)),
            per_layer((CONV_WIDTH, D_CONV)),
            per_layer((1, D_CONV)),
            per_layer((1, D_CONV)),
            per_layer((1, D_CONV)),
            per_layer((D_POOL, D_POOL)),
            per_layer((1, D_POOL)),
            per_layer((1, D_MODEL)),
            per_layer((1, D_MODEL)),
            pl.BlockSpec((TM, 128), lambda b, t, l: (t, 0)),
            pl.BlockSpec((TM, 128), lambda b, t, l: (t, 0)),
        ],
        out_specs=[
            pl.BlockSpec((1, TM, D_MODEL), lambda b, t, l: (b, t, 0)),
            pl.BlockSpec((1, CONV_BUF, D_CONV), lambda b, t, l: (b, 0, 0)),
            pl.BlockSpec((1, POOL_BUF, D_POOL), lambda b, t, l: (b, 0, 0)),
            pl.BlockSpec((1, WINDOW, D_KV), lambda b, t, l: (b, 0, 0)),
            pl.BlockSpec((1, WINDOW, D_KV), lambda b, t, l: (b, 0, 0)),
        ],
        scratch_shapes=[
            pltpu.VMEM((CARRY + TM, D_CONV), F32),
            pltpu.VMEM((SUBLANES - 1, CARRY + TM - SUBLANES, D_CONV), F32),
            pltpu.VMEM((CARRY + TM, D_POOL), F32),
            pltpu.VMEM((CARRY + TM, D_POOL), F32),
            pltpu.VMEM((CARRY + TM, D_POOL), F32),
            pltpu.VMEM((BLOCK + TM, D_KV), BF16),
            pltpu.VMEM((BLOCK + TM, D_KV), BF16),
            pltpu.VMEM((BLOCK + TM, 2 * D_KV), BF16),
            pltpu.VMEM((BLOCK + TM, 2 * D_KV), BF16),
            pltpu.VMEM((TM, D_MODEL), BF16),
        ],
    )
    out_shape = [
        jax.ShapeDtypeStruct((B, T, D_MODEL), F32),
        jax.ShapeDtypeStruct((B, CONV_BUF, D_CONV), F32),
        jax.ShapeDtypeStruct((B, POOL_BUF, D_POOL), F32),
        jax.ShapeDtypeStruct((B, WINDOW, D_KV), F32),
        jax.ShapeDtypeStruct((B, WINDOW, D_KV), F32),
    ]
    return pl.pallas_call(
        _prompt_kernel,
        grid_spec=grid_spec,
        out_shape=out_shape,
        compiler_params=pltpu.CompilerParams(
            dimension_semantics=("arbitrary", "arbitrary"), vmem_limit_bytes=VMEM_LIMIT),
    )(lidx, sink_fill, x, mod_p, win, wout, conv_w, conv_b, cnorm_g, cnorm_b,
      wpool, pool_scale, ln_g, ln_b, cos_t, sin_t)


def _decode_kernel(l_ref,
                   x_ref, mod_ref, win_ref, wout_ref, convw_ref, convb_ref, cng_ref, cnb_ref,
                   wpool_ref, pscale_ref, lng_ref, lnb_ref, cos_ref, sin_ref, sinkrow_ref,
                   cconv_ref, cpool_ref, ck_ref, cv_ref, *rest):
    xo_ref, convo_ref, poolo_ref, ko_ref, vo_ref, qs, ocat, kn_s, vn_s, kbt, vbt, knt, vnt = rest[-13:]
    gb = DEC_GROUP
    nt = x_ref.shape[0]
    x3 = x_ref[...]
    shift = mod_ref[0, :, 0:D_MODEL]
    scale = mod_ref[0, :, D_MODEL:2 * D_MODEL]
    gate = mod_ref[0, :, 2 * D_MODEL:3 * D_MODEL]
    h3 = _ln(x3) * (1.0 + scale)[None] + shift[None]
    hb = h3.reshape(nt * gb, D_MODEL).astype(BF16)

    def proj(lo, hi):
        return jnp.dot(hb, win_ref[0, :, lo:hi], preferred_element_type=F32)

    a = proj(O_U, O_G) * _sigmoid(proj(O_G, O_ZC))
    convo_ref[0, 0:CONV_BUF - nt] = cconv_ref[0, nt:CONV_BUF]
    convo_ref[0, CONV_BUF - nt:CONV_BUF] = a.reshape(nt, gb, D_CONV)
    zc = proj(O_ZC, O_PV)
    ya_rows = []
    for tt in range(nt):
        acc = jnp.zeros((gb, D_CONV), F32)
        for j in range(CONV_WIDTH):
            i = tt + j
            xi = (cconv_ref[0, i] if i < CONV_BUF
                  else a[(i - CONV_BUF) * gb:(i - CONV_BUF + 1) * gb])
            acc = acc + xi * convw_ref[0, j:j + 1, :]
        conv = acc + convb_ref[0]
        ya_rows.append(_silu(_ln(conv) * cng_ref[0] + cnb_ref[0]))
    ya = jnp.concatenate(ya_rows, axis=0) * _silu(zc)

    pv = proj(O_PV, O_ZP)
    poolo_ref[0, 0:POOL_BUF - nt] = cpool_ref[0, nt:POOL_BUF]
    poolo_ref[0, POOL_BUF - nt:POOL_BUF] = pv.reshape(nt, gb, D_POOL)
    lane = lax.broadcasted_iota(jnp.int32, (gb, D_POOL), 1)
    wlen = jnp.where(lane < 64, 2, jnp.where(lane < 128, 4, jnp.where(lane < 192, 8, 16)))
    pooled_rows = []
    for tt in range(nt):
        acc = jnp.zeros((gb, D_POOL), F32)
        for d in range(max(POOL_WINDOWS)):
            i = POOL_BUF + tt - d
            xi = (cpool_ref[0, i] if i < POOL_BUF
                  else pv[(i - POOL_BUF) * gb:(i - POOL_BUF + 1) * gb])
            acc = acc + (xi if d < min(POOL_WINDOWS) else jnp.where(wlen > d, xi, 0.0))
        cnt = jnp.minimum(wlen, PAST_LEN + tt + 1).astype(F32)
        pooled_rows.append(acc / cnt - pv[tt * gb:(tt + 1) * gb])
    pooled = jnp.concatenate(pooled_rows, axis=0).astype(BF16)
    yb = jnp.dot(pooled, wpool_ref[0], preferred_element_type=F32) * pscale_ref[0]
    yb = yb * _silu(proj(O_ZP, O_Q))

    cos3 = cos_ref[...]
    sin3 = sin_ref[...]
    cos = jnp.broadcast_to(cos3, (nt, gb, 128)).reshape(nt * gb, 128)
    sin = jnp.broadcast_to(sin3, (nt, gb, 128)).reshape(nt * gb, 128)
    k = _rope(proj(O_K, O_V), cos, sin)
    v = proj(O_V, O_ZA)
    kn_s[...] = k
    vn_s[...] = v
    q = proj(O_Q, O_K) * (HEAD_DIM ** -0.5)
    for g, qg in enumerate(_group_heads([q[:, 128 * c:128 * (c + 1)] for c in range(GROUP)])):
        qs[:, 128 * g:128 * (g + 1)] = _rope(qg, cos, sin)

    for new_rows, by_batch, new_t in ((k, kbt, knt), (v, vbt, vnt)):
        for tt in range(nt):
            by_batch[pl.ds(tt, gb, stride=nt), :] = new_rows[tt * gb:(tt + 1) * gb]
        new_t[...] = by_batch[...].T
    per_blk = 128 // nt
    lane_kv = lax.broadcasted_iota(jnp.int32, (D_KV, WINDOW), 1)

    def roll_cache(b):
        blk, j = divmod(b, per_blk)
        shift = (WINDOW - nt - nt * j) % WINDOW
        for new_t, cache_ref, out_ref in ((knt, ck_ref, ko_ref), (vnt, cv_ref, vo_ref)):
            old = pltpu.roll(cache_ref[0, b].reshape(D_KV, WINDOW), WINDOW - nt, axis=1)
            new = new_t[:, 128 * blk:128 * (blk + 1)]
            if shift:
                new = pltpu.roll(new, shift, axis=1)
            out_ref[0, b] = jnp.where(lane_kv >= WINDOW - nt, new, old).reshape(
                N_KV_HEADS, HEAD_DIM, WINDOW)

    nrow = nt * GROUP * N_KV_HEADS * SUB
    r1 = lax.broadcasted_iota(jnp.int32, (nrow, SUB * WINDOW), 0)
    c1 = lax.broadcasted_iota(jnp.int32, (nrow, SUB * WINDOW), 1)
    mask1 = ((c1 >> 7) == (r1 & (SUB - 1))) & ((c1 & (WINDOW - 1)) > (r1 >> 6))
    r2 = lax.broadcasted_iota(jnp.int32, (nrow, 128), 0)
    c2 = lax.broadcasted_iota(jnp.int32, (nrow, 128), 1)
    mask2 = ((c2 < nt * SUB) & ((c2 & (SUB - 1)) == (r2 & (SUB - 1)))
             & ((c2 >> 3) <= (r2 >> 6)))
    lane_q = lax.broadcasted_iota(jnp.int32, (SUB, 128), 1)
    low = lane_q < HEAD_DIM
    fill1 = jnp.where(c1 == 0, sinkrow_ref[0], MASK_VALUE)
    lane_c = lax.broadcasted_iota(jnp.int32, (D_KV, WINDOW), 1)
    ones_c = jnp.ones((D_KV, SUB * WINDOW), BF16)
    ones_n = jnp.ones((128, D_KV), BF16)
    zeros_n = jnp.zeros((128 - nt * SUB, D_KV), F32)

    def sub_block(sb):
        b0 = sb * SUB
        pieces = []
        for tt in range(nt):
            for g in range(GROUP):
                qp = qs[pl.ds(tt * gb + b0, SUB), 128 * g:128 * (g + 1)]
                pieces.append(jnp.where(low, qp, 0.0))
                pieces.append(jnp.where(low, 0.0, qp))
        lhs = jnp.concatenate(pieces, axis=0).astype(BF16)
        kblk = ck_ref[0, pl.ds(b0, SUB)]
        vblk = cv_ref[0, pl.ds(b0, SUB)]
        kc = jnp.concatenate(
            [kblk[bb].reshape(D_KV, WINDOW) for bb in range(SUB)], axis=1).astype(BF16)
        vc = jnp.concatenate(
            [jnp.where(lane_c == 0, 0.0, vblk[0].reshape(D_KV, WINDOW))]
            + [vblk[bb].reshape(D_KV, WINDOW) for bb in range(1, SUB)], axis=1).astype(BF16)
        kn = jnp.concatenate(
            [kn_s[pl.ds(tt * gb + b0, SUB), :] for tt in range(nt)] + [zeros_n],
            axis=0).astype(BF16)
        vn = jnp.concatenate(
            [vn_s[pl.ds(tt * gb + b0, SUB), :] for tt in range(nt)] + [zeros_n],
            axis=0).astype(BF16)
        nt_dims = (((1,), (1,)), ((), ()))
        s1 = jnp.where(mask1, jnp.dot(lhs, kc, preferred_element_type=F32), fill1)
        s2 = jnp.where(mask2, lax.dot_general(lhs, kn, nt_dims, preferred_element_type=F32),
                       MASK_VALUE)
        s = jnp.concatenate([s1, s2], axis=1)
        p = jnp.exp(s - jnp.max(s, axis=-1, keepdims=True)).astype(BF16)
        tot = (lax.dot_general(p[:, 0:SUB * WINDOW], jnp.concatenate([vc, ones_c], axis=0),
                               nt_dims, preferred_element_type=F32)
               + jnp.dot(p[:, SUB * WINDOW:], jnp.concatenate([vn, ones_n], axis=1),
                         preferred_element_type=F32))
        o_all = tot[:, 0:D_KV] / tot[:, D_KV:]
        for tt in range(nt):
            for g in range(GROUP):
                r0 = ((tt * GROUP + g) * N_KV_HEADS) * SUB
                o = jnp.where(low, o_all[r0:r0 + SUB], o_all[r0 + SUB:r0 + 2 * SUB])
                ocat[pl.ds(tt * gb + b0, SUB), 128 * g:128 * (g + 1)] = o

    for sb in range(gb // SUB):
        for b in range(sb * SUB, (sb + 1) * SUB):
            roll_cache(b)
        sub_block(sb)
    o_nat = _ungroup_heads([ocat[:, 128 * g:128 * (g + 1)] for g in range(GROUP)])
    yc = jnp.concatenate(o_nat, axis=-1) * _silu(proj(O_ZA, D_IN))

    mixcat = jnp.concatenate([ya, yb, yc], axis=-1).astype(BF16)
    mix = jnp.dot(mixcat, wout_ref[0], preferred_element_type=F32).reshape(nt, gb, D_MODEL)
    y = ALPHA * x3 + (1.0 + gate)[None] * mix
    xo_ref[...] = _ln(y) * lng_ref[0][None] + lnb_ref[0][None]


def _decode_layer(lidx, x, mod_s, win, wout, conv_w, conv_b, cnorm_g, cnorm_b, wpool, pool_scale,
                  ln_g, ln_b, cos_s, sin_s, sink_rows, cache_conv, cache_pool, cache_k, cache_v,
                  rolled):
    nt, nbatch, _ = x.shape
    gb = DEC_GROUP
    carried = list(rolled)
    per_layer = lambda shape: pl.BlockSpec((1,) + shape, lambda g, l: (l[0],) + (0,) * len(shape))
    tok = lambda c: pl.BlockSpec((nt, gb, c), lambda g, l: (0, g, 0))
    time_major = lambda r, c: pl.BlockSpec((1, r, gb, c), lambda g, l: (l[0], 0, g, 0))
    kv_blk = pl.BlockSpec((1, gb, N_KV_HEADS, HEAD_DIM, WINDOW), lambda g, l: (l[0], g, 0, 0, 0))
    nrow = nt * GROUP * N_KV_HEADS * SUB
    grid_spec = pltpu.PrefetchScalarGridSpec(
        num_scalar_prefetch=1,
        grid=(nbatch // gb,),
        in_specs=[
            tok(D_MODEL),
            pl.BlockSpec((1, gb, 3 * D_MODEL), lambda g, l: (l[0], g, 0)),
            per_layer((D_MODEL, D_IN)),
            per_layer((D_MODEL, D_MODEL)),
            per_layer((CONV_WIDTH, D_CONV)),
            per_layer((1, D_CONV)),
            per_layer((1, D_CONV)),
            per_layer((1, D_CONV)),
            per_layer((D_POOL, D_POOL)),
            per_layer((1, D_POOL)),
            per_layer((1, D_MODEL)),
            per_layer((1, D_MODEL)),
            pl.BlockSpec((nt, 1, 128), lambda g, l: (0, 0, 0)),
            pl.BlockSpec((nt, 1, 128), lambda g, l: (0, 0, 0)),
            per_layer((nrow, 1)),
            time_major(CONV_BUF, D_CONV),
            time_major(POOL_BUF, D_POOL),
            kv_blk,
            kv_blk,
        ] + [pl.BlockSpec(memory_space=pl.ANY)] * len(carried),
        out_specs=[tok(D_MODEL), time_major(CONV_BUF, D_CONV), time_major(POOL_BUF, D_POOL),
                   kv_blk, kv_blk],
        scratch_shapes=[
            pltpu.VMEM((nt * gb, D_ATTN), F32),
            pltpu.VMEM((nt * gb, D_ATTN), F32),
            pltpu.VMEM((nt * gb, D_KV), F32),
            pltpu.VMEM((nt * gb, D_KV), F32),
            pltpu.VMEM((nt * gb, D_KV), F32),
            pltpu.VMEM((nt * gb, D_KV), F32),
            pltpu.VMEM((D_KV, nt * gb), F32),
            pltpu.VMEM((D_KV, nt * gb), F32),
        ],
    )
    out_shape = [jax.ShapeDtypeStruct((nt, nbatch, D_MODEL), F32)] + [
        jax.ShapeDtypeStruct(c.shape, F32) for c in (cache_conv, cache_pool, cache_k, cache_v)]
    operands = (lidx, x, mod_s, win, wout, conv_w, conv_b, cnorm_g, cnorm_b, wpool, pool_scale,
                ln_g, ln_b, cos_s, sin_s, sink_rows, cache_conv, cache_pool, cache_k, cache_v)
    n_in = len(operands)
    return pl.pallas_call(
        _decode_kernel,
        grid_spec=grid_spec,
        out_shape=out_shape,
        input_output_aliases={n_in + i: 1 + i for i in range(len(carried))},
        compiler_params=pltpu.CompilerParams(
            dimension_semantics=("arbitrary",), vmem_limit_bytes=VMEM_LIMIT),
    )(*operands, *carried)


def _rope_tables(pos):
    half = HEAD_DIM // 2
    inv_freq = ROPE_THETA ** (-jnp.arange(half, dtype=F32) * (2.0 / HEAD_DIM))
    ang = pos.astype(F32)[:, None] * inv_freq[None, :]
    cos = jnp.cos(ang)
    sin = jnp.sin(ang)
    cos_t = jnp.concatenate([cos, cos, cos, cos], axis=-1)
    sin_t = jnp.concatenate([-sin, sin, -sin, sin], axis=-1)
    return cos_t, sin_t


def kernel(x_prompt, x_sample, cache_conv, cache_pool, cache_k, cache_v, c_prompt, c_sample,
           w_in, w_out, conv_w, conv_b, cnorm_g, cnorm_b, pool_w, pool_scale, sinks,
           w_mod, b_mod, ln_g, ln_b):
    B, T, _ = x_prompt.shape
    nbatch, nt, _ = x_sample.shape

    win = w_in.astype(BF16)
    wout = w_out.astype(BF16)
    wpool = jnp.zeros((DEPTH, D_POOL, D_POOL), F32)
    for gi in range(len(POOL_WINDOWS)):
        sl = slice(gi * POOL_GROUP_DIM, (gi + 1) * POOL_GROUP_DIM)
        wpool = wpool.at[:, sl, sl].set(pool_w[:, gi])
    wpool = wpool.astype(BF16)
    sinks_gh = sinks.reshape(DEPTH, N_KV_HEADS, GROUP).transpose(0, 2, 1)
    nq = D_ATTN // 128
    order = ([2 * c + (2 * c) // GROUP for c in range(nq)]
             + [2 * c + 1 - (2 * c) // GROUP for c in range(nq)])
    sinks_ord = jnp.stack([sinks[:, hd] for hd in order], axis=1)
    sink_fill = jnp.concatenate(
        [sinks_ord[:, :, None, None] * LOG2E,
         jnp.full((DEPTH, N_HEADS, 1, 2 * BLOCK - 1), MASK_VALUE, F32)], axis=-1)
    sink_rows = jnp.broadcast_to(
        sinks_gh[:, None, :, :, None], (DEPTH, nt, GROUP, N_KV_HEADS, SUB)
    ).reshape(DEPTH, nt * GROUP * N_KV_HEADS * SUB, 1)

    r3 = lambda p: p.reshape(DEPTH, 1, -1)
    conv_b3, cng3, cnb3, pscale3, lng3, lnb3 = map(r3, (conv_b, cnorm_g, cnorm_b, pool_scale, ln_g, ln_b))

    cos_p, sin_p = _rope_tables(jnp.arange(T, dtype=jnp.int32))
    cos_s, sin_s = _rope_tables(PAST_LEN + jnp.arange(nt, dtype=jnp.int32))
    cos_s = cos_s.reshape(nt, 1, 128)
    sin_s = sin_s.reshape(nt, 1, 128)

    mod = _modulation(jnp.concatenate([c_sample, c_prompt], axis=0), w_mod, b_mod)
    mod_p = mod[:, nbatch:].reshape(DEPTH, B, 3, D_MODEL)

    cconv_t = cache_conv.transpose(0, 2, 1, 3)
    cpool_t = cache_pool.transpose(0, 2, 1, 3)
    ck_t = cache_k.transpose(0, 1, 3, 4, 2)
    cv_t = cache_v.transpose(0, 1, 3, 4, 2)

    xp = x_prompt
    xs = x_sample.transpose(1, 0, 2)
    conv_p, pool_p, k_p, v_p = ([] for _ in range(4))
    rolled = ()
    for l in range(DEPTH):
        lidx = jnp.full((1,), l, jnp.int32)
        xp, cp, pp, kp, vp = _prompt_layer(
            lidx, sink_fill, xp, mod_p, win, wout, conv_w, conv_b3, cng3, cnb3, wpool, pscale3,
            lng3, lnb3, cos_p, sin_p)
        xs, *rolled = _decode_layer(
            lidx, xs, mod, win, wout, conv_w, conv_b3, cng3, cnb3, wpool, pscale3,
            lng3, lnb3, cos_s, sin_s, sink_rows, cconv_t, cpool_t, ck_t, cv_t, rolled)
        conv_p.append(cp); pool_p.append(pp); k_p.append(kp); v_p.append(vp)

    kv5 = lambda z: z.reshape(z.shape[:-1] + (N_KV_HEADS, HEAD_DIM))
    conv_s, pool_s, k_s, v_s = rolled
    return (xp, xs.transpose(1, 0, 2),
            jnp.stack(conv_p), jnp.stack(pool_p), kv5(jnp.stack(k_p)), kv5(jnp.stack(v_p)),
            conv_s.transpose(0, 2, 1, 3), pool_s.transpose(0, 2, 1, 3),
            k_s.transpose(0, 1, 4, 2, 3), v_s.transpose(0, 1, 4, 2, 3))
```

```python
import jax
import jax.numpy as jnp
from jax import lax
from jax.experimental import pallas as pl
from jax.experimental.pallas import tpu as pltpu

F32 = jnp.float32
BF16 = jnp.bfloat16

D_MODEL = 1024
DEPTH = 4
D_CONV = 256
D_POOL = 256
D_ATTN = 512
HEAD_DIM = 64
N_HEADS = 8
N_KV_HEADS = 2
GROUP = 4
D_KV = 128
WINDOW = 128
BLOCK = 128
CONV_WIDTH = 31
CONV_BUF = 30
POOL_WINDOWS = (2, 4, 8, 16)
POOL_GROUP_DIM = 64
POOL_BUF = 15
ROPE_THETA = 10000.0
LN_EPS = 1e-5
ALPHA = (2.0 * DEPTH) ** 0.25
LOG2E = 1.4426950408889634
MASK_VALUE = -1e30
PAST_LEN = 8192

O_U, O_G, O_ZC, O_PV, O_ZP, O_Q, O_K, O_V, O_ZA, D_IN = (
    0, 256, 512, 768, 1024, 1280, 1792, 1920, 2048, 2560)

SUBLANES = 8
CARRY = 32
TM = 1024
ROW_PIECES = 4
OUT_PIECES = 4
DEC_GROUP = 32
SUB = 8
VMEM_LIMIT = 48 * 1024 * 1024


def _sigmoid(x):
    return 1.0 / (1.0 + jnp.exp(-x))


def _silu(x):
    return x * _sigmoid(x)


def _ln(x, eps=LN_EPS):
    mu = jnp.mean(x, axis=-1, keepdims=True)
    xc = x - mu
    var = jnp.mean(xc * xc, axis=-1, keepdims=True)
    return xc * lax.rsqrt(var + eps)


def _rope(x, cos, sin_signed):
    lane = lax.broadcasted_iota(jnp.int32, x.shape, 1)
    first_half = (lane & 63) < 32
    rot = jnp.where(first_half, pltpu.roll(x, 96, axis=1), pltpu.roll(x, 32, axis=1))
    return x * cos + rot * sin_signed


def _swap_halves(x):
    return pltpu.roll(x, HEAD_DIM, axis=1)


def _group_heads(nat):
    low = lax.broadcasted_iota(jnp.int32, nat[0].shape, 1) < HEAD_DIM
    out = []
    for g in range(GROUP):
        a, b = nat[g // 2], nat[GROUP // 2 + g // 2]
        out.append(jnp.where(low, a, _swap_halves(b)) if g % 2 == 0
                   else jnp.where(low, _swap_halves(a), b))
    return out


def _ungroup_heads(grouped):
    low = lax.broadcasted_iota(jnp.int32, grouped[0].shape, 1) < HEAD_DIM
    nat = [None] * GROUP
    for c in range(GROUP // 2):
        even, odd = grouped[2 * c], grouped[2 * c + 1]
        nat[c] = jnp.where(low, even, _swap_halves(odd))
        nat[GROUP // 2 + c] = jnp.where(low, _swap_halves(even), odd)
    return nat


def _mod_kernel(c_ref, w_ref, b_ref, o_ref):
    c = c_ref[...]
    sc = _silu(c).astype(BF16)
    o_ref[0] = jnp.dot(sc, w_ref[0].astype(BF16), preferred_element_type=F32) + b_ref[0]


def _modulation(c_all, w_mod, b_mod):
    n = c_all.shape[0]
    tn = 3 * D_MODEL
    return pl.pallas_call(
        _mod_kernel,
        grid=(DEPTH, 3 * D_MODEL // tn),
        in_specs=[
            pl.BlockSpec((n, D_MODEL), lambda l, j: (0, 0)),
            pl.BlockSpec((1, D_MODEL, tn), lambda l, j: (l, 0, j)),
            pl.BlockSpec((1, 1, tn), lambda l, j: (l, 0, j)),
        ],
        out_specs=pl.BlockSpec((1, n, tn), lambda l, j: (l, 0, j)),
        out_shape=jax.ShapeDtypeStruct((DEPTH, n, 3 * D_MODEL), F32),
        compiler_params=pltpu.CompilerParams(
            dimension_semantics=("arbitrary", "arbitrary"), vmem_limit_bytes=VMEM_LIMIT),
    )(c_all, w_mod, b_mod.reshape(DEPTH, 1, 3 * D_MODEL))


def _prompt_kernel(l_ref, sink_ref,
                   x_ref, mod_ref, win_ref, wout_ref, convw_ref, convb_ref, cng_ref, cnb_ref,
                   wpool_ref, pscale_ref, lng_ref, lnb_ref, cos_ref, sin_ref,
                   xo_ref, convo_ref, poolo_ref, ko_ref, vo_ref,
                   acat, ashift, pcat, ps_a, ps_b, kcat0, kcat1, vcat0, vcat1, mixcat):
    del l_ref
    kcat, vcat = (kcat0, kcat1), (vcat0, vcat1)
    t = pl.program_id(1)
    last = pl.num_programs(1) - 1
    nb = TM // BLOCK

    @pl.when(t == 0)
    def _():
        acat[0:CARRY, :] = jnp.zeros((CARRY, D_CONV), F32)
        pcat[0:CARRY, :] = jnp.zeros((CARRY, D_POOL), F32)
        for kc, vc in zip(kcat, vcat):
            kc[0:BLOCK, :] = jnp.zeros((BLOCK, D_KV), BF16)
            vc[0:BLOCK, 0:D_KV] = jnp.zeros((BLOCK, D_KV), BF16)
            vc[:, D_KV:2 * D_KV] = jnp.ones((BLOCK + TM, D_KV), BF16)

    kprev = [kc[0:BLOCK, :] for kc in kcat]
    vprev = [vc[0:BLOCK, :] for vc in vcat]
    shift = mod_ref[0, 0, 0:1, :]
    scale = mod_ref[0, 0, 1:2, :]
    gate = mod_ref[0, 0, 2:3, :]
    piece = TM // ROW_PIECES
    nq = D_ATTN // 128
    cos = cos_ref[...]
    sin = sin_ref[...]
    hbs, ks, vs, zas, qrs = [], [], [], [], []
    for r0 in range(0, TM, piece):
        hbs.append((_ln(x_ref[0, r0:r0 + piece, :]) * (1.0 + scale) + shift).astype(BF16))
        pr = jnp.dot(hbs[-1], win_ref[0, :, O_Q:D_IN], preferred_element_type=F32)
        cos_p, sin_p = cos[r0:r0 + piece], sin[r0:r0 + piece]
        k_p = _rope(pr[:, O_K - O_Q:O_V - O_Q], cos_p, sin_p)
        v_p = pr[:, O_V - O_Q:O_ZA - O_Q]
        kcat[0][BLOCK + r0:BLOCK + r0 + piece, :] = k_p.astype(BF16)
        kcat[1][BLOCK + r0:BLOCK + r0 + piece, :] = _swap_halves(k_p).astype(BF16)
        vcat[0][BLOCK + r0:BLOCK + r0 + piece, 0:D_KV] = v_p.astype(BF16)
        vcat[1][BLOCK + r0:BLOCK + r0 + piece, 0:D_KV] = _swap_halves(v_p).astype(BF16)
        q_p = pr[:, 0:D_ATTN] * (HEAD_DIM ** -0.5 * LOG2E)
        qrs.append([_rope(q_p[:, 128 * c:128 * (c + 1)], cos_p, sin_p) for c in range(nq)])
        ks.append(k_p)
        vs.append(v_p)
        zas.append(_silu(pr[:, O_ZA - O_Q:D_IN - O_Q]))
    hb = jnp.concatenate(hbs, axis=0)
    k = jnp.concatenate(ks, axis=0)
    v = jnp.concatenate(vs, axis=0)
    za = jnp.concatenate(zas, axis=0)
    qr = [jnp.concatenate([qp[c] for qp in qrs], axis=0) for c in range(nq)]

    def proj(lo, hi):
        return jnp.dot(hb, win_ref[0, :, lo:hi], preferred_element_type=F32)


    @pl.when(t == last)
    def _():
        ko_ref[0] = k[TM - WINDOW:TM, :]
        vo_ref[0] = v[TM - WINDOW:TM, :]

    row = lax.broadcasted_iota(jnp.int32, (BLOCK, 2 * BLOCK), 0)
    col = lax.broadcasted_iota(jnp.int32, (BLOCK, 2 * BLOCK), 1)
    rel = col - BLOCK - row
    band = (rel <= 0) & (rel > -WINDOW)
    band_first = band & (col + jnp.minimum(t, 1) * BLOCK >= BLOCK)
    lane_q = lax.broadcasted_iota(jnp.int32, (BLOCK, 2 * HEAD_DIM), 1)
    low = lane_q < HEAD_DIM

    same = [(c, (2 * c) // GROUP) for c in range(nq)]
    other = [(c, 1 - (2 * c) // GROUP) for c in range(nq)]
    sink_fill = sink_ref[0]
    vrow = lax.broadcasted_iota(jnp.int32, (2 * BLOCK, 2 * D_KV), 0)
    vcol = lax.broadcasted_iota(jnp.int32, (2 * BLOCK, 2 * D_KV), 1)
    sink_row = (vrow == 0) & (vcol < D_KV)
    nt_dims = (((1,), (1,)), ((), ()))
    def attn_block(i):
        mask = band if i > 0 else band_first
        scores = []
        for copy, heads in ((0, same), (1, other)):
            lhs = jnp.concatenate(
                [jnp.where(low == (half == 0), qr[c][i * BLOCK:(i + 1) * BLOCK], 0.0)
                 for c, half in heads], axis=0).astype(BF16)
            kk = (kcat[copy][i * BLOCK:(i + 2) * BLOCK, :] if i > 0 else
                  jnp.concatenate([kprev[copy], kcat[copy][BLOCK:2 * BLOCK, :]], axis=0))
            scores.append(lax.dot_general(lhs, kk, nt_dims, preferred_element_type=F32))
        s = jnp.concatenate(scores, axis=0).reshape(N_HEADS, BLOCK, 2 * BLOCK)
        s = jnp.where(mask[None], s, sink_fill)
        m = jnp.max(s, axis=-1, keepdims=True)
        p = jnp.exp2(s - m).astype(BF16)
        half_rows = (N_HEADS // 2) * BLOCK
        outs = []
        for copy in range(2):
            vv = (vcat[copy][i * BLOCK:(i + 2) * BLOCK, :] if i > 0 else
                  jnp.concatenate([vprev[copy], vcat[copy][BLOCK:2 * BLOCK, :]], axis=0))
            vv = jnp.where(sink_row, jnp.zeros((), BF16), vv)
            pv = jnp.dot(p[copy * (N_HEADS // 2):(copy + 1) * (N_HEADS // 2)].reshape(
                half_rows, 2 * BLOCK), vv,
                preferred_element_type=F32).reshape(N_HEADS // 2, BLOCK, 2 * D_KV)
            outs.append(pv[:, :, 0:D_KV] / pv[:, :, D_KV:])
        for c in range(nq):
            o = jnp.where(low == (same[c][1] == 0), outs[0][c], outs[1][c])
            zg = za[i * BLOCK:(i + 1) * BLOCK, 128 * c:128 * (c + 1)]
            c0 = D_CONV + D_POOL + 128 * c
            mixcat[i * BLOCK:(i + 1) * BLOCK, c0:c0 + 128] = (o * zg).astype(BF16)

    a = proj(O_U, O_G) * _sigmoid(proj(O_G, O_ZC))
    acat[CARRY:CARRY + TM, :] = a
    zc = proj(O_ZC, O_PV)
    for r in range(1, SUBLANES):
        ashift[r - 1] = acat[r:r + TM + CARRY - SUBLANES, :]
    rows = 256

    def conv_chunk(c):
        acc = jnp.zeros((rows, D_CONV), F32)
        for j in range(CONV_WIDTH):
            off = CARRY - CONV_BUF + j
            r, lo = off % SUBLANES, c * rows + off - off % SUBLANES
            tap = acat[lo:lo + rows, :] if r == 0 else ashift[r - 1, lo:lo + rows, :]
            acc = acc + tap * convw_ref[0, j:j + 1, :]
        conv = acc + convb_ref[0]
        ya = _silu(_ln(conv) * cng_ref[0] + cnb_ref[0])
        mixcat[c * rows:(c + 1) * rows, 0:D_CONV] = (
            ya * _silu(zc[c * rows:(c + 1) * rows])).astype(BF16)

    pool_v = proj(O_PV, O_ZP)
    pcat[CARRY:CARRY + TM, :] = pool_v
    zp = proj(O_ZP, O_Q)

    for i in range(nb):
        attn_block(i)
    for c in range(TM // rows):
        conv_chunk(c)

    kcat[0][0:BLOCK, :] = k[TM - BLOCK:TM].astype(BF16)
    kcat[1][0:BLOCK, :] = _swap_halves(k[TM - BLOCK:TM]).astype(BF16)
    vcat[0][0:BLOCK, 0:D_KV] = v[TM - BLOCK:TM].astype(BF16)
    vcat[1][0:BLOCK, 0:D_KV] = _swap_halves(v[TM - BLOCK:TM]).astype(BF16)

    @pl.when(t == last)
    def _():
        convo_ref[0] = acat[CARRY + TM - CONV_BUF:CARRY + TM, :]

    acat[0:CARRY, :] = acat[TM:TM + CARRY, :]

    n = CARRY + TM
    ps_a[8:n, :] = pcat[8:n, :] + pcat[7:n - 1, :]
    ps_b[16:n, :] = ps_a[16:n, :] + ps_a[14:n - 2, :]
    s2 = ps_a[CARRY:n, :]
    s4 = ps_b[CARRY:n, :]
    ps_a[24:n, :] = ps_b[24:n, :] + ps_b[20:n - 4, :]
    s8 = ps_a[CARRY:n, :]
    s16 = s8 + ps_a[CARRY - 8:n - 8, :]
    lane = lax.broadcasted_iota(jnp.int32, (TM, D_POOL), 1)
    sums = jnp.where(lane < 64, s2, jnp.where(lane < 128, s4, jnp.where(lane < 192, s8, s16)))
    wlen = jnp.where(lane < 64, 2.0, jnp.where(lane < 128, 4.0, jnp.where(lane < 192, 8.0, 16.0)))
    pos1 = (lax.broadcasted_iota(jnp.int32, (TM, D_POOL), 0) + (t * TM + 1)).astype(F32)
    cnt = jnp.minimum(wlen, pos1)
    pooled = (sums / cnt - pool_v).astype(BF16)
    yb = jnp.dot(pooled, wpool_ref[0], preferred_element_type=F32) * pscale_ref[0]
    mixcat[:, D_CONV:D_CONV + D_POOL] = (yb * _silu(zp)).astype(BF16)

    @pl.when(t == last)
    def _():
        poolo_ref[0] = pcat[CARRY + TM - POOL_BUF:CARRY + TM, :]

    pcat[0:CARRY, :] = pcat[TM:TM + CARRY, :]

    g_res = (1.0 + gate) * (1.0 / ALPHA)
    piece = TM // OUT_PIECES
    for r0 in range(0, TM, piece):
        mix = jnp.dot(mixcat[r0:r0 + piece, :], wout_ref[0], preferred_element_type=F32)
        y = x_ref[0, r0:r0 + piece, :] + g_res * mix
        xo_ref[0, r0:r0 + piece, :] = _ln(y, LN_EPS / ALPHA ** 2) * lng_ref[0] + lnb_ref[0]


def _prompt_layer(lidx, sink_fill, x, mod_p, win, wout, conv_w, conv_b, cnorm_g, cnorm_b,
                  wpool, pool_scale, ln_g, ln_b, cos_t, sin_t):
    B, T, _ = x.shape
    nt = T // TM
    per_layer = lambda shape: pl.BlockSpec((1,) + shape, lambda b, t, l: (l[0],) + (0,) * len(shape))
    grid_spec = pltpu.PrefetchScalarGridSpec(
        num_scalar_prefetch=1,
        grid=(B, nt),
        in_specs=[
            per_layer((N_HEADS, 1, 2 * BLOCK)),
            pl.BlockSpec((1, TM, D_MODEL), lambda b, t, l: (b, t, 0)),
            pl.BlockSpec((1, 1, 3, D_MODEL), lambda b, t, l: (l[0], b, 0, 0)),
            per_layer((D_MODEL, D_IN)),
            per_layer((D_MODEL, D_MODEL)),
            per_layer((CONV_WIDTH, D_CONV)),
            per_layer((1, D_CONV)),
            per_layer((1, D_CONV)),
            per_layer((1, D_CONV)),
            per_layer((D_POOL, D_POOL)),
            per_layer((1, D_POOL)),
            per_layer((1, D_MODEL)),
            per_layer((1, D_MODEL)),
            pl.BlockSpec((TM, 128), lambda b, t, l: (t, 0)),
            pl.BlockSpec((TM, 128), lambda b, t, l: (t, 0)),
        ],
        out_specs=[
            pl.BlockSpec((1, TM, D_MODEL), lambda b, t, l: (b, t, 0)),
            pl.BlockSpec((1, CONV_BUF, D_CONV), lambda b, t, l: (b, 0, 0)),
            pl.BlockSpec((1, POOL_BUF, D_POOL), lambda b, t, l: (b, 0, 0)),
            pl.BlockSpec((1, WINDOW, D_KV), lambda b, t, l: (b, 0, 0)),
            pl.BlockSpec((1, WINDOW, D_KV), lambda b, t, l: (b, 0, 0)),
        ],
        scratch_shapes=[
            pltpu.VMEM((CARRY + TM, D_CONV), F32),
            pltpu.VMEM((SUBLANES - 1, CARRY + TM - SUBLANES, D_CONV), F32),
            pltpu.VMEM((CARRY + TM, D_POOL), F32),
            pltpu.VMEM((CARRY + TM, D_POOL), F32),
            pltpu.VMEM((CARRY + TM, D_POOL), F32),
            pltpu.VMEM((BLOCK + TM, D_KV), BF16),
            pltpu.VMEM((BLOCK + TM, D_KV), BF16),
            pltpu.VMEM((BLOCK + TM, 2 * D_KV), BF16),
            pltpu.VMEM((BLOCK + TM, 2 * D_KV), BF16),
            pltpu.VMEM((TM, D_MODEL), BF16),
        ],
    )
    out_shape = [
        jax.ShapeDtypeStruct((B, T, D_MODEL), F32),
        jax.ShapeDtypeStruct((B, CONV_BUF, D_CONV), F32),
        jax.ShapeDtypeStruct((B, POOL_BUF, D_POOL), F32),
        jax.ShapeDtypeStruct((B, WINDOW, D_KV), F32),
        jax.ShapeDtypeStruct((B, WINDOW, D_KV), F32),
    ]
    return pl.pallas_call(
        _prompt_kernel,
        grid_spec=grid_spec,
        out_shape=out_shape,
        compiler_params=pltpu.CompilerParams(
            dimension_semantics=("arbitrary", "arbitrary"), vmem_limit_bytes=VMEM_LIMIT),
    )(lidx, sink_fill, x, mod_p, win, wout, conv_w, conv_b, cnorm_g, cnorm_b,
      wpool, pool_scale, ln_g, ln_b, cos_t, sin_t)


def _decode_kernel(l_ref,
                   x_ref, mod_ref, win_ref, wout_ref, convw_ref, convb_ref, cng_ref, cnb_ref,
                   wpool_ref, pscale_ref, lng_ref, lnb_ref, cos_ref, sin_ref, sinkrow_ref,
                   cconv_ref, cpool_ref, ck_ref, cv_ref, *rest):
    xo_ref, convo_ref, poolo_ref, ko_ref, vo_ref, qs, ocat, kn_s, vn_s, kbt, vbt, knt, vnt = rest[-13:]
    gb = DEC_GROUP
    nt = x_ref.shape[0]
    x3 = x_ref[...]
    shift = mod_ref[0, :, 0:D_MODEL]
    scale = mod_ref[0, :, D_MODEL:2 * D_MODEL]
    gate = mod_ref[0, :, 2 * D_MODEL:3 * D_MODEL]
    h3 = _ln(x3) * (1.0 + scale)[None] + shift[None]
    hb = h3.reshape(nt * gb, D_MODEL).astype(BF16)

    def proj(lo, hi):
        return jnp.dot(hb, win_ref[0, :, lo:hi], preferred_element_type=F32)

    a = proj(O_U, O_G) * _sigmoid(proj(O_G, O_ZC))
    convo_ref[0, 0:CONV_BUF - nt] = cconv_ref[0, nt:CONV_BUF]
    convo_ref[0, CONV_BUF - nt:CONV_BUF] = a.reshape(nt, gb, D_CONV)
    zc = proj(O_ZC, O_PV)
    ya_rows = []
    for tt in range(nt):
        acc = jnp.zeros((gb, D_CONV), F32)
        for j in range(CONV_WIDTH):
            i = tt + j
            xi = (cconv_ref[0, i] if i < CONV_BUF
                  else a[(i - CONV_BUF) * gb:(i - CONV_BUF + 1) * gb])
            acc = acc + xi * convw_ref[0, j:j + 1, :]
        conv = acc + convb_ref[0]
        ya_rows.append(_silu(_ln(conv) * cng_ref[0] + cnb_ref[0]))
    ya = jnp.concatenate(ya_rows, axis=0) * _silu(zc)

    pv = proj(O_PV, O_ZP)
    poolo_ref[0, 0:POOL_BUF - nt] = cpool_ref[0, nt:POOL_BUF]
    poolo_ref[0, POOL_BUF - nt:POOL_BUF] = pv.reshape(nt, gb, D_POOL)
    lane = lax.broadcasted_iota(jnp.int32, (gb, D_POOL), 1)
    wlen = jnp.where(lane < 64, 2, jnp.where(lane < 128, 4, jnp.where(lane < 192, 8, 16)))
    pooled_rows = []
    for tt in range(nt):
        acc = jnp.zeros((gb, D_POOL), F32)
        for d in range(max(POOL_WINDOWS)):
            i = POOL_BUF + tt - d
            xi = (cpool_ref[0, i] if i < POOL_BUF
                  else pv[(i - POOL_BUF) * gb:(i - POOL_BUF + 1) * gb])
            acc = acc + (xi if d < min(POOL_WINDOWS) else jnp.where(wlen > d, xi, 0.0))
        cnt = jnp.minimum(wlen, PAST_LEN + tt + 1).astype(F32)
        pooled_rows.append(acc / cnt - pv[tt * gb:(tt + 1) * gb])
    pooled = jnp.concatenate(pooled_rows, axis=0).astype(BF16)
    yb = jnp.dot(pooled, wpool_ref[0], preferred_element_type=F32) * pscale_ref[0]
    yb = yb * _silu(proj(O_ZP, O_Q))

    cos3 = cos_ref[...]
    sin3 = sin_ref[...]
    cos = jnp.broadcast_to(cos3, (nt, gb, 128)).reshape(nt * gb, 128)
    sin = jnp.broadcast_to(sin3, (nt, gb, 128)).reshape(nt * gb, 128)
    k = _rope(proj(O_K, O_V), cos, sin)
    v = proj(O_V, O_ZA)
    kn_s[...] = k
    vn_s[...] = v
    q = proj(O_Q, O_K) * (HEAD_DIM ** -0.5)
    for g, qg in enumerate(_group_heads([q[:, 128 * c:128 * (c + 1)] for c in range(GROUP)])):
        qs[:, 128 * g:128 * (g + 1)] = _rope(qg, cos, sin)

    for new_rows, by_batch, new_t in ((k, kbt, knt), (v, vbt, vnt)):
        for tt in range(nt):
            by_batch[pl.ds(tt, gb, stride=nt), :] = new_rows[tt * gb:(tt + 1) * gb]
        new_t[...] = by_batch[...].T
    per_blk = 128 // nt
    lane_kv = lax.broadcasted_iota(jnp.int32, (D_KV, WINDOW), 1)

    def roll_cache(b):
        blk, j = divmod(b, per_blk)
        shift = (WINDOW - nt - nt * j) % WINDOW
        for new_t, cache_ref, out_ref in ((knt, ck_ref, ko_ref), (vnt, cv_ref, vo_ref)):
            old = pltpu.roll(cache_ref[0, b].reshape(D_KV, WINDOW), WINDOW - nt, axis=1)
            new = new_t[:, 128 * blk:128 * (blk + 1)]
            if shift:
                new = pltpu.roll(new, shift, axis=1)
            out_ref[0, b] = jnp.where(lane_kv >= WINDOW - nt, new, old).reshape(
                N_KV_HEADS, HEAD_DIM, WINDOW)

    nrow = nt * GROUP * N_KV_HEADS * SUB
    r1 = lax.broadcasted_iota(jnp.int32, (nrow, SUB * WINDOW), 0)
    c1 = lax.broadcasted_iota(jnp.int32, (nrow, SUB * WINDOW), 1)
    mask1 = ((c1 >> 7) == (r1 & (SUB - 1))) & ((c1 & (WINDOW - 1)) > (r1 >> 6))
    r2 = lax.broadcasted_iota(jnp.int32, (nrow, 128), 0)
    c2 = lax.broadcasted_iota(jnp.int32, (nrow, 128), 1)
    mask2 = ((c2 < nt * SUB) & ((c2 & (SUB - 1)) == (r2 & (SUB - 1)))
             & ((c2 >> 3) <= (r2 >> 6)))
    lane_q = lax.broadcasted_iota(jnp.int32, (SUB, 128), 1)
    low = lane_q < HEAD_DIM
    fill1 = jnp.where(c1 == 0, sinkrow_ref[0], MASK_VALUE)
    lane_c = lax.broadcasted_iota(jnp.int32, (D_KV, WINDOW), 1)
    ones_c = jnp.ones((D_KV, SUB * WINDOW), BF16)
    ones_n = jnp.ones((128, D_KV), BF16)
    zeros_n = jnp.zeros((128 - nt * SUB, D_KV), F32)

    def sub_block(sb):
        b0 = sb * SUB
        pieces = []
        for tt in range(nt):
            for g in range(GROUP):
                qp = qs[pl.ds(tt * gb + b0, SUB), 128 * g:128 * (g + 1)]
                pieces.append(jnp.where(low, qp, 0.0))
                pieces.append(jnp.where(low, 0.0, qp))
        lhs = jnp.concatenate(pieces, axis=0).astype(BF16)
        kblk = ck_ref[0, pl.ds(b0, SUB)]
        vblk = cv_ref[0, pl.ds(b0, SUB)]
        kc = jnp.concatenate(
            [kblk[bb].reshape(D_KV, WINDOW) for bb in range(SUB)], axis=1).astype(BF16)
        vc = jnp.concatenate(
            [jnp.where(lane_c == 0, 0.0, vblk[0].reshape(D_KV, WINDOW))]
            + [vblk[bb].reshape(D_KV, WINDOW) for bb in range(1, SUB)], axis=1).astype(BF16)
        kn = jnp.concatenate(
            [kn_s[pl.ds(tt * gb + b0, SUB), :] for tt in range(nt)] + [zeros_n],
            axis=0).astype(BF16)
        vn = jnp.concatenate(
            [vn_s[pl.ds(tt * gb + b0, SUB), :] for tt in range(nt)] + [zeros_n],
            axis=0).astype(BF16)
        nt_dims = (((1,), (1,)), ((), ()))
        s1 = jnp.where(mask1, jnp.dot(lhs, kc, preferred_element_type=F32), fill1)
        s2 = jnp.where(mask2, lax.dot_general(lhs, kn, nt_dims, preferred_element_type=F32),
                       MASK_VALUE)
        s = jnp.concatenate([s1, s2], axis=1)
        p = jnp.exp(s - jnp.max(s, axis=-1, keepdims=True)).astype(BF16)
        tot = (lax.dot_general(p[:, 0:SUB * WINDOW], jnp.concatenate([vc, ones_c], axis=0),
                               nt_dims, preferred_element_type=F32)
               + jnp.dot(p[:, SUB * WINDOW:], jnp.concatenate([vn, ones_n], axis=1),
                         preferred_element_type=F32))
        o_all = tot[:, 0:D_KV] / tot[:, D_KV:]
        for tt in range(nt):
            for g in range(GROUP):
                r0 = ((tt * GROUP + g) * N_KV_HEADS) * SUB
                o = jnp.where(low, o_all[r0:r0 + SUB], o_all[r0 + SUB:r0 + 2 * SUB])
                ocat[pl.ds(tt * gb + b0, SUB), 128 * g:128 * (g + 1)] = o

    for sb in range(gb // SUB):
        for b in range(sb * SUB, (sb + 1) * SUB):
            roll_cache(b)
        sub_block(sb)
    o_nat = _ungroup_heads([ocat[:, 128 * g:128 * (g + 1)] for g in range(GROUP)])
    yc = jnp.concatenate(o_nat, axis=-1) * _silu(proj(O_ZA, D_IN))

    mixcat = jnp.concatenate([ya, yb, yc], axis=-1).astype(BF16)
    mix = jnp.dot(mixcat, wout_ref[0], preferred_element_type=F32).reshape(nt, gb, D_MODEL)
    y = ALPHA * x3 + (1.0 + gate)[None] * mix
    xo_ref[...] = _ln(y) * lng_ref[0][None] + lnb_ref[0][None]


def _decode_layer(lidx, x, mod_s, win, wout, conv_w, conv_b, cnorm_g, cnorm_b, wpool, pool_scale,
                  ln_g, ln_b, cos_s, sin_s, sink_rows, cache_conv, cache_pool, cache_k, cache_v,
                  rolled):
    nt, nbatch, _ = x.shape
    gb = DEC_GROUP
    carried = list(rolled)
    per_layer = lambda shape: pl.BlockSpec((1,) + shape, lambda g, l: (l[0],) + (0,) * len(shape))
    tok = lambda c: pl.BlockSpec((nt, gb, c), lambda g, l: (0, g, 0))
    time_major = lambda r, c: pl.BlockSpec((1, r, gb, c), lambda g, l: (l[0], 0, g, 0))
    kv_blk = pl.BlockSpec((1, gb, N_KV_HEADS, HEAD_DIM, WINDOW), lambda g, l: (l[0], g, 0, 0, 0))
    nrow = nt * GROUP * N_KV_HEADS * SUB
    grid_spec = pltpu.PrefetchScalarGridSpec(
        num_scalar_prefetch=1,
        grid=(nbatch // gb,),
        in_specs=[
            tok(D_MODEL),
            pl.BlockSpec((1, gb, 3 * D_MODEL), lambda g, l: (l[0], g, 0)),
            per_layer((D_MODEL, D_IN)),
            per_layer((D_MODEL, D_MODEL)),
            per_layer((CONV_WIDTH, D_CONV)),
            per_layer((1, D_CONV)),
            per_layer((1, D_CONV)),
            per_layer((1, D_CONV)),
            per_layer((D_POOL, D_POOL)),
            per_layer((1, D_POOL)),
            per_layer((1, D_MODEL)),
            per_layer((1, D_MODEL)),
            pl.BlockSpec((nt, 1, 128), lambda g, l: (0, 0, 0)),
            pl.BlockSpec((nt, 1, 128), lambda g, l: (0, 0, 0)),
            per_layer((nrow, 1)),
            time_major(CONV_BUF, D_CONV),
            time_major(POOL_BUF, D_POOL),
            kv_blk,
            kv_blk,
        ] + [pl.BlockSpec(memory_space=pl.ANY)] * len(carried),
        out_specs=[tok(D_MODEL), time_major(CONV_BUF, D_CONV), time_major(POOL_BUF, D_POOL),
                   kv_blk, kv_blk],
        scratch_shapes=[
            pltpu.VMEM((nt * gb, D_ATTN), F32),
            pltpu.VMEM((nt * gb, D_ATTN), F32),
            pltpu.VMEM((nt * gb, D_KV), F32),
            pltpu.VMEM((nt * gb, D_KV), F32),
            pltpu.VMEM((nt * gb, D_KV), F32),
            pltpu.VMEM((nt * gb, D_KV), F32),
            pltpu.VMEM((D_KV, nt * gb), F32),
            pltpu.VMEM((D_KV, nt * gb), F32),
        ],
    )
    out_shape = [jax.ShapeDtypeStruct((nt, nbatch, D_MODEL), F32)] + [
        jax.ShapeDtypeStruct(c.shape, F32) for c in (cache_conv, cache_pool, cache_k, cache_v)]
    operands = (lidx, x, mod_s, win, wout, conv_w, conv_b, cnorm_g, cnorm_b, wpool, pool_scale,
                ln_g, ln_b, cos_s, sin_s, sink_rows, cache_conv, cache_pool, cache_k, cache_v)
    n_in = len(operands)
    return pl.pallas_call(
        _decode_kernel,
        grid_spec=grid_spec,
        out_shape=out_shape,
        input_output_aliases={n_in + i: 1 + i for i in range(len(carried))},
        compiler_params=pltpu.CompilerParams(
            dimension_semantics=("arbitrary",), vmem_limit_bytes=VMEM_LIMIT),
    )(*operands, *carried)


def _rope_tables(pos):
    half = HEAD_DIM // 2
    inv_freq = ROPE_THETA ** (-jnp.arange(half, dtype=F32) * (2.0 / HEAD_DIM))
    ang = pos.astype(F32)[:, None] * inv_freq[None, :]
    cos = jnp.cos(ang)
    sin = jnp.sin(ang)
    cos_t = jnp.concatenate([cos, cos, cos, cos], axis=-1)
    sin_t = jnp.concatenate([-sin, sin, -sin, sin], axis=-1)
    return cos_t, sin_t


def kernel(x_prompt, x_sample, cache_conv, cache_pool, cache_k, cache_v, c_prompt, c_sample,
           w_in, w_out, conv_w, conv_b, cnorm_g, cnorm_b, pool_w, pool_scale, sinks,
           w_mod, b_mod, ln_g, ln_b):
    B, T, _ = x_prompt.shape
    nbatch, nt, _ = x_sample.shape

    win = w_in.astype(BF16)
    wout = w_out.astype(BF16)
    wpool = jnp.zeros((DEPTH, D_POOL, D_POOL), F32)
    for gi in range(len(POOL_WINDOWS)):
        sl = slice(gi * POOL_GROUP_DIM, (gi + 1) * POOL_GROUP_DIM)
        wpool = wpool.at[:, sl, sl].set(pool_w[:, gi])
    wpool = wpool.astype(BF16)
    sinks_gh = sinks.reshape(DEPTH, N_KV_HEADS, GROUP).transpose(0, 2, 1)
    nq = D_ATTN // 128
    order = ([2 * c + (2 * c) // GROUP for c in range(nq)]
             + [2 * c + 1 - (2 * c) // GROUP for c in range(nq)])
    sinks_ord = jnp.stack([sinks[:, hd] for hd in order], axis=1)
    sink_fill = jnp.concatenate(
        [sinks_ord[:, :, None, None] * LOG2E,
         jnp.full((DEPTH, N_HEADS, 1, 2 * BLOCK - 1), MASK_VALUE, F32)], axis=-1)
    sink_rows = jnp.broadcast_to(
        sinks_gh[:, None, :, :, None], (DEPTH, nt, GROUP, N_KV_HEADS, SUB)
    ).reshape(DEPTH, nt * GROUP * N_KV_HEADS * SUB, 1)

    r3 = lambda p: p.reshape(DEPTH, 1, -1)
    conv_b3, cng3, cnb3, pscale3, lng3, lnb3 = map(r3, (conv_b, cnorm_g, cnorm_b, pool_scale, ln_g, ln_b))

    cos_p, sin_p = _rope_tables(jnp.arange(T, dtype=jnp.int32))
    cos_s, sin_s = _rope_tables(PAST_LEN + jnp.arange(nt, dtype=jnp.int32))
    cos_s = cos_s.reshape(nt, 1, 128)
    sin_s = sin_s.reshape(nt, 1, 128)

    mod = _modulation(jnp.concatenate([c_sample, c_prompt], axis=0), w_mod, b_mod)
    mod_p = mod[:, nbatch:].reshape(DEPTH, B, 3, D_MODEL)

    cconv_t = cache_conv.transpose(0, 2, 1, 3)
    cpool_t = cache_pool.transpose(0, 2, 1, 3)
    ck_t = cache_k.transpose(0, 1, 3, 4, 2)
    cv_t = cache_v.transpose(0, 1, 3, 4, 2)

    xp = x_prompt
    xs = x_sample.transpose(1, 0, 2)
    conv_p, pool_p, k_p, v_p = ([] for _ in range(4))
    rolled = ()
    for l in range(DEPTH):
        lidx = jnp.full((1,), l, jnp.int32)
        xp, cp, pp, kp, vp = _prompt_layer(
            lidx, sink_fill, xp, mod_p, win, wout, conv_w, conv_b3, cng3, cnb3, wpool, pscale3,
            lng3, lnb3, cos_p, sin_p)
        xs, *rolled = _decode_layer(
            lidx, xs, mod, win, wout, conv_w, conv_b3, cng3, cnb3, wpool, pscale3,
            lng3, lnb3, cos_s, sin_s, sink_rows, cconv_t, cpool_t, ck_t, cv_t, rolled)
        conv_p.append(cp); pool_p.append(pp); k_p.append(kp); v_p.append(vp)

    kv5 = lambda z: z.reshape(z.shape[:-1] + (N_KV_HEADS, HEAD_DIM))
    conv_s, pool_s, k_s, v_s = rolled
    return (xp, xs.transpose(1, 0, 2),
            jnp.stack(conv_p), jnp.stack(pool_p), kv5(jnp.stack(k_p)), kv5(jnp.stack(v_p)),
            conv_s.transpose(0, 2, 1, 3), pool_s.transpose(0, 2, 1, 3),
            k_s.transpose(0, 1, 4, 2, 3), v_s.transpose(0, 1, 4, 2, 3))
```
